```python
import math
import jax, jax.numpy as jnp
from jax import lax
import numpy as np

D_MODEL = 1024
BATCH = 1
SEQ = 16384
DEPTH = 2
DEC_BATCH = 32
DEC_SEQ = 4
PAST_LEN = 16384
PAGE_SIZE = 128

N_EVEN = (DEPTH + 1) // 2
N_ODD = DEPTH // 2
D_A = D_MODEL // 2
H_A = 4
DK_A = D_A // H_A
DV_A = DK_A
MLSTM_CHUNK = 128
F_BIAS = 3.0
D_B = D_MODEL - D_A
POOL_WINDOWS = (2, 4, 8, 16)
N_POOL = len(POOL_WINDOWS)
G_B = D_B // N_POOL
POOL_BUF = max(POOL_WINDOWS) - 1
E_IN = 4 * D_A + 2 * H_A + D_B
H_C = 8
DC = D_MODEL // (2 * H_C)
DV_C = 2 * DC
Q_BLOCK = 128
N_BUCKETS = 32
MAX_DIST = 128
D_FF = ((8 * D_MODEL // 3 + 127) // 128) * 128
CONV_W = 3

kernel_name = "hybrid_mlstm_pool_diffattn_convffn_step"


def rmsnorm(x, g, eps=1e-6):
    xf = x.astype(jnp.float32)
    y = xf * lax.rsqrt(jnp.mean(xf * xf, axis=-1, keepdims=True) + eps)
    return (y * g.astype(jnp.float32)).astype(x.dtype)


def mlstm_chunk(carry, xs):
    C, n, m0 = carry
    q, k, v, ig, lf = [a.astype(jnp.float32) for a in xs]
    L = q.shape[1]
    b = jnp.transpose(jnp.cumsum(lf, axis=1), (0, 2, 1))
    ig = jnp.transpose(ig, (0, 2, 1))
    causal = jnp.tril(jnp.ones((L, L), dtype=bool))
    logD = jnp.where(causal, b[..., :, None] - b[..., None, :] + ig[..., None, :], -jnp.inf)
    log_inter = b + m0[..., None]
    m_t = jnp.maximum(log_inter, jnp.max(logD, axis=-1))
    Dm = jnp.exp(logD - m_t[..., None])
    w_inter = jnp.exp(log_inter - m_t)
    S = jnp.einsum('blhd,bshd->bhls', q, k) * Dm
    num = jnp.einsum('bhls,bshv->blhv', S, v) + jnp.transpose(w_inter, (0, 2, 1))[..., None] * jnp.einsum('bhvd,blhd->blhv', C, q)
    ndot = jnp.sum(S, axis=-1) + w_inter * jnp.einsum('bhd,blhd->bhl', n, q)
    denom = jnp.maximum(jnp.abs(ndot), jnp.exp(-m_t))
    h = num / jnp.transpose(denom, (0, 2, 1))[..., None]
    m_new = m_t[..., -1]
    w_s = jnp.exp(b[..., -1:] - b + ig - m_new[..., None])
    decay = jnp.exp(log_inter[..., -1] - m_new)
    C_new = decay[..., None, None] * C + jnp.einsum('bhs,bshv,bshd->bhvd', w_s, v, k)
    n_new = decay[..., None] * n + jnp.einsum('bhs,bshd->bhd', w_s, k)
    return (C_new, n_new, m_new), h


def mlstm_scan(q, k, v, ig, lf, C0, n0, m0):
    B, T = q.shape[:2]
    L = MLSTM_CHUNK if T % MLSTM_CHUNK == 0 else T
    NC = T // L

    def to_chunks(a):
        return jnp.moveaxis(a.reshape((B, NC, L) + a.shape[2:]), 1, 0)

    carry0 = (C0.astype(jnp.float32), n0.astype(jnp.float32), m0.astype(jnp.float32))
    (C, n, m), hs = lax.scan(mlstm_chunk, carry0, tuple(to_chunks(a) for a in (q, k, v, ig, lf)))
    h = jnp.moveaxis(hs, 0, 1).reshape(B, T, H_A, DV_A)
    return h, (C, n, m)


def pool_mix(u_prev, u, pos, w_pool, pool_scale):
    B, T, _ = u.shape
    P = POOL_BUF
    ext = jnp.concatenate([u_prev.astype(u.dtype), u], axis=1).astype(jnp.float32)
    cs = jnp.concatenate([jnp.zeros((B, 1, D_B), jnp.float32), jnp.cumsum(ext, axis=1)], axis=1)
    end = cs[:, P + 1:]
    outs = []
    for g, w in enumerate(POOL_WINDOWS):
        sl = slice(g * G_B, (g + 1) * G_B)
        win_sum = end[..., sl] - cs[:, P + 1 - w:P + 1 - w + T, sl]
        count = jnp.minimum(pos + 1, w).astype(jnp.float32)[None, :, None]
        outs.append(win_sum / count - ext[:, P:, sl])
    pooled = jnp.concatenate(outs, axis=-1).reshape(B, T, N_POOL, G_B)
    y = jnp.einsum('btgc,gcd->btgd', pooled, w_pool.astype(jnp.float32)).reshape(B, T, D_B) * pool_scale
    return y, ext[:, -P:]


def even_mixer(h, C0, n0, m0, pool_prev, pos, w_in, b_gate, g_head, w_pool, pool_scale, w_out):
    B, T, _ = h.shape
    z = h @ w_in
    q = z[..., :D_A].reshape(B, T, H_A, DK_A)
    k = z[..., D_A:2 * D_A].reshape(B, T, H_A, DK_A) * (DK_A ** -0.5)
    v = z[..., 2 * D_A:3 * D_A].reshape(B, T, H_A, DV_A)
    og = z[..., 3 * D_A:4 * D_A].reshape(B, T, H_A, DV_A)
    gates = z[..., 4 * D_A:4 * D_A + 2 * H_A].astype(jnp.float32) + b_gate.astype(jnp.float32)
    u = z[..., 4 * D_A + 2 * H_A:]
    ig = gates[..., :H_A]
    lf = jax.nn.log_sigmoid(gates[..., H_A:])
    hh, (C, n, m) = mlstm_scan(q, k, v, ig, lf, C0, n0, m0)
    hh = rmsnorm(hh, g_head.reshape(H_A, DV_A)) * jax.nn.sigmoid(og.astype(jnp.float32))
    yb, pool_new = pool_mix(pool_prev, u, pos, w_pool, pool_scale)
    y = jnp.concatenate([hh.reshape(B, T, D_A).astype(h.dtype), yb.astype(h.dtype)], axis=-1) @ w_out
    return y, (C, n, m, pool_new)


def t5_bucket(rel):
    n = jnp.maximum(rel, 0)
    max_exact = N_BUCKETS // 2
    nf = jnp.maximum(n, 1).astype(jnp.float32)
    large = max_exact + (jnp.log(nf / max_exact) / math.log(MAX_DIST / max_exact) * (N_BUCKETS - max_exact)).astype(jnp.int32)
    large = jnp.minimum(large, N_BUCKETS - 1)
    return jnp.where(n < max_exact, n, large)


def diff_attend(qb, k, v, q_pos, k_pos, rel_bias, lam):
    B, S = k.shape[:2]
    kk = k.reshape(B, S, H_C, 2, DC)
    s = jnp.einsum('bqhmd,bkhmd->bmhqk', qb, kk).astype(jnp.float32) * (DC ** -0.5)
    rel = q_pos[:, None] - k_pos[None, :]
    bias = jnp.transpose(rel_bias[t5_bucket(rel)], (2, 0, 1)).astype(jnp.float32)
    s = jnp.where(rel >= 0, s + bias, -jnp.inf)
    p = jax.nn.softmax(s, axis=-1)
    a = p[:, 0] - lam * p[:, 1]
    return jnp.einsum('bhqk,bkhv->bqhv', a.astype(v.dtype), v)


def diff_attn_prompt(q, k, v, rel_bias, lam):
    B, T = q.shape[:2]
    NB = T // Q_BLOCK
    qb = jnp.moveaxis(q.reshape(B, NB, Q_BLOCK, H_C, 2, DC), 1, 0)
    pos = jnp.arange(T, dtype=jnp.int32)
    qpos = pos.reshape(NB, Q_BLOCK)
    o = lax.map(lambda a: diff_attend(a[0], k, v, a[1], pos, rel_bias, lam), (qb, qpos))
    return jnp.moveaxis(o, 0, 1).reshape(B, T, H_C, DV_C)


def diff_attn_sample(q, k, v, cache_k, cache_v, page_table, rel_bias, lam):
    T = q.shape[1]
    n_pages = PAST_LEN // PAGE_SIZE
    k_pos = jnp.arange(PAST_LEN + T, dtype=jnp.int32)
    q_pos = PAST_LEN + jnp.arange(T, dtype=jnp.int32)

    def one(args):
        qi, ki, vi, pt = args
        kp = cache_k[pt].reshape(n_pages * PAGE_SIZE, H_C, 2 * DC)
        vp = cache_v[pt].reshape(n_pages * PAGE_SIZE, H_C, DV_C)
        kf = jnp.concatenate([kp.astype(ki.dtype), ki], axis=0)[None]
        vf = jnp.concatenate([vp.astype(vi.dtype), vi], axis=0)[None]
        return diff_attend(qi[None], kf, vf, q_pos, k_pos, rel_bias, lam)[0]

    return lax.map(one, (q, k, v, page_table))


def diff_qkv(h, w_in):
    B, T, _ = h.shape
    z = h @ w_in
    q = z[..., :D_MODEL].reshape(B, T, H_C, 2, DC)
    k = z[..., D_MODEL:2 * D_MODEL].reshape(B, T, H_C, 2 * DC)
    v = z[..., 2 * D_MODEL:].reshape(B, T, H_C, DV_C)
    return q, k, v


def diff_out(o, gain, lam_init, w_out):
    B, T = o.shape[:2]
    o = rmsnorm(o, gain) * (1.0 - lam_init)
    return o.reshape(B, T, H_C * DV_C) @ w_out


def conv_ffn(h, prev, w_up, conv_w, conv_b, w_down):
    T = h.shape[1]
    ab = h @ w_up
    a, g = ab[..., :D_FF], ab[..., D_FF:]
    ext = jnp.concatenate([prev.astype(a.dtype), a], axis=1)
    y = conv_b
    for j in range(CONV_W):
        y = y + conv_w[j] * ext[:, j:j + T]
    out = (jax.nn.gelu(y) * g) @ w_down
    return out, ext[:, -(CONV_W - 1):]


def setup_inputs(seed: int = 0) -> dict:
    key = jax.random.key(seed)
    ks = jax.random.split(key, 40)
    f32 = jnp.float32
    nrm = lambda i, shape, s=1.0: jax.random.normal(ks[i], shape, f32) * s
    n_pages = PAST_LEN // PAGE_SIZE
    n_phys = (DEC_BATCH * n_pages * 5) // 4
    perm = jax.random.permutation(ks[0], n_phys)[:DEC_BATCH * n_pages]
    page_table = perm.reshape(DEC_BATCH, n_pages).astype(jnp.int32)
    b_gate_e = jnp.concatenate([nrm(30, (N_EVEN, H_A), 0.1), F_BIAS + nrm(31, (N_EVEN, H_A), 0.1)], axis=-1)
    return {
        "x_prompt": nrm(1, (BATCH, SEQ, D_MODEL)),
        "x_sample": nrm(2, (DEC_BATCH, DEC_SEQ, D_MODEL)),
        "state_mlstm_C": nrm(3, (N_EVEN, DEC_BATCH, H_A, DV_A, DK_A), 0.1),
        "state_mlstm_n": nrm(4, (N_EVEN, DEC_BATCH, H_A, DK_A), 0.1),
        "state_mlstm_m": nrm(5, (N_EVEN, DEC_BATCH, H_A)),
        "state_pool": nrm(6, (N_EVEN, DEC_BATCH, POOL_BUF, D_B)),
        "cache_k": nrm(7, (N_ODD, n_phys, PAGE_SIZE, H_C, 2 * DC)),
        "cache_v": nrm(8, (N_ODD, n_phys, PAGE_SIZE, H_C, DV_C)),
        "state_ffn_conv": nrm(9, (DEPTH, DEC_BATCH, CONV_W - 1, D_FF)),
        "page_table": page_table,
        "norm_mix": 1.0 + nrm(10, (DEPTH, D_MODEL), 0.02),
        "norm_ffn": 1.0 + nrm(11, (DEPTH, D_MODEL), 0.02),
        "norm_final": 1.0 + nrm(12, (D_MODEL,), 0.02),
        "w_in_e": nrm(13, (N_EVEN, D_MODEL, E_IN), D_MODEL ** -0.5),
        "b_gate_e": b_gate_e,
        "mlstm_gain": 1.0 + nrm(14, (N_EVEN, D_A), 0.02),
        "w_pool": nrm(15, (N_EVEN, N_POOL, G_B, G_B), G_B ** -0.5),
        "pool_scale": 1.0 + nrm(16, (N_EVEN, D_B), 0.02),
        "w_out_e": nrm(17, (N_EVEN, D_A + D_B, D_MODEL), (D_A + D_B) ** -0.5),
        "w_in_o": nrm(18, (N_ODD, D_MODEL, 3 * D_MODEL), D_MODEL ** -0.5),
        "lambda_q1": nrm(19, (N_ODD, DC), 0.1),
        "lambda_k1": nrm(20, (N_ODD, DC), 0.1),
        "lambda_q2": nrm(21, (N_ODD, DC), 0.1),
        "lambda_k2": nrm(22, (N_ODD, DC), 0.1),
        "subln_gain": 1.0 + nrm(23, (N_ODD, DV_C), 0.02),
        "rel_bias": nrm(24, (N_BUCKETS, H_C), 0.3),
        "w_out_o": nrm(25, (N_ODD, H_C * DV_C, D_MODEL), (H_C * DV_C) ** -0.5),
        "w_up": nrm(26, (DEPTH, D_MODEL, 2 * D_FF), D_MODEL ** -0.5),
        "conv_w": nrm(27, (DEPTH, CONV_W, D_FF), CONV_W ** -0.5),
        "conv_b": nrm(28, (DEPTH, D_FF), 0.02),
        "w_down": nrm(29, (DEPTH, D_FF, D_MODEL), D_FF ** -0.5),
    }


def reference(x_prompt, x_sample, state_mlstm_C, state_mlstm_n, state_mlstm_m, state_pool, cache_k, cache_v, state_ffn_conv, page_table, norm_mix, norm_ffn, norm_final, w_in_e, b_gate_e, mlstm_gain, w_pool, pool_scale, w_out_e, w_in_o, lambda_q1, lambda_k1, lambda_q2, lambda_k2, subln_gain, rel_bias, w_out_o, w_up, conv_w, conv_b, w_down):
    xp, xs = x_prompt, x_sample
    Bp, Tp = xp.shape[:2]
    pos_p = jnp.arange(Tp, dtype=jnp.int32)
    pos_s = PAST_LEN + jnp.arange(xs.shape[1], dtype=jnp.int32)
    Cp_l, np_l, mp_l, pp_l, kp_l, vp_l, cp_l = [], [], [], [], [], [], []
    Cs_l, ns_l, ms_l, ps_l, ks_l, vs_l, cs_l = [], [], [], [], [], [], []
    for l in range(DEPTH):
        hp = rmsnorm(xp, norm_mix[l])
        hs = rmsnorm(xs, norm_mix[l])
        if l % 2 == 0:
            e = l // 2
            zC = jnp.zeros((Bp, H_A, DV_A, DK_A), jnp.float32)
            zn = jnp.zeros((Bp, H_A, DK_A), jnp.float32)
            zm = jnp.zeros((Bp, H_A), jnp.float32)
            zpool = jnp.zeros((Bp, POOL_BUF, D_B), xp.dtype)
            yp, (C_p, n_p, m_p, pool_p) = even_mixer(hp, zC, zn, zm, zpool, pos_p, w_in_e[e], b_gate_e[e], mlstm_gain[e], w_pool[e], pool_scale[e], w_out_e[e])
            ys, (C_s, n_s, m_s, pool_s) = even_mixer(hs, state_mlstm_C[e], state_mlstm_n[e], state_mlstm_m[e], state_pool[e], pos_s, w_in_e[e], b_gate_e[e], mlstm_gain[e], w_pool[e], pool_scale[e], w_out_e[e])
            Cp_l.append(C_p); np_l.append(n_p); mp_l.append(m_p); pp_l.append(pool_p)
            Cs_l.append(C_s); ns_l.append(n_s); ms_l.append(m_s); ps_l.append(pool_s)
        else:
            o = l // 2
            lam_init = 0.8 - 0.6 * math.exp(-0.3 * l)
            lam = (jnp.exp(jnp.sum(lambda_q1[o] * lambda_k1[o])) - jnp.exp(jnp.sum(lambda_q2[o] * lambda_k2[o])) + lam_init).astype(jnp.float32)
            q, k, v = diff_qkv(hp, w_in_o[o])
            yp = diff_out(diff_attn_prompt(q, k, v, rel_bias, lam), subln_gain[o], lam_init, w_out_o[o])
            kp_l.append(k); vp_l.append(v)
            q, k, v = diff_qkv(hs, w_in_o[o])
            ys = diff_out(diff_attn_sample(q, k, v, cache_k[o], cache_v[o], page_table, rel_bias, lam), subln_gain[o], lam_init, w_out_o[o])
            ks_l.append(k); vs_l.append(v)
        xp = xp + yp.astype(xp.dtype)
        xs = xs + ys.astype(xs.dtype)
        fp, conv_p = conv_ffn(rmsnorm(xp, norm_ffn[l]), jnp.zeros((Bp, CONV_W - 1, D_FF), xp.dtype), w_up[l], conv_w[l], conv_b[l], w_down[l])
        fs, conv_s = conv_ffn(rmsnorm(xs, norm_ffn[l]), state_ffn_conv[l], w_up[l], conv_w[l], conv_b[l], w_down[l])
        cp_l.append(conv_p); cs_l.append(conv_s)
        xp = xp + fp.astype(xp.dtype)
        xs = xs + fs.astype(xs.dtype)
    y_prompt = rmsnorm(xp, norm_final)
    y_sample = rmsnorm(xs, norm_final)
    new_C_p = jnp.stack(Cp_l); new_n_p = jnp.stack(np_l); new_m_p = jnp.stack(mp_l); new_pool_p = jnp.stack(pp_l)
    new_k_p = jnp.stack(kp_l); new_v_p = jnp.stack(vp_l); new_conv_p = jnp.stack(cp_l)
    new_C_s = jnp.stack(Cs_l); new_n_s = jnp.stack(ns_l); new_m_s = jnp.stack(ms_l); new_pool_s = jnp.stack(ps_l)
    new_k_s = jnp.stack(ks_l); new_v_s = jnp.stack(vs_l); new_conv_s = jnp.stack(cs_l)
    return (y_prompt, y_sample, new_C_p, new_n_p, new_m_p, new_pool_p, new_k_p, new_v_p, new_conv_p, new_C_s, new_n_s, new_m_s, new_pool_s, new_k_s, new_v_s, new_conv_s)
```

```python
import functools
import math

import numpy as np
import jax
import jax.numpy as jnp
from jax import lax
from jax.experimental import pallas as pl
from jax.experimental.pallas import tpu as pltpu

F32 = jnp.float32
BF16 = jnp.bfloat16
HIGHEST = lax.Precision.HIGHEST

D_MODEL = 1024
PAST_LEN = 16384
PAGE_SIZE = 128
D_A = 512
H_A = 4
DK_A = 128
MLSTM_CHUNK = 128
D_B = 512
POOL_WINDOWS = (2, 4, 8, 16)
G_B = 128
POOL_BUF = 15
POOL_HDR = 16
H_C = 8
DC = 64
DV_C = 128
N_BUCKETS = 32
MAX_DIST = 128
D_FF = 2816
CONV_W = 3
EPS = 1e-6

VMEM_LIMIT = 56 * 1024 * 1024
NEG_INF = float("-inf")


def _cparams(sem):
    return pltpu.CompilerParams(dimension_semantics=sem, vmem_limit_bytes=VMEM_LIMIT)


def _const_spec(shape):
    nd = len(shape)
    return pl.BlockSpec(shape, lambda *_: (0,) * nd, pipeline_mode=pl.Buffered(1))


def _rms(x, g):
    return x * lax.rsqrt(jnp.mean(x * x, axis=-1, keepdims=True) + EPS) * g


def _dot(a, b):
    return jnp.dot(a, b, preferred_element_type=F32)


def _dot_nt(a, b):
    return lax.dot_general(a, b, (((1,), (1,)), ((), ())), preferred_element_type=F32)


def _dot_tn(a, b):
    return lax.dot_general(a, b, (((0,), (0,)), ((), ())), preferred_element_type=F32)


def _log_sigmoid(x):
    return jnp.minimum(x, 0.0) - jnp.log1p(jnp.exp(-jnp.abs(x)))


def _sigmoid(x):
    return 1.0 / (1.0 + jnp.exp(-x))


def _proj_even_kernel(x_ref, g_ref, w_ref, wgc_ref, wgr_ref, bc_ref, br_ref, qkv_ref, ogu_ref, gc_ref, gr_ref):
    h = _rms(x_ref[...], g_ref[...])
    z = _dot(h.astype(BF16), w_ref[...])
    qkv_ref[:, 0:D_A] = z[:, 0:D_A].astype(BF16)
    qkv_ref[:, D_A:2 * D_A] = (z[:, D_A:2 * D_A] * (DK_A ** -0.5)).astype(BF16)
    qkv_ref[:, 2 * D_A:3 * D_A] = z[:, 2 * D_A:3 * D_A].astype(BF16)
    ogu_ref[...] = z[:, 3 * D_A:]
    gc_ref[...] = jnp.dot(h, wgc_ref[...], precision=HIGHEST, preferred_element_type=F32) + bc_ref[...]
    gr_ref[...] = lax.dot_general(wgr_ref[...], h, (((1,), (1,)), ((), ())), precision=HIGHEST,
                                  preferred_element_type=F32) + br_ref[...]


def proj_even(x, g, w_main, wg_col, wg_row, b_col, b_row, tm):
    M = x.shape[0]
    n_main = w_main.shape[1]
    return pl.pallas_call(
        _proj_even_kernel,
        grid=(M // tm,),
        in_specs=[
            pl.BlockSpec((tm, D_MODEL), lambda i: (i, 0)),
            _const_spec((1, D_MODEL)),
            _const_spec((D_MODEL, n_main)),
            _const_spec((D_MODEL, 128)),
            _const_spec((8, D_MODEL)),
            _const_spec((1, 128)),
            _const_spec((8, 1)),
        ],
        out_specs=[
            pl.BlockSpec((tm, 3 * D_A), lambda i: (i, 0)),
            pl.BlockSpec((tm, D_A + D_B), lambda i: (i, 0)),
            pl.BlockSpec((tm, 128), lambda i: (i, 0)),
            pl.BlockSpec((8, tm), lambda i: (0, i)),
        ],
        out_shape=[
            jax.ShapeDtypeStruct((M, 3 * D_A), BF16),
            jax.ShapeDtypeStruct((M, D_A + D_B), F32),
            jax.ShapeDtypeStruct((M, 128), F32),
            jax.ShapeDtypeStruct((8, M), F32),
        ],
        compiler_params=_cparams(("arbitrary",)),
        name="proj_even",
    )(x, g, w_main, wg_col, wg_row, b_col, b_row)


def _mlstm_kernel(qkv_ref, og_ref, gc_ref, gr_ref, c0_ref, n0_ref, m0_ref, gain_ref,
                  hh_ref, c_out_ref, n_out_ref, m_out_ref, c_s, n_s, m_s, *, valid):
    L = MLSTM_CHUNK
    c = pl.program_id(1)

    @pl.when(c == 0)
    def _():
        c_s[...] = c0_ref[0]
        n_s[...] = n0_ref[0]
        m_s[...] = m0_ref[0]

    row = lax.broadcasted_iota(jnp.int32, (L, L), 0)
    col = lax.broadcasted_iota(jnp.int32, (L, L), 1)
    tri = (col <= row).astype(F32)
    mask = (col <= row) & (col < valid)
    rvalid = lax.broadcasted_iota(jnp.int32, (L, 1), 0) < valid
    cvalid = lax.broadcasted_iota(jnp.int32, (1, L), 1) < valid

    gcol = gc_ref[0]
    grow = gr_ref[0]
    lf_col = jnp.where(rvalid, _log_sigmoid(gcol), 0.0)
    lf_row = jnp.where(cvalid, _log_sigmoid(grow), 0.0)
    b_col_all = jnp.dot(tri, lf_col, precision=HIGHEST, preferred_element_type=F32)
    b_row_all = lax.dot_general(lf_row, tri, (((1,), (1,)), ((), ())), precision=HIGHEST,
                                preferred_element_type=F32)

    for h in range(H_A):
        bc = b_col_all[:, H_A + h:H_A + h + 1]
        br = b_row_all[H_A + h:H_A + h + 1, :]
        igc = gcol[:, h:h + 1]
        igr = grow[h:h + 1, :]
        m0 = m_s[h:h + 1, 0:1]
        logd = jnp.where(mask, bc - br + igr, NEG_INF)
        log_inter = bc + m0
        m_t = jnp.maximum(log_inter, jnp.max(logd, axis=-1, keepdims=True))
        dm = jnp.exp(logd - m_t)
        w_inter = jnp.exp(log_inter - m_t)
        q = qkv_ref[0, :, h * DK_A:(h + 1) * DK_A]
        k = qkv_ref[0, :, D_A + h * DK_A:D_A + (h + 1) * DK_A]
        v = qkv_ref[0, :, 2 * D_A + h * DK_A:2 * D_A + (h + 1) * DK_A]
        s = _dot_nt(q, k) * dm
        c_old = c_s[h]
        n_old = n_s[h:h + 1, :]
        num = _dot(s.astype(BF16), v) + w_inter * _dot_nt(q, c_old.astype(BF16))
        qn = jnp.sum(q.astype(F32) * n_old, axis=-1, keepdims=True)
        ndot = jnp.sum(s, axis=-1, keepdims=True) + w_inter * qn
        denom = jnp.maximum(jnp.abs(ndot), jnp.exp(-m_t))
        hh = num / denom
        y = _rms(hh, gain_ref[:, h * DK_A:(h + 1) * DK_A]) * _sigmoid(og_ref[0, :, h * DK_A:(h + 1) * DK_A])
        hh_ref[0, :, h * DK_A:(h + 1) * DK_A] = y.astype(BF16)
        m_new = m_t[valid - 1:valid, :]
        b_last = bc[valid - 1:valid, :]
        w_s = jnp.where(rvalid, jnp.exp(b_last - bc + igc - m_new), 0.0)
        decay = jnp.exp(b_last + m0 - m_new)
        kf = k.astype(F32)
        vw = (v.astype(F32) * w_s).astype(BF16)
        c_s[h] = decay * c_old + _dot_tn(vw, k)
        n_s[h:h + 1, :] = decay * n_old + jnp.sum(kf * w_s, axis=0, keepdims=True)
        m_s[h:h + 1, :] = jnp.broadcast_to(m_new, (1, 128))

    @pl.when(c == pl.num_programs(1) - 1)
    def _():
        c_out_ref[0] = c_s[...]
        n_out_ref[0] = n_s[...]
        m_out_ref[0] = m_s[...]


def mlstm(qkv, ogu, gc, gr, c0, n0, m0, gain, valid):
    B, T = qkv.shape[:2]
    L = MLSTM_CHUNK
    nc = T // L
    return pl.pallas_call(
        functools.partial(_mlstm_kernel, valid=valid),
        grid=(B, nc),
        in_specs=[
            pl.BlockSpec((1, L, 3 * D_A), lambda b, c: (b, c, 0)),
            pl.BlockSpec((1, L, D_A), lambda b, c: (b, c, 0)),
            pl.BlockSpec((1, L, 128), lambda b, c: (b, c, 0)),
            pl.BlockSpec((1, 8, L), lambda b, c: (b, 0, c)),
            pl.BlockSpec((1, H_A, DK_A, DK_A), lambda b, c: (b, 0, 0, 0)),
            pl.BlockSpec((1, H_A, DK_A), lambda b, c: (b, 0, 0)),
            pl.BlockSpec((1, 8, 128), lambda b, c: (b, 0, 0)),
            pl.BlockSpec((1, D_A), lambda b, c: (0, 0)),
        ],
        out_specs=[
            pl.BlockSpec((1, L, D_A), lambda b, c: (b, c, 0)),
            pl.BlockSpec((1, H_A, DK_A, DK_A), lambda b, c: (b, 0, 0, 0)),
            pl.BlockSpec((1, H_A, DK_A), lambda b, c: (b, 0, 0)),
            pl.BlockSpec((1, 8, 128), lambda b, c: (b, 0, 0)),
        ],
        out_shape=[
            jax.ShapeDtypeStruct((B, T, D_A), BF16),
            jax.ShapeDtypeStruct((B, H_A, DK_A, DK_A), F32),
            jax.ShapeDtypeStruct((B, H_A, DK_A), F32),
            jax.ShapeDtypeStruct((B, 8, 128), F32),
        ],
        scratch_shapes=[
            pltpu.VMEM((H_A, DK_A, DK_A), F32),
            pltpu.VMEM((H_A, DK_A), F32),
            pltpu.VMEM((8, 128), F32),
        ],
        compiler_params=_cparams(("arbitrary", "arbitrary")),
        name="mlstm",
    )(qkv, ogu, gc, gr, c0, n0, m0, gain)


def _pool_out_kernel(hh_ref, u_ref, prev_ref, wp_ref, ps_ref, wo_ref, x_ref, o_ref, e_s, *, tm, nt, pos0):
    t = pl.program_id(1)
    H = POOL_HDR

    @pl.when(t == 0)
    def _():
        e_s[0:H, :] = prev_ref[0]

    if nt > 1:
        @pl.when(t > 0)
        def _():
            e_s[0:H, :] = e_s[tm:tm + H, :]

    e_s[H:H + tm, :] = u_ref[0]
    pos = pos0 + t * tm + lax.broadcasted_iota(jnp.int32, (tm, 1), 0)
    ys = []
    for g, w in enumerate(POOL_WINDOWS):
        sl = slice(g * G_B, (g + 1) * G_B)
        cur = e_s[H:H + tm, sl]
        win = cur
        for j in range(1, w):
            win = win + e_s[H - j:H - j + tm, sl]
        cnt = jnp.minimum(pos + 1, w).astype(F32)
        pooled = win / cnt - cur
        ys.append(_dot(pooled.astype(BF16), wp_ref[g]))
    yb = jnp.concatenate(ys, axis=-1) * ps_ref[...]
    o_ref[0] = x_ref[0] + _dot(hh_ref[0], wo_ref[0:D_A, :]) + _dot(yb.astype(BF16), wo_ref[D_A:, :])


def pool_out(hh, ogu, prev16, w_pool, pool_scale, w_out, x, tm, pos0):
    B, T = x.shape[:2]
    nt = T // tm
    return pl.pallas_call(
        functools.partial(_pool_out_kernel, tm=tm, nt=nt, pos0=pos0),
        grid=(B, nt),
        in_specs=[
            pl.BlockSpec((1, tm, D_A), lambda b, t: (b, t, 0)),
            pl.BlockSpec((1, tm, D_B), lambda b, t: (b, t, 1)),
            pl.BlockSpec((1, POOL_HDR, D_B), lambda b, t: (b, 0, 0)),
            _const_spec((len(POOL_WINDOWS), G_B, G_B)),
            _const_spec((1, D_B)),
            _const_spec((D_A + D_B, D_MODEL)),
            pl.BlockSpec((1, tm, D_MODEL), lambda b, t: (b, t, 0)),
        ],
        out_specs=pl.BlockSpec((1, tm, D_MODEL), lambda b, t: (b, t, 0)),
        out_shape=jax.ShapeDtypeStruct((B, T, D_MODEL), F32),
        scratch_shapes=[pltpu.VMEM((POOL_HDR + tm, D_B), F32)],
        compiler_params=_cparams(("arbitrary", "arbitrary")),
        name="pool_out",
    )(hh, ogu, prev16, w_pool, pool_scale, w_out, x)


def _gelu_tanh(y):
    return 0.5 * y * (1.0 + jnp.tanh(math.sqrt(2.0 / math.pi) * (y + 0.044715 * (y * y * y))))


def _ffn_kernel(x_ref, g_ref, wup_ref, cw_ref, cb_ref, wdn_ref, p1_ref, p2_ref, gf_ref,
                o_ref, st_ref, carry_s, *, tm, seq_len, carried, final_norm):
    i = pl.program_id(0)
    x = x_ref[...]
    h = _rms(x, g_ref[...]).astype(BF16)
    a = _dot(h, wup_ref[:, 0:D_FF])
    gate = _dot(h, wup_ref[:, D_FF:])
    t = lax.broadcasted_iota(jnp.int32, (tm, 1), 0) % seq_len
    s1 = jnp.where(t >= 1, pltpu.roll(a, 1, 0), 0.0)
    s2 = jnp.where(t >= 2, pltpu.roll(a, 2, 0), 0.0)
    if carried:
        @pl.when(i == 0)
        def _():
            carry_s[...] = jnp.zeros_like(carry_s)
        prev0 = carry_s[6:7, :]
        prev1 = carry_s[7:8, :]
        s1 = s1 + jnp.where(t == 0, prev1, 0.0)
        s2 = s2 + jnp.where(t == 0, prev0, 0.0) + jnp.where(t == 1, prev1, 0.0)
        carry_s[...] = a[tm - 8:tm, :]
        st_ref[...] = a[tm - 8:tm, :]
    else:
        s1 = s1 + p1_ref[...]
        s2 = s2 + p2_ref[...]
        st_ref[...] = a
    y = cb_ref[...] + cw_ref[0:1, :] * s2 + cw_ref[1:2, :] * s1 + cw_ref[2:3, :] * a
    act = (_gelu_tanh(y) * gate).astype(BF16)
    out = x + _dot(act, wdn_ref[...])
    if final_norm:
        out = _rms(out, gf_ref[...])
    o_ref[...] = out


def ffn(x, g, w_up, conv_w, conv_b, w_down, p1, p2, g_final, tm, seq_len, carried, final_norm):
    M = x.shape[0]
    st_rows = 8 if carried else tm
    st_total = 8 if carried else M
    row_spec = lambda n: pl.BlockSpec((tm, n), lambda i: (i, 0))
    p_spec = _const_spec((8, D_FF)) if carried else row_spec(D_FF)
    return pl.pallas_call(
        functools.partial(_ffn_kernel, tm=tm, seq_len=seq_len, carried=carried, final_norm=final_norm),
        grid=(M // tm,),
        in_specs=[
            row_spec(D_MODEL),
            _const_spec((1, D_MODEL)),
            _const_spec((D_MODEL, 2 * D_FF)),
            _const_spec((CONV_W, D_FF)),
            _const_spec((1, D_FF)),
            _const_spec((D_FF, D_MODEL)),
            p_spec,
            p_spec,
            _const_spec((1, D_MODEL)),
        ],
        out_specs=[
            row_spec(D_MODEL),
            pl.BlockSpec((st_rows, D_FF), (lambda i: (0, 0)) if carried else (lambda i: (i, 0))),
        ],
        out_shape=[
            jax.ShapeDtypeStruct((M, D_MODEL), F32),
            jax.ShapeDtypeStruct((st_total, D_FF), F32),
        ],
        scratch_shapes=[pltpu.VMEM((8, D_FF), F32)],
        compiler_params=_cparams(("arbitrary",)),
        name="ffn",
    )(x, g, w_up, conv_w, conv_b, w_down, p1, p2, g_final)


def _proj_odd_kernel(x_ref, g_ref, w_ref, q_ref, kf_ref, vf_ref, kb_ref, vb_ref):
    h = _rms(x_ref[...], g_ref[...]).astype(BF16)
    q = _dot(h, w_ref[:, 0:D_MODEL])
    q_ref[...] = (q * (DC ** -0.5)).astype(BF16)
    k = _dot(h, w_ref[:, D_MODEL:2 * D_MODEL])
    kf_ref[...] = k
    kb_ref[...] = k.astype(BF16)
    v = _dot(h, w_ref[:, 2 * D_MODEL:])
    vf_ref[...] = v
    vb_ref[...] = v.astype(BF16)


def proj_odd(x, g, w, tm):
    M = x.shape[0]
    row_spec = pl.BlockSpec((tm, D_MODEL), lambda i: (i, 0))
    return pl.pallas_call(
        _proj_odd_kernel,
        grid=(M // tm,),
        in_specs=[row_spec, _const_spec((1, D_MODEL)), _const_spec((D_MODEL, 3 * D_MODEL))],
        out_specs=[row_spec] * 5,
        out_shape=[
            jax.ShapeDtypeStruct((M, D_MODEL), BF16),
            jax.ShapeDtypeStruct((M, D_MODEL), F32),
            jax.ShapeDtypeStruct((M, D_MODEL), F32),
            jax.ShapeDtypeStruct((M, D_MODEL), BF16),
            jax.ShapeDtypeStruct((M, D_MODEL), BF16),
        ],
        compiler_params=_cparams(("arbitrary",)),
        name="proj_odd",
    )(x, g, w)


def _attn_prompt_kernel(it_ref, jt_ref, lam_ref, q_ref, k_ref, v_ref, bias_ref, gain_ref, o_ref,
                        m_s, l_s, acc_s, *, tq, out_scale):
    i = it_ref[pl.program_id(1)]
    j = jt_ref[pl.program_id(1)]

    @pl.when(j == 0)
    def _():
        m_s[...] = jnp.full_like(m_s, NEG_INF)
        l_s[...] = jnp.zeros_like(l_s)
        acc_s[...] = jnp.zeros_like(acc_s)

    def scores():
        q = q_ref[...]
        lane = lax.broadcasted_iota(jnp.int32, q.shape, 1)
        zero = jnp.zeros_like(q)
        q2 = jnp.concatenate([jnp.where(lane < DC, q, zero), jnp.where(lane >= DC, q, zero)], axis=0)
        return _dot_nt(q2, k_ref[...])

    def update(s):
        m_old = m_s[...]
        m_new = jnp.maximum(m_old, jnp.max(s, axis=-1, keepdims=True))
        alpha = jnp.exp(m_old - m_new)
        p = jnp.exp(s - m_new)
        l_s[...] = alpha * l_s[...] + jnp.sum(p, axis=-1, keepdims=True)
        acc_s[...] = alpha * acc_s[...] + _dot(p.astype(BF16), v_ref[...])
        m_s[...] = m_new

    @pl.when(j < i - 1)
    def _():
        update(scores())

    for kind in (1, 0):
        @pl.when(j == i - kind)
        def _(kind=kind):
            s = scores().reshape(2, tq, tq) + bias_ref[0, kind][None]
            update(s.reshape(2 * tq, tq))

    @pl.when(j == i)
    def _():
        n = acc_s[...] / l_s[...]
        o = n[0:tq] - lam_ref[0] * n[tq:]
        o_ref[...] = (_rms(o, gain_ref[...]) * out_scale).astype(BF16)


def attn_prompt(lam, q, k, v, bias, gain, tq, out_scale):
    T = q.shape[0]
    nq = T // tq
    pairs = [(i, j) for i in range(nq) for j in range(i + 1)]
    itab = jnp.asarray(np.array([p[0] for p in pairs], np.int32))
    jtab = jnp.asarray(np.array([p[1] for p in pairs], np.int32))
    grid_spec = pltpu.PrefetchScalarGridSpec(
        num_scalar_prefetch=2,
        grid=(H_C, len(pairs)),
        in_specs=[
            pl.BlockSpec(memory_space=pltpu.SMEM),
            pl.BlockSpec((tq, DV_C), lambda h, p, it, jt: (it[p], h)),
            pl.BlockSpec((tq, DV_C), lambda h, p, it, jt: (jt[p], h)),
            pl.BlockSpec((tq, DV_C), lambda h, p, it, jt: (jt[p], h)),
            pl.BlockSpec((1, 2, tq, tq), lambda h, p, it, jt: (h, 0, 0, 0)),
            pl.BlockSpec((1, DV_C), lambda h, p, it, jt: (0, 0)),
        ],
        out_specs=pl.BlockSpec((tq, DV_C), lambda h, p, it, jt: (it[p], h)),
        scratch_shapes=[
            pltpu.VMEM((2 * tq, 1), F32),
            pltpu.VMEM((2 * tq, 1), F32),
            pltpu.VMEM((2 * tq, DV_C), F32),
        ],
    )
    return pl.pallas_call(
        functools.partial(_attn_prompt_kernel, tq=tq, out_scale=out_scale),
        grid_spec=grid_spec,
        out_shape=jax.ShapeDtypeStruct((T, H_C * DV_C), BF16),
        compiler_params=_cparams(("arbitrary", "arbitrary")),
        name="attn_prompt",
    )(itab, jtab, lam, q, k, v, bias, gain)


PAGES_PER_STEP = 4
ROWS_PER_HEAD = 16


def _attn_sample_kernel(pt_ref, lam_ref, q_ref, *refs, out_scale):
    P = PAGES_PER_STEP
    R = ROWS_PER_HEAD
    k_refs = refs[0:P]
    v_refs = refs[P:2 * P]
    kn_ref, vn_ref, bias_last_ref, bias_new_ref, gain_ref, o_ref, m_s, l_s, acc_s, s_s, pv_s = refs[2 * P:]
    j = pl.program_id(1)
    nj = pl.num_programs(1)

    @pl.when(j == 0)
    def _():
        m_s[...] = jnp.full_like(m_s, NEG_INF)
        l_s[...] = jnp.zeros_like(l_s)
        acc_s[...] = jnp.zeros_like(acc_s)

    def update(s, v_of_head, width):
        m_old = m_s[...]
        m_new = jnp.maximum(m_old, jnp.max(s, axis=-1, keepdims=True))
        alpha = jnp.exp(m_old - m_new)
        p = jnp.exp(s - m_new).astype(BF16)
        l_s[...] = alpha * l_s[...] + jnp.sum(p.astype(F32), axis=-1, keepdims=True)
        for h in range(H_C):
            pv_s[h * R:(h + 1) * R, :] = _dot(p[h * R:(h + 1) * R, :], v_of_head(h))
        acc_s[...] = alpha * acc_s[...] + pv_s[...]
        m_s[...] = m_new

    for p in range(P):
        for h in range(H_C):
            kh = k_refs[p][0, pl.ds(h, PAGE_SIZE, stride=H_C), :].astype(BF16)
            s_s[h * R:(h + 1) * R, p * PAGE_SIZE:(p + 1) * PAGE_SIZE] = _dot_nt(q_ref[0, h * R:(h + 1) * R, :], kh)

    def v_cached(h):
        return jnp.concatenate(
            [v_refs[p][0, pl.ds(h, PAGE_SIZE, stride=H_C), :].astype(BF16) for p in range(P)], axis=0)

    @pl.when(j < nj - 1)
    def _():
        update(s_s[...], v_cached, P * PAGE_SIZE)

    @pl.when(j == nj - 1)
    def _():
        update(s_s[...] + bias_last_ref[...], v_cached, P * PAGE_SIZE)
        for h in range(H_C):
            kh = kn_ref[0, :, h * DV_C:(h + 1) * DV_C]
            s_s[h * R:(h + 1) * R, 0:PAGE_SIZE] = _dot_nt(q_ref[0, h * R:(h + 1) * R, :], kh)
        update(s_s[:, 0:PAGE_SIZE] + bias_new_ref[...], lambda h: vn_ref[0, :, h * DV_C:(h + 1) * DV_C], PAGE_SIZE)
        n = acc_s[...] / l_s[...]
        for h in range(H_C):
            o = n[h * R:h * R + 8, :] - lam_ref[0] * n[h * R + 8:(h + 1) * R, :]
            o_ref[0, h * 8:(h + 1) * 8, :] = _rms(o, gain_ref[...]) * out_scale


def attn_sample(page_table, lam, qm, cache_k, cache_v, k_new, v_new, bias_last, bias_new, gain, out_scale):
    B = qm.shape[0]
    P = PAGES_PER_STEP
    n_pages = page_table.shape[1]
    nj = n_pages // P
    rows = H_C * ROWS_PER_HEAD
    page_rows = PAGE_SIZE * H_C

    def page_spec(p):
        return pl.BlockSpec((1, page_rows, DV_C), lambda b, j, pt, p=p: (pt[b, j * P + p], 0, 0))

    grid_spec = pltpu.PrefetchScalarGridSpec(
        num_scalar_prefetch=1,
        grid=(B, nj),
        in_specs=[
            pl.BlockSpec(memory_space=pltpu.SMEM),
            pl.BlockSpec((1, rows, DV_C), lambda b, j, pt: (b, 0, 0)),
            *[page_spec(p) for p in range(P)],
            *[page_spec(p) for p in range(P)],
            pl.BlockSpec((1, PAGE_SIZE, H_C * DV_C), lambda b, j, pt: (b, 0, 0)),
            pl.BlockSpec((1, PAGE_SIZE, H_C * DV_C), lambda b, j, pt: (b, 0, 0)),
            pl.BlockSpec((rows, P * PAGE_SIZE), lambda b, j, pt: (0, 0)),
            pl.BlockSpec((rows, PAGE_SIZE), lambda b, j, pt: (0, 0)),
            pl.BlockSpec((1, DV_C), lambda b, j, pt: (0, 0)),
        ],
        out_specs=pl.BlockSpec((1, H_C * 8, DV_C), lambda b, j, pt: (b, 0, 0)),
        scratch_shapes=[
            pltpu.VMEM((rows, 1), F32),
            pltpu.VMEM((rows, 1), F32),
            pltpu.VMEM((rows, DV_C), F32),
            pltpu.VMEM((rows, P * PAGE_SIZE), F32),
            pltpu.VMEM((rows, DV_C), F32),
        ],
    )
    return pl.pallas_call(
        functools.partial(_attn_sample_kernel, out_scale=out_scale),
        grid_spec=grid_spec,
        out_shape=jax.ShapeDtypeStruct((B, H_C * 8, DV_C), F32),
        compiler_params=_cparams(("arbitrary", "arbitrary")),
        name="attn_sample",
    )(page_table, lam, qm, *([cache_k] * P), *([cache_v] * P), k_new, v_new, bias_last, bias_new, gain)


def _out_proj_kernel(a_ref, w_ref, x_ref, o_ref):
    o_ref[...] = x_ref[...] + _dot(a_ref[...], w_ref[...])


def out_proj(a, w, x, tm):
    M = x.shape[0]
    row_spec = pl.BlockSpec((tm, D_MODEL), lambda i: (i, 0))
    return pl.pallas_call(
        _out_proj_kernel,
        grid=(M // tm,),
        in_specs=[row_spec, _const_spec((D_MODEL, D_MODEL)), row_spec],
        out_specs=row_spec,
        out_shape=jax.ShapeDtypeStruct((M, D_MODEL), F32),
        compiler_params=_cparams(("arbitrary",)),
        name="out_proj",
    )(a, w, x)


def _t5_bucket_table():
    n = np.arange(MAX_DIST + 1)
    max_exact = N_BUCKETS // 2
    nf = np.maximum(n, 1).astype(np.float32)
    large = max_exact + (np.log(nf / max_exact) / math.log(MAX_DIST / max_exact) * (N_BUCKETS - max_exact)).astype(np.int32)
    large = np.minimum(large, N_BUCKETS - 1)
    return np.where(n < max_exact, n, large).astype(np.int32)


def _rel_bias_minus_far(rel_bias, rel):
    tab = rel_bias[_t5_bucket_table()]
    tab = tab - tab[MAX_DIST][None, :]
    vals = tab[np.clip(rel, 0, MAX_DIST)]
    vals = jnp.where(jnp.asarray(rel >= 0)[..., None], vals, NEG_INF)
    return jnp.moveaxis(vals, -1, 0).astype(F32)


TM_PROMPT = 512
TM_FFN = 256
TQ = 512


def kernel(x_prompt, x_sample, state_mlstm_C, state_mlstm_n, state_mlstm_m, state_pool, cache_k, cache_v, state_ffn_conv, page_table, norm_mix, norm_ffn, norm_final, w_in_e, b_gate_e, mlstm_gain, w_pool, pool_scale, w_out_e, w_in_o, lambda_q1, lambda_k1, lambda_q2, lambda_k2, subln_gain, rel_bias, w_out_o, w_up, conv_w, conv_b, w_down):
    Bp, Tp = x_prompt.shape[:2]
    Bs, Ts = x_sample.shape[:2]
    assert Bp == 1
    Ms = Bs * Ts
    xp = x_prompt.reshape(Tp, D_MODEL)
    xs = x_sample.reshape(Ms, D_MODEL)
    row = lambda a: a.reshape(1, -1)

    w_in = w_in_e[0]
    n_gate = 2 * H_A
    w_main = jnp.concatenate([w_in[:, :4 * D_A], w_in[:, 4 * D_A + n_gate:]], axis=1).astype(BF16)
    w_gate = w_in[:, 4 * D_A:4 * D_A + n_gate]
    wg_col = jnp.pad(w_gate, ((0, 0), (0, 128 - n_gate)))
    wg_row = w_gate.T
    b_col = jnp.pad(b_gate_e[0], (0, 128 - n_gate)).reshape(1, 128)
    b_row = b_gate_e[0].reshape(n_gate, 1)
    g_mix0 = row(norm_mix[0])
    wp_b = w_pool[0].astype(BF16)
    wo_e = w_out_e[0].astype(BF16)
    gain_e = row(mlstm_gain[0])
    ps_e = row(pool_scale[0])

    qkv_p, ogu_p, gc_p, gr_p = proj_even(xp, g_mix0, w_main, wg_col, wg_row, b_col, b_row, TM_PROMPT)
    zc = jnp.zeros((1, H_A, DK_A, DK_A), F32)
    zn = jnp.zeros((1, H_A, DK_A), F32)
    zm = jnp.zeros((1, 8, 128), F32)
    hh_p, C_p, n_p, m_p = mlstm(qkv_p[None], ogu_p[None], gc_p[None], gr_p[None], zc, zn, zm, gain_e, MLSTM_CHUNK)
    xp = pool_out(hh_p, ogu_p[None], jnp.zeros((1, POOL_HDR, D_B), F32), wp_b, ps_e, wo_e, xp[None], TM_PROMPT, 0)[0]
    pool_p = ogu_p[Tp - POOL_BUF:, D_A:][None]

    L = MLSTM_CHUNK
    qkv_s, ogu_s, gc_s, gr_s = proj_even(xs, g_mix0, w_main, wg_col, wg_row, b_col, b_row, Ms)
    pad_t = lambda a, n: jnp.pad(a.reshape(Bs, Ts, a.shape[-1]), ((0, 0), (0, n - Ts), (0, 0)))
    gr_s3 = jnp.pad(gr_s.reshape(8, Bs, Ts).transpose(1, 0, 2), ((0, 0), (0, 0), (0, L - Ts)))
    m0_s = jnp.broadcast_to(jnp.pad(state_mlstm_m[0], ((0, 0), (0, 8 - H_A)))[:, :, None], (Bs, 8, 128))
    hh_s, C_s, n_s, m_s = mlstm(pad_t(qkv_s, L), pad_t(ogu_s, L), pad_t(gc_s, L), gr_s3,
                                state_mlstm_C[0], state_mlstm_n[0], m0_s, gain_e, Ts)
    prev16 = jnp.pad(state_pool[0], ((0, 0), (POOL_HDR - POOL_BUF, 0), (0, 0)))
    xs = pool_out(hh_s[:, :16], pad_t(ogu_s, 16), prev16, wp_b, ps_e, wo_e, pad_t(xs, 16), 16, PAST_LEN)[:, :Ts].reshape(Ms, D_MODEL)
    pool_s = jnp.concatenate([state_pool[0], ogu_s[:, D_A:].reshape(Bs, Ts, D_B)], axis=1)[:, -POOL_BUF:]

    def run_ffn(l, xp, xs, final_norm):
        g = row(norm_ffn[l])
        wu = w_up[l].astype(BF16)
        wd = w_down[l].astype(BF16)
        cb = row(conv_b[l])
        gf = row(norm_final)
        zp = jnp.zeros((8, D_FF), F32)
        xp, st_p = ffn(xp, g, wu, conv_w[l], cb, wd, zp, zp, gf, TM_FFN, TM_FFN, True, final_norm)
        st = state_ffn_conv[l]
        z1 = jnp.zeros((Bs, 1, D_FF), F32)
        p1 = jnp.concatenate([st[:, 1:2], z1, z1, z1], axis=1).reshape(Ms, D_FF)
        p2 = jnp.concatenate([st[:, 0:1], st[:, 1:2], z1, z1], axis=1).reshape(Ms, D_FF)
        xs, a_s = ffn(xs, g, wu, conv_w[l], cb, wd, p1, p2, gf, Ms, Ts, False, final_norm)
        conv_p = st_p[8 - (CONV_W - 1):][None]
        conv_s = a_s.reshape(Bs, Ts, D_FF)[:, Ts - (CONV_W - 1):]
        return xp, xs, conv_p, conv_s

    xp, xs, conv_p0, conv_s0 = run_ffn(0, xp, xs, False)

    lam_init = 0.8 - 0.6 * math.exp(-0.3 * 1)
    lam = (jnp.exp(jnp.sum(lambda_q1[0] * lambda_k1[0])) - jnp.exp(jnp.sum(lambda_q2[0] * lambda_k2[0])) + lam_init).astype(F32).reshape(1)
    out_scale = 1.0 - lam_init
    g_mix1 = row(norm_mix[1])
    w_qkv = w_in_o[0].astype(BF16)
    wo_o = w_out_o[0].astype(BF16)
    gain_o = row(subln_gain[0])

    q_p, kf_p, vf_p, kb_p, vb_p = proj_odd(xp, g_mix1, w_qkv, TM_PROMPT)
    r = np.arange(TQ)
    rel_blocks = np.stack([r[:, None] - r[None, :], TQ + r[:, None] - r[None, :]])
    bias_p = _rel_bias_minus_far(rel_bias, rel_blocks)
    o_p = attn_prompt(lam, q_p, kb_p, vb_p, bias_p, gain_o, TQ, out_scale)
    xp = out_proj(o_p, wo_o, xp, TM_PROMPT)

    q_s, kf_s, vf_s, kb_s, vb_s = proj_odd(xs, g_mix1, w_qkv, Ms)
    lane_map = (np.arange(DV_C) >= DC).astype(np.int32)
    q4 = q_s.reshape(Bs, Ts, H_C, DV_C).transpose(0, 2, 1, 3)
    q4 = jnp.pad(q4, ((0, 0), (0, 0), (0, 8 - Ts), (0, 0)))
    qm = jnp.stack([jnp.where(lane_map == m, q4, jnp.zeros_like(q4)) for m in (0, 1)], axis=2)
    qm = qm.reshape(Bs, H_C * ROWS_PER_HEAD, DV_C)
    tok = np.minimum(np.arange(8), Ts - 1)
    tok = np.tile(tok, 2)
    ccol = np.arange(PAGE_SIZE)
    rel_last = PAGE_SIZE + tok[:, None] - ccol[None, :]
    rel_new = np.where(ccol[None, :] < Ts, tok[:, None] - ccol[None, :], -1)
    P = PAGES_PER_STEP
    bias_last = _rel_bias_minus_far(rel_bias, rel_last).reshape(H_C * ROWS_PER_HEAD, PAGE_SIZE)
    bias_last = jnp.pad(bias_last, ((0, 0), ((P - 1) * PAGE_SIZE, 0)))
    bias_new = _rel_bias_minus_far(rel_bias, rel_new).reshape(H_C * ROWS_PER_HEAD, PAGE_SIZE)
    n_phys = cache_k.shape[1]
    ck = cache_k[0].reshape(n_phys, PAGE_SIZE * H_C, DV_C)
    cv = cache_v[0].reshape(n_phys, PAGE_SIZE * H_C, DV_C)
    kn = jnp.pad(kb_s.reshape(Bs, Ts, D_MODEL), ((0, 0), (0, PAGE_SIZE - Ts), (0, 0)))
    vn = jnp.pad(vb_s.reshape(Bs, Ts, D_MODEL), ((0, 0), (0, PAGE_SIZE - Ts), (0, 0)))
    o_s = attn_sample(page_table, lam, qm, ck, cv, kn, vn, bias_last, bias_new, gain_o, out_scale)
    o_s = o_s.reshape(Bs, H_C, 8, DV_C)[:, :, :Ts].transpose(0, 2, 1, 3).reshape(Ms, D_MODEL).astype(BF16)
    xs = out_proj(o_s, wo_o, xs, Ms)

    yp, ys, conv_p1, conv_s1 = run_ffn(1, xp, xs, True)

    y_prompt = yp.reshape(Bp, Tp, D_MODEL)
    y_sample = ys.reshape(Bs, Ts, D_MODEL)
    new_m_p = m_p[:, :H_A, 0]
    new_m_s = m_s[:, :H_A, 0]
    new_k_p = kf_p.reshape(1, Bp, Tp, H_C, DV_C)
    new_v_p = vf_p.reshape(1, Bp, Tp, H_C, DV_C)
    new_k_s = kf_s.reshape(1, Bs, Ts, H_C, DV_C)
    new_v_s = vf_s.reshape(1, Bs, Ts, H_C, DV_C)
    return (y_prompt, y_sample,
            C_p[None], n_p[None], new_m_p[None], pool_p[None], new_k_p, new_v_p,
            jnp.stack([conv_p0, conv_p1]),
            C_s[None], n_s[None], new_m_s[None], pool_s[None], new_k_s, new_v_s,
            jnp.stack([conv_s0, conv_s1]))
```

```python
import functools
import math

import numpy as np
import jax
import jax.numpy as jnp
from jax import lax
from jax.experimental import pallas as pl
from jax.experimental.pallas import tpu as pltpu

F32 = jnp.float32
BF16 = jnp.bfloat16
HIGHEST = lax.Precision.HIGHEST

D_MODEL = 1024
PAST_LEN = 16384
PAGE_SIZE = 128
D_A = 512
H_A = 4
DK_A = 128
MLSTM_CHUNK = 128
D_B = 512
POOL_WINDOWS = (2, 4, 8, 16)
G_B = 128
POOL_BUF = 15
POOL_HDR = 16
H_C = 8
DC = 64
DV_C = 128
N_BUCKETS = 32
MAX_DIST = 128
LOG2E = math.log2(math.e)
SCORE_SCALE = DC ** -0.5 * LOG2E
D_FF = 2816
CONV_W = 3
EPS = 1e-6

VMEM_LIMIT = 56 * 1024 * 1024
NEG_INF = float("-inf")


def _cparams(sem):
    return pltpu.CompilerParams(dimension_semantics=sem, vmem_limit_bytes=VMEM_LIMIT)


def _const_spec(shape):
    nd = len(shape)
    return pl.BlockSpec(shape, lambda *_: (0,) * nd, pipeline_mode=pl.Buffered(1))


def _rms(x, g):
    return x * lax.rsqrt(jnp.mean(x * x, axis=-1, keepdims=True) + EPS) * g


def _dot(a, b):
    return jnp.dot(a, b, preferred_element_type=F32)


def _dot_nt(a, b):
    return lax.dot_general(a, b, (((1,), (1,)), ((), ())), preferred_element_type=F32)


def _dot_tn(a, b):
    return lax.dot_general(a, b, (((0,), (0,)), ((), ())), preferred_element_type=F32)


def _log_sigmoid(x):
    return jnp.minimum(x, 0.0) - jnp.log1p(jnp.exp(-jnp.abs(x)))


def _sigmoid(x):
    return 1.0 / (1.0 + jnp.exp(-x))


def _proj_even_kernel(x_ref, g_ref, w_ref, wgc_ref, wgr_ref, bc_ref, br_ref, qkv_ref, ogu_ref, gc_ref, gr_ref):
    h = _rms(x_ref[...], g_ref[...])
    z = _dot(h.astype(BF16), w_ref[...])
    qkv_ref[:, 0:D_A] = z[:, 0:D_A].astype(BF16)
    qkv_ref[:, D_A:2 * D_A] = (z[:, D_A:2 * D_A] * (DK_A ** -0.5)).astype(BF16)
    qkv_ref[:, 2 * D_A:3 * D_A] = z[:, 2 * D_A:3 * D_A].astype(BF16)
    ogu_ref[...] = z[:, 3 * D_A:]
    gc_ref[...] = jnp.dot(h, wgc_ref[...], precision=HIGHEST, preferred_element_type=F32) + bc_ref[...]
    gr_ref[...] = lax.dot_general(wgr_ref[...], h, (((1,), (1,)), ((), ())), precision=HIGHEST,
                                  preferred_element_type=F32) + br_ref[...]


def proj_even(x, g, w_main, wg_col, wg_row, b_col, b_row, tm):
    M = x.shape[0]
    n_main = w_main.shape[1]
    return pl.pallas_call(
        _proj_even_kernel,
        grid=(M // tm,),
        in_specs=[
            pl.BlockSpec((tm, D_MODEL), lambda i: (i, 0)),
            _const_spec((1, D_MODEL)),
            _const_spec((D_MODEL, n_main)),
            _const_spec((D_MODEL, 128)),
            _const_spec((8, D_MODEL)),
            _const_spec((1, 128)),
            _const_spec((8, 1)),
        ],
        out_specs=[
            pl.BlockSpec((tm, 3 * D_A), lambda i: (i, 0)),
            pl.BlockSpec((tm, D_A + D_B), lambda i: (i, 0)),
            pl.BlockSpec((tm, 128), lambda i: (i, 0)),
            pl.BlockSpec((8, tm), lambda i: (0, i)),
        ],
        out_shape=[
            jax.ShapeDtypeStruct((M, 3 * D_A), BF16),
            jax.ShapeDtypeStruct((M, D_A + D_B), F32),
            jax.ShapeDtypeStruct((M, 128), F32),
            jax.ShapeDtypeStruct((8, M), F32),
        ],
        compiler_params=_cparams(("arbitrary",)),
        name="proj_even",
    )(x, g, w_main, wg_col, wg_row, b_col, b_row)


def _mlstm_kernel(qkv_ref, og_ref, gc_ref, gr_ref, c0_ref, n0_ref, m0_ref, gain_ref,
                  hh_ref, c_out_ref, n_out_ref, m_out_ref, c_s, n_s, m_s, *, valid):
    L = MLSTM_CHUNK
    c = pl.program_id(1)

    @pl.when(c == 0)
    def _():
        c_s[...] = c0_ref[0]
        n_s[...] = n0_ref[0]
        m_s[...] = m0_ref[0]

    row = lax.broadcasted_iota(jnp.int32, (L, L), 0)
    col = lax.broadcasted_iota(jnp.int32, (L, L), 1)
    tri = (col <= row).astype(F32)
    mask = (col <= row) & (col < valid)
    rvalid = lax.broadcasted_iota(jnp.int32, (L, 1), 0) < valid
    cvalid = lax.broadcasted_iota(jnp.int32, (1, L), 1) < valid

    gcol = gc_ref[0]
    grow = gr_ref[0]
    lf_col = jnp.where(rvalid, _log_sigmoid(gcol), 0.0)
    lf_row = jnp.where(cvalid, _log_sigmoid(grow), 0.0)
    b_col_all = jnp.dot(tri, lf_col, precision=HIGHEST, preferred_element_type=F32)
    b_row_all = lax.dot_general(lf_row, tri, (((1,), (1,)), ((), ())), precision=HIGHEST,
                                preferred_element_type=F32)

    for h in range(H_A):
        bc = b_col_all[:, H_A + h:H_A + h + 1]
        br = b_row_all[H_A + h:H_A + h + 1, :]
        igc = gcol[:, h:h + 1]
        igr = grow[h:h + 1, :]
        m0 = m_s[h:h + 1, 0:1]
        logd = jnp.where(mask, bc - br + igr, NEG_INF)
        log_inter = bc + m0
        m_t = jnp.maximum(log_inter, jnp.max(logd, axis=-1, keepdims=True))
        dm = jnp.exp(logd - m_t)
        w_inter = jnp.exp(log_inter - m_t)
        q = qkv_ref[0, :, h * DK_A:(h + 1) * DK_A]
        k = qkv_ref[0, :, D_A + h * DK_A:D_A + (h + 1) * DK_A]
        v = qkv_ref[0, :, 2 * D_A + h * DK_A:2 * D_A + (h + 1) * DK_A]
        s = _dot_nt(q, k) * dm
        c_old = c_s[h]
        n_old = n_s[h:h + 1, :]
        num = _dot(s.astype(BF16), v) + w_inter * _dot_nt(q, c_old.astype(BF16))
        qn = jnp.sum(q.astype(F32) * n_old, axis=-1, keepdims=True)
        ndot = jnp.sum(s, axis=-1, keepdims=True) + w_inter * qn
        denom = jnp.maximum(jnp.abs(ndot), jnp.exp(-m_t))
        hh = num / denom
        y = _rms(hh, gain_ref[:, h * DK_A:(h + 1) * DK_A]) * _sigmoid(og_ref[0, :, h * DK_A:(h + 1) * DK_A])
        hh_ref[0, :, h * DK_A:(h + 1) * DK_A] = y.astype(BF16)
        m_new = m_t[valid - 1:valid, :]
        b_last = bc[valid - 1:valid, :]
        w_s = jnp.where(rvalid, jnp.exp(b_last - bc + igc - m_new), 0.0)
        decay = jnp.exp(b_last + m0 - m_new)
        kf = k.astype(F32)
        vw = (v.astype(F32) * w_s).astype(BF16)
        c_s[h] = decay * c_old + _dot_tn(vw, k)
        n_s[h:h + 1, :] = decay * n_old + jnp.sum(kf * w_s, axis=0, keepdims=True)
        m_s[h:h + 1, :] = jnp.broadcast_to(m_new, (1, 128))

    @pl.when(c == pl.num_programs(1) - 1)
    def _():
        c_out_ref[0] = c_s[...]
        n_out_ref[0] = n_s[...]
        m_out_ref[0] = m_s[...]


def mlstm(qkv, ogu, gc, gr, c0, n0, m0, gain, valid):
    B, T = qkv.shape[:2]
    L = MLSTM_CHUNK
    nc = T // L
    return pl.pallas_call(
        functools.partial(_mlstm_kernel, valid=valid),
        grid=(B, nc),
        in_specs=[
            pl.BlockSpec((1, L, 3 * D_A), lambda b, c: (b, c, 0)),
            pl.BlockSpec((1, L, D_A), lambda b, c: (b, c, 0)),
            pl.BlockSpec((1, L, 128), lambda b, c: (b, c, 0)),
            pl.BlockSpec((1, 8, L), lambda b, c: (b, 0, c)),
            pl.BlockSpec((1, H_A, DK_A, DK_A), lambda b, c: (b, 0, 0, 0)),
            pl.BlockSpec((1, H_A, DK_A), lambda b, c: (b, 0, 0)),
            pl.BlockSpec((1, 8, 128), lambda b, c: (b, 0, 0)),
            pl.BlockSpec((1, D_A), lambda b, c: (0, 0)),
        ],
        out_specs=[
            pl.BlockSpec((1, L, D_A), lambda b, c: (b, c, 0)),
            pl.BlockSpec((1, H_A, DK_A, DK_A), lambda b, c: (b, 0, 0, 0)),
            pl.BlockSpec((1, H_A, DK_A), lambda b, c: (b, 0, 0)),
            pl.BlockSpec((1, 8, 128), lambda b, c: (b, 0, 0)),
        ],
        out_shape=[
            jax.ShapeDtypeStruct((B, T, D_A), BF16),
            jax.ShapeDtypeStruct((B, H_A, DK_A, DK_A), F32),
            jax.ShapeDtypeStruct((B, H_A, DK_A), F32),
            jax.ShapeDtypeStruct((B, 8, 128), F32),
        ],
        scratch_shapes=[
            pltpu.VMEM((H_A, DK_A, DK_A), F32),
            pltpu.VMEM((H_A, DK_A), F32),
            pltpu.VMEM((8, 128), F32),
        ],
        compiler_params=_cparams(("arbitrary", "arbitrary")),
        name="mlstm",
    )(qkv, ogu, gc, gr, c0, n0, m0, gain)


def _pool_out_kernel(hh_ref, u_ref, prev_ref, wp_ref, ps_ref, wo_ref, x_ref, o_ref, e_s, *, tm, nt, pos0):
    t = pl.program_id(1)
    H = POOL_HDR

    @pl.when(t == 0)
    def _():
        e_s[0:H, :] = prev_ref[0]

    if nt > 1:
        @pl.when(t > 0)
        def _():
            e_s[0:H, :] = e_s[tm:tm + H, :]

    e_s[H:H + tm, :] = u_ref[0]
    pos = pos0 + t * tm + lax.broadcasted_iota(jnp.int32, (tm, 1), 0)
    ys = []
    for g, w in enumerate(POOL_WINDOWS):
        sl = slice(g * G_B, (g + 1) * G_B)
        cur = e_s[H:H + tm, sl]
        win = cur
        for j in range(1, w):
            win = win + e_s[H - j:H - j + tm, sl]
        cnt = jnp.minimum(pos + 1, w).astype(F32)
        pooled = win / cnt - cur
        ys.append(_dot(pooled.astype(BF16), wp_ref[g]))
    yb = jnp.concatenate(ys, axis=-1) * ps_ref[...]
    o_ref[0] = x_ref[0] + _dot(hh_ref[0], wo_ref[0:D_A, :]) + _dot(yb.astype(BF16), wo_ref[D_A:, :])


def pool_out(hh, ogu, prev16, w_pool, pool_scale, w_out, x, tm, pos0):
    B, T = x.shape[:2]
    nt = T // tm
    return pl.pallas_call(
        functools.partial(_pool_out_kernel, tm=tm, nt=nt, pos0=pos0),
        grid=(B, nt),
        in_specs=[
            pl.BlockSpec((1, tm, D_A), lambda b, t: (b, t, 0)),
            pl.BlockSpec((1, tm, D_B), lambda b, t: (b, t, 1)),
            pl.BlockSpec((1, POOL_HDR, D_B), lambda b, t: (b, 0, 0)),
            _const_spec((len(POOL_WINDOWS), G_B, G_B)),
            _const_spec((1, D_B)),
            _const_spec((D_A + D_B, D_MODEL)),
            pl.BlockSpec((1, tm, D_MODEL), lambda b, t: (b, t, 0)),
        ],
        out_specs=pl.BlockSpec((1, tm, D_MODEL), lambda b, t: (b, t, 0)),
        out_shape=jax.ShapeDtypeStruct((B, T, D_MODEL), F32),
        scratch_shapes=[pltpu.VMEM((POOL_HDR + tm, D_B), F32)],
        compiler_params=_cparams(("arbitrary", "arbitrary")),
        name="pool_out",
    )(hh, ogu, prev16, w_pool, pool_scale, w_out, x)


def _gelu_tanh(y):
    return 0.5 * y * (1.0 + jnp.tanh(math.sqrt(2.0 / math.pi) * (y + 0.044715 * (y * y * y))))


def _ffn_kernel(x_ref, g_ref, wup_ref, cw_ref, cb_ref, wdn_ref, p1_ref, p2_ref, gf_ref,
                o_ref, st_ref, carry_s, *, tm, seq_len, carried, final_norm):
    i = pl.program_id(0)
    x = x_ref[...]
    h = _rms(x, g_ref[...]).astype(BF16)
    a = _dot(h, wup_ref[:, 0:D_FF])
    gate = _dot(h, wup_ref[:, D_FF:])
    t = lax.broadcasted_iota(jnp.int32, (tm, 1), 0) % seq_len
    s1 = jnp.where(t >= 1, pltpu.roll(a, 1, 0), 0.0)
    s2 = jnp.where(t >= 2, pltpu.roll(a, 2, 0), 0.0)
    if carried:
        @pl.when(i == 0)
        def _():
            carry_s[...] = jnp.zeros_like(carry_s)
        prev0 = carry_s[6:7, :]
        prev1 = carry_s[7:8, :]
        s1 = s1 + jnp.where(t == 0, prev1, 0.0)
        s2 = s2 + jnp.where(t == 0, prev0, 0.0) + jnp.where(t == 1, prev1, 0.0)
        carry_s[...] = a[tm - 8:tm, :]
        st_ref[...] = a[tm - 8:tm, :]
    else:
        s1 = s1 + p1_ref[...]
        s2 = s2 + p2_ref[...]
        st_ref[...] = a
    y = cb_ref[...] + cw_ref[0:1, :] * s2 + cw_ref[1:2, :] * s1 + cw_ref[2:3, :] * a
    act = (_gelu_tanh(y) * gate).astype(BF16)
    out = x + _dot(act, wdn_ref[...])
    if final_norm:
        out = _rms(out, gf_ref[...])
    o_ref[...] = out


def ffn(x, g, w_up, conv_w, conv_b, w_down, p1, p2, g_final, tm, seq_len, carried, final_norm):
    M = x.shape[0]
    st_rows = 8 if carried else tm
    st_total = 8 if carried else M
    row_spec = lambda n: pl.BlockSpec((tm, n), lambda i: (i, 0))
    p_spec = _const_spec((8, D_FF)) if carried else row_spec(D_FF)
    return pl.pallas_call(
        functools.partial(_ffn_kernel, tm=tm, seq_len=seq_len, carried=carried, final_norm=final_norm),
        grid=(M // tm,),
        in_specs=[
            row_spec(D_MODEL),
            _const_spec((1, D_MODEL)),
            _const_spec((D_MODEL, 2 * D_FF)),
            _const_spec((CONV_W, D_FF)),
            _const_spec((1, D_FF)),
            _const_spec((D_FF, D_MODEL)),
            p_spec,
            p_spec,
            _const_spec((1, D_MODEL)),
        ],
        out_specs=[
            row_spec(D_MODEL),
            pl.BlockSpec((st_rows, D_FF), (lambda i: (0, 0)) if carried else (lambda i: (i, 0))),
        ],
        out_shape=[
            jax.ShapeDtypeStruct((M, D_MODEL), F32),
            jax.ShapeDtypeStruct((st_total, D_FF), F32),
        ],
        scratch_shapes=[pltpu.VMEM((8, D_FF), F32)],
        compiler_params=_cparams(("arbitrary",)),
        name="ffn",
    )(x, g, w_up, conv_w, conv_b, w_down, p1, p2, g_final)


def _proj_odd_kernel(x_ref, g_ref, w_ref, q_ref, kf_ref, vf_ref, kb_ref, vb_ref):
    h = _rms(x_ref[...], g_ref[...]).astype(BF16)
    q = _dot(h, w_ref[:, 0:D_MODEL]) * SCORE_SCALE
    lane = lax.broadcasted_iota(jnp.int32, q.shape, 1) % DV_C
    q_ref[0] = jnp.where(lane < DC, q, 0.0).astype(BF16)
    q_ref[1] = jnp.where(lane >= DC, q, 0.0).astype(BF16)
    k = _dot(h, w_ref[:, D_MODEL:2 * D_MODEL])
    kf_ref[...] = k
    kb_ref[...] = k.astype(BF16)
    v = _dot(h, w_ref[:, 2 * D_MODEL:])
    vf_ref[...] = v
    vb_ref[...] = v.astype(BF16)


def proj_odd(x, g, w, tm):
    M = x.shape[0]
    row_spec = pl.BlockSpec((tm, D_MODEL), lambda i: (i, 0))
    return pl.pallas_call(
        _proj_odd_kernel,
        grid=(M // tm,),
        in_specs=[row_spec, _const_spec((1, D_MODEL)), _const_spec((D_MODEL, 3 * D_MODEL))],
        out_specs=[pl.BlockSpec((2, tm, D_MODEL), lambda i: (0, i, 0))] + [row_spec] * 4,
        out_shape=[
            jax.ShapeDtypeStruct((2, M, D_MODEL), BF16),
            jax.ShapeDtypeStruct((M, D_MODEL), F32),
            jax.ShapeDtypeStruct((M, D_MODEL), F32),
            jax.ShapeDtypeStruct((M, D_MODEL), BF16),
            jax.ShapeDtypeStruct((M, D_MODEL), BF16),
        ],
        compiler_params=_cparams(("arbitrary",)),
        name="proj_odd",
    )(x, g, w)


ATTN_STRIP = 256
ATTN_KEY_CHUNK = 256


def _attn_prompt_kernel(it_ref, jt_ref, lam_ref, q_ref, k_ref, vt_ref, bias_ref, gain_ref, o_ref,
                        m_s, l_s, acc_s, s_s, *, tq, out_scale):
    i = it_ref[pl.program_id(1)]
    j = jt_ref[pl.program_id(1)]
    W = ATTN_STRIP
    KC = ATTN_KEY_CHUNK
    nstrip = tq // W

    @pl.when(j == 0)
    def _():
        m_s[...] = jnp.full_like(m_s, NEG_INF)
        l_s[...] = jnp.zeros_like(l_s)
        acc_s[...] = jnp.zeros_like(acc_s)

    def tile(kind):
        strips = [(mp, rb) for mp in range(2) for rb in range(nstrip)]

        def nkeys(rb):
            return (rb + 1) * W if kind == 0 else tq

        def scores(idx):
            mp, rb = strips[idx]
            qs = q_ref[mp, rb * W:(rb + 1) * W, :]
            chunks = [_dot_nt(k_ref[c * KC:(c + 1) * KC, :], qs) for c in range(nkeys(rb) // KC)]

            def add_bias(key_block, b):
                for c in range(key_block * W // KC, (key_block + 1) * W // KC):
                    off = c * KC - key_block * W
                    chunks[c] = chunks[c] + b[off:off + KC, :]

            if kind == 0:
                add_bias(rb, bias_ref[0, 0])
                if rb >= 1:
                    add_bias(rb - 1, bias_ref[0, 1])
            elif kind == 1 and rb == 0:
                add_bias(nstrip - 1, bias_ref[0, 1])
            for c, s in enumerate(chunks):
                s_s[idx % 2, c * KC:(c + 1) * KC, :] = s

        def consume(idx):
            mp, rb = strips[idx]
            nchunks = nkeys(rb) // KC
            cols = slice(mp * tq + rb * W, mp * tq + (rb + 1) * W)
            chunks = [s_s[idx % 2, c * KC:(c + 1) * KC, :] for c in range(nchunks)]
            m_old = m_s[:, cols]
            m_new = m_old
            for s in chunks:
                m_new = jnp.maximum(m_new, jnp.max(s, axis=0, keepdims=True))
            alpha = jnp.exp2(m_old - m_new)
            l_new = alpha * l_s[:, cols]
            ps = []
            for s in chunks:
                p = jnp.exp2(s - m_new)
                l_new = l_new + jnp.sum(p, axis=0, keepdims=True)
                ps.append(p.astype(BF16))
            pcat = jnp.concatenate(ps, axis=0) if nchunks > 1 else ps[0]
            l_s[:, cols] = l_new
            acc_s[:, cols] = alpha * acc_s[:, cols] + _dot(vt_ref[:, 0:nchunks * KC], pcat)
            m_s[:, cols] = m_new

        scores(0)
        for idx in range(len(strips)):
            if idx + 1 < len(strips):
                scores(idx + 1)
            consume(idx)

    @pl.when(j < i - 1)
    def _():
        tile(2)

    @pl.when(j == i - 1)
    def _():
        tile(1)

    @pl.when(j == i)
    def _():
        tile(0)
        n = acc_s[...] / l_s[...]
        o = n[:, 0:tq] - lam_ref[0] * n[:, tq:]
        o = o * lax.rsqrt(jnp.mean(o * o, axis=0, keepdims=True) + EPS) * gain_ref[...] * out_scale
        o_ref[...] = o.T.astype(BF16)


def attn_prompt(lam, q2, k, vt, bias, gain_col, tq, out_scale):
    T = k.shape[0]
    nq = T // tq
    pairs = [(i, j) for i in range(nq) for j in range(i + 1)]
    itab = jnp.asarray(np.array([p[0] for p in pairs], np.int32))
    jtab = jnp.asarray(np.array([p[1] for p in pairs], np.int32))
    grid_spec = pltpu.PrefetchScalarGridSpec(
        num_scalar_prefetch=2,
        grid=(H_C, len(pairs)),
        in_specs=[
            pl.BlockSpec(memory_space=pltpu.SMEM),
            pl.BlockSpec((2, tq, DV_C), lambda h, p, it, jt: (0, it[p], h)),
            pl.BlockSpec((tq, DV_C), lambda h, p, it, jt: (jt[p], h)),
            pl.BlockSpec((DV_C, tq), lambda h, p, it, jt: (h, jt[p])),
            pl.BlockSpec((1, 2, ATTN_STRIP, ATTN_STRIP), lambda h, p, it, jt: (h, 0, 0, 0)),
            pl.BlockSpec((DV_C, 1), lambda h, p, it, jt: (0, 0)),
        ],
        out_specs=pl.BlockSpec((tq, DV_C), lambda h, p, it, jt: (it[p], h)),
        scratch_shapes=[
            pltpu.VMEM((1, 2 * tq), F32),
            pltpu.VMEM((1, 2 * tq), F32),
            pltpu.VMEM((DV_C, 2 * tq), F32),
            pltpu.VMEM((2, tq, ATTN_STRIP), F32),
        ],
    )
    return pl.pallas_call(
        functools.partial(_attn_prompt_kernel, tq=tq, out_scale=out_scale),
        grid_spec=grid_spec,
        out_shape=jax.ShapeDtypeStruct((T, H_C * DV_C), BF16),
        compiler_params=_cparams(("arbitrary", "arbitrary")),
        name="attn_prompt",
    )(itab, jtab, lam, q2, k, vt, bias, gain_col)


PAGES_PER_STEP = 4
ROWS_PER_HEAD = 16


def _attn_sample_kernel(pt_ref, lam_ref, q_ref, *refs, out_scale):
    P = PAGES_PER_STEP
    R = ROWS_PER_HEAD
    k_refs = refs[0:P]
    v_refs = refs[P:2 * P]
    kn_ref, vn_ref, bias_last_ref, bias_new_ref, gain_ref, o_ref, m_s, l_s, acc_s, s_s, pv_s = refs[2 * P:]
    j = pl.program_id(1)
    nj = pl.num_programs(1)

    @pl.when(j == 0)
    def _():
        m_s[...] = jnp.full_like(m_s, NEG_INF)
        l_s[...] = jnp.zeros_like(l_s)
        acc_s[...] = jnp.zeros_like(acc_s)

    def update(s, v_of_head):
        m_old = m_s[...]
        m_new = jnp.maximum(m_old, jnp.max(s, axis=-1, keepdims=True))
        alpha = jnp.exp2(m_old - m_new)
        p = jnp.exp2(s - m_new).astype(BF16)
        l_s[...] = alpha * l_s[...] + jnp.sum(p.astype(F32), axis=-1, keepdims=True)
        for h in range(H_C):
            pv_s[h * R:(h + 1) * R, :] = _dot(p[h * R:(h + 1) * R, :], v_of_head(h))
        acc_s[...] = alpha * acc_s[...] + pv_s[...]
        m_s[...] = m_new

    for p in range(P):
        for h in range(H_C):
            kh = k_refs[p][0, pl.ds(h, PAGE_SIZE, stride=H_C), :].astype(BF16)
            s_s[h * R:(h + 1) * R, p * PAGE_SIZE:(p + 1) * PAGE_SIZE] = _dot_nt(q_ref[0, h * R:(h + 1) * R, :], kh)

    def v_cached(h):
        return jnp.concatenate(
            [v_refs[p][0, pl.ds(h, PAGE_SIZE, stride=H_C), :].astype(BF16) for p in range(P)], axis=0)

    @pl.when(j < nj - 1)
    def _():
        update(s_s[...], v_cached)

    @pl.when(j == nj - 1)
    def _():
        update(s_s[...] + bias_last_ref[...], v_cached)
        for h in range(H_C):
            kh = kn_ref[0, :, h * DV_C:(h + 1) * DV_C]
            s_s[h * R:(h + 1) * R, 0:PAGE_SIZE] = _dot_nt(q_ref[0, h * R:(h + 1) * R, :], kh)
        update(s_s[:, 0:PAGE_SIZE] + bias_new_ref[...], lambda h: vn_ref[0, :, h * DV_C:(h + 1) * DV_C])
        n = acc_s[...] / l_s[...]
        for h in range(H_C):
            o = n[h * R:h * R + 8, :] - lam_ref[0] * n[h * R + 8:(h + 1) * R, :]
            o_ref[0, h * 8:(h + 1) * 8, :] = _rms(o, gain_ref[...]) * out_scale


def attn_sample(page_table, lam, qm, cache_k, cache_v, k_new, v_new, bias_last, bias_new, gain, out_scale):
    B = qm.shape[0]
    P = PAGES_PER_STEP
    n_pages = page_table.shape[1]
    nj = n_pages // P
    rows = H_C * ROWS_PER_HEAD
    page_rows = PAGE_SIZE * H_C

    def page_spec(p):
        return pl.BlockSpec((1, page_rows, DV_C), lambda b, j, pt, p=p: (pt[b, j * P + p], 0, 0))

    grid_spec = pltpu.PrefetchScalarGridSpec(
        num_scalar_prefetch=1,
        grid=(B, nj),
        in_specs=[
            pl.BlockSpec(memory_space=pltpu.SMEM),
            pl.BlockSpec((1, rows, DV_C), lambda b, j, pt: (b, 0, 0)),
            *[page_spec(p) for p in range(P)],
            *[page_spec(p) for p in range(P)],
            pl.BlockSpec((1, PAGE_SIZE, H_C * DV_C), lambda b, j, pt: (b, 0, 0)),
            pl.BlockSpec((1, PAGE_SIZE, H_C * DV_C), lambda b, j, pt: (b, 0, 0)),
            pl.BlockSpec((rows, P * PAGE_SIZE), lambda b, j, pt: (0, 0)),
            pl.BlockSpec((rows, PAGE_SIZE), lambda b, j, pt: (0, 0)),
            pl.BlockSpec((1, DV_C), lambda b, j, pt: (0, 0)),
        ],
        out_specs=pl.BlockSpec((1, H_C * 8, DV_C), lambda b, j, pt: (b, 0, 0)),
        scratch_shapes=[
            pltpu.VMEM((rows, 1), F32),
            pltpu.VMEM((rows, 1), F32),
            pltpu.VMEM((rows, DV_C), F32),
            pltpu.VMEM((rows, P * PAGE_SIZE), F32),
            pltpu.VMEM((rows, DV_C), F32),
        ],
    )
    return pl.pallas_call(
        functools.partial(_attn_sample_kernel, out_scale=out_scale),
        grid_spec=grid_spec,
        out_shape=jax.ShapeDtypeStruct((B, H_C * 8, DV_C), F32),
        compiler_params=_cparams(("arbitrary", "arbitrary")),
        name="attn_sample",
    )(page_table, lam, qm, *([cache_k] * P), *([cache_v] * P), k_new, v_new, bias_last, bias_new, gain)


def _out_proj_kernel(a_ref, w_ref, x_ref, o_ref):
    o_ref[...] = x_ref[...] + _dot(a_ref[...], w_ref[...])


def out_proj(a, w, x, tm):
    M = x.shape[0]
    row_spec = pl.BlockSpec((tm, D_MODEL), lambda i: (i, 0))
    return pl.pallas_call(
        _out_proj_kernel,
        grid=(M // tm,),
        in_specs=[row_spec, _const_spec((D_MODEL, D_MODEL)), row_spec],
        out_specs=row_spec,
        out_shape=jax.ShapeDtypeStruct((M, D_MODEL), F32),
        compiler_params=_cparams(("arbitrary",)),
        name="out_proj",
    )(a, w, x)


def _t5_bucket_table():
    n = np.arange(MAX_DIST + 1)
    max_exact = N_BUCKETS // 2
    nf = np.maximum(n, 1).astype(np.float32)
    large = max_exact + (np.log(nf / max_exact) / math.log(MAX_DIST / max_exact) * (N_BUCKETS - max_exact)).astype(np.int32)
    large = np.minimum(large, N_BUCKETS - 1)
    return np.where(n < max_exact, n, large).astype(np.int32)


def _rel_bias_minus_far(rel_bias, rel):
    tab = _near_bias_table(rel_bias)
    vals = tab[np.clip(rel, 0, MAX_DIST)]
    vals = jnp.where(jnp.asarray(rel >= 0)[..., None], vals, NEG_INF)
    return jnp.moveaxis(vals, -1, 0).astype(F32)


def _near_bias_table(rel_bias):
    tab = rel_bias[_t5_bucket_table()]
    return ((tab - tab[MAX_DIST][None, :]) * LOG2E).astype(F32)


def _bias_blocks_t(rel_bias):
    W = ATTN_STRIP
    H = rel_bias.shape[1]
    f = jnp.concatenate([_near_bias_table(rel_bias), jnp.zeros((W - MAX_DIST - 1, H), F32)], axis=0).T
    g0 = jnp.concatenate([f, jnp.full((H, W), NEG_INF, F32)], axis=1)
    g1 = jnp.concatenate([jnp.zeros((H, W), F32), f], axis=1)
    g = jnp.stack([g0, g1], axis=1)
    rep = jnp.tile(g, (1, 1, W))[:, :, :W * (2 * W - 1)].reshape(H, 2, W, 2 * W - 1)
    return rep[:, :, :, :W]


TM_PROMPT = 512
TM_FFN = 256
TQ = 1024


def kernel(x_prompt, x_sample, state_mlstm_C, state_mlstm_n, state_mlstm_m, state_pool, cache_k, cache_v, state_ffn_conv, page_table, norm_mix, norm_ffn, norm_final, w_in_e, b_gate_e, mlstm_gain, w_pool, pool_scale, w_out_e, w_in_o, lambda_q1, lambda_k1, lambda_q2, lambda_k2, subln_gain, rel_bias, w_out_o, w_up, conv_w, conv_b, w_down):
    Bp, Tp = x_prompt.shape[:2]
    Bs, Ts = x_sample.shape[:2]
    assert Bp == 1
    Ms = Bs * Ts
    xp = x_prompt.reshape(Tp, D_MODEL)
    xs = x_sample.reshape(Ms, D_MODEL)
    row = lambda a: a.reshape(1, -1)

    w_in = w_in_e[0]
    n_gate = 2 * H_A
    w_main = jnp.concatenate([w_in[:, :4 * D_A], w_in[:, 4 * D_A + n_gate:]], axis=1).astype(BF16)
    w_gate = w_in[:, 4 * D_A:4 * D_A + n_gate]
    wg_col = jnp.pad(w_gate, ((0, 0), (0, 128 - n_gate)))
    wg_row = w_gate.T
    b_col = jnp.pad(b_gate_e[0], (0, 128 - n_gate)).reshape(1, 128)
    b_row = b_gate_e[0].reshape(n_gate, 1)
    g_mix0 = row(norm_mix[0])
    wp_b = w_pool[0].astype(BF16)
    wo_e = w_out_e[0].astype(BF16)
    gain_e = row(mlstm_gain[0])
    ps_e = row(pool_scale[0])

    qkv_p, ogu_p, gc_p, gr_p = proj_even(xp, g_mix0, w_main, wg_col, wg_row, b_col, b_row, TM_PROMPT)
    zc = jnp.zeros((1, H_A, DK_A, DK_A), F32)
    zn = jnp.zeros((1, H_A, DK_A), F32)
    zm = jnp.zeros((1, 8, 128), F32)
    hh_p, C_p, n_p, m_p = mlstm(qkv_p[None], ogu_p[None], gc_p[None], gr_p[None], zc, zn, zm, gain_e, MLSTM_CHUNK)
    xp = pool_out(hh_p, ogu_p[None], jnp.zeros((1, POOL_HDR, D_B), F32), wp_b, ps_e, wo_e, xp[None], TM_PROMPT, 0)[0]
    pool_p = ogu_p[Tp - POOL_BUF:, D_A:][None]

    L = MLSTM_CHUNK
    qkv_s, ogu_s, gc_s, gr_s = proj_even(xs, g_mix0, w_main, wg_col, wg_row, b_col, b_row, Ms)
    pad_t = lambda a, n: jnp.pad(a.reshape(Bs, Ts, a.shape[-1]), ((0, 0), (0, n - Ts), (0, 0)))
    gr_s3 = jnp.pad(gr_s.reshape(8, Bs, Ts).transpose(1, 0, 2), ((0, 0), (0, 0), (0, L - Ts)))
    m0_s = jnp.broadcast_to(jnp.pad(state_mlstm_m[0], ((0, 0), (0, 8 - H_A)))[:, :, None], (Bs, 8, 128))
    hh_s, C_s, n_s, m_s = mlstm(pad_t(qkv_s, L), pad_t(ogu_s, L), pad_t(gc_s, L), gr_s3,
                                state_mlstm_C[0], state_mlstm_n[0], m0_s, gain_e, Ts)
    prev16 = jnp.pad(state_pool[0], ((0, 0), (POOL_HDR - POOL_BUF, 0), (0, 0)))
    xs = pool_out(hh_s[:, :16], pad_t(ogu_s, 16), prev16, wp_b, ps_e, wo_e, pad_t(xs, 16), 16, PAST_LEN)[:, :Ts].reshape(Ms, D_MODEL)
    pool_s = jnp.concatenate([state_pool[0], ogu_s[:, D_A:].reshape(Bs, Ts, D_B)], axis=1)[:, -POOL_BUF:]

    def run_ffn(l, xp, xs, final_norm):
        g = row(norm_ffn[l])
        wu = w_up[l].astype(BF16)
        wd = w_down[l].astype(BF16)
        cb = row(conv_b[l])
        gf = row(norm_final)
        zp = jnp.zeros((8, D_FF), F32)
        xp, st_p = ffn(xp, g, wu, conv_w[l], cb, wd, zp, zp, gf, TM_FFN, TM_FFN, True, final_norm)
        st = state_ffn_conv[l]
        z1 = jnp.zeros((Bs, 1, D_FF), F32)
        p1 = jnp.concatenate([st[:, 1:2], z1, z1, z1], axis=1).reshape(Ms, D_FF)
        p2 = jnp.concatenate([st[:, 0:1], st[:, 1:2], z1, z1], axis=1).reshape(Ms, D_FF)
        xs, a_s = ffn(xs, g, wu, conv_w[l], cb, wd, p1, p2, gf, Ms, Ts, False, final_norm)
        conv_p = st_p[8 - (CONV_W - 1):][None]
        conv_s = a_s.reshape(Bs, Ts, D_FF)[:, Ts - (CONV_W - 1):]
        return xp, xs, conv_p, conv_s

    xp, xs, conv_p0, conv_s0 = run_ffn(0, xp, xs, False)

    lam_init = 0.8 - 0.6 * math.exp(-0.3 * 1)
    lam = (jnp.exp(jnp.sum(lambda_q1[0] * lambda_k1[0])) - jnp.exp(jnp.sum(lambda_q2[0] * lambda_k2[0])) + lam_init).astype(F32).reshape(1)
    out_scale = 1.0 - lam_init
    g_mix1 = row(norm_mix[1])
    w_qkv = w_in_o[0].astype(BF16)
    wo_o = w_out_o[0].astype(BF16)
    gain_o = row(subln_gain[0])

    q2_p, kf_p, vf_p, kb_p, vb_p = proj_odd(xp, g_mix1, w_qkv, TM_PROMPT)
    o_p = attn_prompt(lam, q2_p, kb_p, vb_p.T, _bias_blocks_t(rel_bias), gain_o.reshape(DV_C, 1), TQ, out_scale)
    xp = out_proj(o_p, wo_o, xp, TM_PROMPT)

    q2_s, kf_s, vf_s, kb_s, vb_s = proj_odd(xs, g_mix1, w_qkv, Ms)
    qm = q2_s.reshape(2, Bs, Ts, H_C, DV_C).transpose(1, 3, 0, 2, 4)
    qm = jnp.pad(qm, ((0, 0), (0, 0), (0, 0), (0, 8 - Ts), (0, 0)))
    qm = qm.reshape(Bs, H_C * ROWS_PER_HEAD, DV_C)
    tok = np.minimum(np.arange(8), Ts - 1)
    tok = np.tile(tok, 2)
    ccol = np.arange(PAGE_SIZE)
    rel_last = PAGE_SIZE + tok[:, None] - ccol[None, :]
    rel_new = np.where(ccol[None, :] < Ts, tok[:, None] - ccol[None, :], -1)
    P = PAGES_PER_STEP
    bias_last = _rel_bias_minus_far(rel_bias, rel_last).reshape(H_C * ROWS_PER_HEAD, PAGE_SIZE)
    bias_last = jnp.pad(bias_last, ((0, 0), ((P - 1) * PAGE_SIZE, 0)))
    bias_new = _rel_bias_minus_far(rel_bias, rel_new).reshape(H_C * ROWS_PER_HEAD, PAGE_SIZE)
    n_phys = cache_k.shape[1]
    ck = cache_k[0].reshape(n_phys, PAGE_SIZE * H_C, DV_C)
    cv = cache_v[0].reshape(n_phys, PAGE_SIZE * H_C, DV_C)
    kn = jnp.pad(kb_s.reshape(Bs, Ts, D_MODEL), ((0, 0), (0, PAGE_SIZE - Ts), (0, 0)))
    vn = jnp.pad(vb_s.reshape(Bs, Ts, D_MODEL), ((0, 0), (0, PAGE_SIZE - Ts), (0, 0)))
    o_s = attn_sample(page_table, lam, qm, ck, cv, kn, vn, bias_last, bias_new, gain_o, out_scale)
    o_s = o_s.reshape(Bs, H_C, 8, DV_C)[:, :, :Ts].transpose(0, 2, 1, 3).reshape(Ms, D_MODEL).astype(BF16)
    xs = out_proj(o_s, wo_o, xs, Ms)

    yp, ys, conv_p1, conv_s1 = run_ffn(1, xp, xs, True)

    y_prompt = yp.reshape(Bp, Tp, D_MODEL)
    y_sample = ys.reshape(Bs, Ts, D_MODEL)
    new_m_p = m_p[:, :H_A, 0]
    new_m_s = m_s[:, :H_A, 0]
    new_k_p = kf_p.reshape(1, Bp, Tp, H_C, DV_C)
    new_v_p = vf_p.reshape(1, Bp, Tp, H_C, DV_C)
    new_k_s = kf_s.reshape(1, Bs, Ts, H_C, DV_C)
    new_v_s = vf_s.reshape(1, Bs, Ts, H_C, DV_C)
    return (y_prompt, y_sample,
            C_p[None], n_p[None], new_m_p[None], pool_p[None], new_k_p, new_v_p,
            jnp.stack([conv_p0, conv_p1]),
            C_s[None], n_s[None], new_m_s[None], pool_s[None], new_k_s, new_v_s,
            jnp.stack([conv_s0, conv_s1]))
```

```python
import functools
import math

import numpy as np
import jax
import jax.numpy as jnp
from jax import lax
from jax.experimental import pallas as pl
from jax.experimental.pallas import tpu as pltpu

F32 = jnp.float32
BF16 = jnp.bfloat16
HIGHEST = lax.Precision.HIGHEST

D_MODEL = 1024
PAST_LEN = 16384
PAGE_SIZE = 128
D_A = 512
H_A = 4
DK_A = 128
MLSTM_CHUNK = 128
D_B = 512
POOL_WINDOWS = (2, 4, 8, 16)
G_B = 128
POOL_BUF = 15
POOL_HDR = 16
H_C = 8
DC = 64
DV_C = 128
N_BUCKETS = 32
MAX_DIST = 128
LOG2E = math.log2(math.e)
SCORE_SCALE = DC ** -0.5 * LOG2E
D_FF = 2816
CONV_W = 3
EPS = 1e-6

VMEM_LIMIT = 56 * 1024 * 1024
NEG_INF = float("-inf")


def _cparams(sem):
    return pltpu.CompilerParams(dimension_semantics=sem, vmem_limit_bytes=VMEM_LIMIT)


def _const_spec(shape):
    nd = len(shape)
    return pl.BlockSpec(shape, lambda *_: (0,) * nd, pipeline_mode=pl.Buffered(1))


def _rms(x, g):
    return x * lax.rsqrt(jnp.mean(x * x, axis=-1, keepdims=True) + EPS) * g


def _dot(a, b):
    return jnp.dot(a, b, preferred_element_type=F32)


def _dot_nt(a, b):
    return lax.dot_general(a, b, (((1,), (1,)), ((), ())), preferred_element_type=F32)


def _dot_tn(a, b):
    return lax.dot_general(a, b, (((0,), (0,)), ((), ())), preferred_element_type=F32)


def _log_sigmoid(x):
    return jnp.minimum(x, 0.0) - jnp.log1p(jnp.exp(-jnp.abs(x)))


def _sigmoid(x):
    return 1.0 / (1.0 + jnp.exp(-x))


def _proj_even_kernel(x_ref, g_ref, w_ref, bc_ref, qkv_ref, ogu_ref, gc_ref, gr_ref):
    h = _rms(x_ref[...], g_ref[...])
    z = _dot(h.astype(BF16), w_ref[...])
    qkv_ref[:, 0:D_A] = z[:, 0:D_A].astype(BF16)
    qkv_ref[:, D_A:2 * D_A] = (z[:, D_A:2 * D_A] * (DK_A ** -0.5)).astype(BF16)
    qkv_ref[:, 2 * D_A:3 * D_A] = z[:, 2 * D_A:3 * D_A].astype(BF16)
    ogu_ref[...] = z[:, 3 * D_A:3 * D_A + D_A + D_B]
    gc = z[:, 3 * D_A + D_A + D_B:] + bc_ref[...]
    gc_ref[...] = gc
    gr_ref[...] = gc.T[0:8, :]


def proj_even(x, g, w_main, b_col, tm):
    M = x.shape[0]
    n_main = w_main.shape[1]
    return pl.pallas_call(
        _proj_even_kernel,
        grid=(M // tm,),
        in_specs=[
            pl.BlockSpec((tm, D_MODEL), lambda i: (i, 0)),
            _const_spec((1, D_MODEL)),
            _const_spec((D_MODEL, n_main)),
            _const_spec((1, 128)),
        ],
        out_specs=[
            pl.BlockSpec((tm, 3 * D_A), lambda i: (i, 0)),
            pl.BlockSpec((tm, D_A + D_B), lambda i: (i, 0)),
            pl.BlockSpec((tm, 128), lambda i: (i, 0)),
            pl.BlockSpec((8, tm), lambda i: (0, i)),
        ],
        out_shape=[
            jax.ShapeDtypeStruct((M, 3 * D_A), BF16),
            jax.ShapeDtypeStruct((M, D_A + D_B), F32),
            jax.ShapeDtypeStruct((M, 128), F32),
            jax.ShapeDtypeStruct((8, M), F32),
        ],
        compiler_params=_cparams(("arbitrary",)),
        name="proj_even",
    )(x, g, w_main, b_col)


def _mlstm_kernel(qkv_ref, og_ref, gc_ref, gr_ref, c0_ref, n0_ref, m0_ref, gain_ref,
                  hh_ref, c_out_ref, n_out_ref, m_out_ref, c_s, n_s, m_s, *, valid):
    L = MLSTM_CHUNK
    c = pl.program_id(1)

    @pl.when(c == 0)
    def _():
        c_s[...] = c0_ref[0]
        n_s[...] = n0_ref[0]
        m_s[...] = m0_ref[0]

    row = lax.broadcasted_iota(jnp.int32, (L, L), 0)
    col = lax.broadcasted_iota(jnp.int32, (L, L), 1)
    tri = (col <= row).astype(F32)
    mask = (col <= row) & (col < valid)
    rvalid = lax.broadcasted_iota(jnp.int32, (L, 1), 0) < valid
    cvalid = lax.broadcasted_iota(jnp.int32, (1, L), 1) < valid

    gcol = gc_ref[0]
    grow = gr_ref[0]
    lf_col = jnp.where(rvalid, _log_sigmoid(gcol), 0.0)
    lf_row = jnp.where(cvalid, _log_sigmoid(grow), 0.0)
    b_col_all = jnp.dot(tri, lf_col, precision=HIGHEST, preferred_element_type=F32)
    b_row_all = lax.dot_general(lf_row, tri, (((1,), (1,)), ((), ())), precision=HIGHEST,
                                preferred_element_type=F32)

    for h in range(H_A):
        bc = b_col_all[:, H_A + h:H_A + h + 1]
        br = b_row_all[H_A + h:H_A + h + 1, :]
        igc = gcol[:, h:h + 1]
        igr = grow[h:h + 1, :]
        m0 = m_s[h:h + 1, 0:1]
        logd = jnp.where(mask, bc - br + igr, NEG_INF)
        log_inter = bc + m0
        m_t = jnp.maximum(log_inter, jnp.max(logd, axis=-1, keepdims=True))
        dm = jnp.exp(logd - m_t)
        w_inter = jnp.exp(log_inter - m_t)
        q = qkv_ref[0, :, h * DK_A:(h + 1) * DK_A]
        k = qkv_ref[0, :, D_A + h * DK_A:D_A + (h + 1) * DK_A]
        v = qkv_ref[0, :, 2 * D_A + h * DK_A:2 * D_A + (h + 1) * DK_A]
        s = _dot_nt(q, k) * dm
        c_old = c_s[h]
        n_old = n_s[h:h + 1, :]
        num = _dot(s.astype(BF16), v) + w_inter * _dot_nt(q, c_old.astype(BF16))
        qn = jnp.sum(q.astype(F32) * n_old, axis=-1, keepdims=True)
        ndot = jnp.sum(s, axis=-1, keepdims=True) + w_inter * qn
        denom = jnp.maximum(jnp.abs(ndot), jnp.exp(-m_t))
        hh = num / denom
        y = _rms(hh, gain_ref[:, h * DK_A:(h + 1) * DK_A]) * _sigmoid(og_ref[0, :, h * DK_A:(h + 1) * DK_A])
        hh_ref[0, :, h * DK_A:(h + 1) * DK_A] = y.astype(BF16)
        m_new = m_t[valid - 1:valid, :]
        b_last = bc[valid - 1:valid, :]
        w_s = jnp.where(rvalid, jnp.exp(b_last - bc + igc - m_new), 0.0)
        decay = jnp.exp(b_last + m0 - m_new)
        kf = k.astype(F32)
        vw = (v.astype(F32) * w_s).astype(BF16)
        c_s[h] = decay * c_old + _dot_tn(vw, k)
        n_s[h:h + 1, :] = decay * n_old + jnp.sum(kf * w_s, axis=0, keepdims=True)
        m_s[h:h + 1, :] = jnp.broadcast_to(m_new, (1, 128))

    @pl.when(c == pl.num_programs(1) - 1)
    def _():
        c_out_ref[0] = c_s[...]
        n_out_ref[0] = n_s[...]
        m_out_ref[0] = m_s[...]


def mlstm(qkv, ogu, gc, gr, c0, n0, m0, gain, valid):
    B, T = qkv.shape[:2]
    L = MLSTM_CHUNK
    nc = T // L
    return pl.pallas_call(
        functools.partial(_mlstm_kernel, valid=valid),
        grid=(B, nc),
        in_specs=[
            pl.BlockSpec((1, L, 3 * D_A), lambda b, c: (b, c, 0)),
            pl.BlockSpec((1, L, D_A), lambda b, c: (b, c, 0)),
            pl.BlockSpec((1, L, 128), lambda b, c: (b, c, 0)),
            pl.BlockSpec((1, 8, L), lambda b, c: (b, 0, c)),
            pl.BlockSpec((1, H_A, DK_A, DK_A), lambda b, c: (b, 0, 0, 0)),
            pl.BlockSpec((1, H_A, DK_A), lambda b, c: (b, 0, 0)),
            pl.BlockSpec((1, 8, 128), lambda b, c: (b, 0, 0)),
            pl.BlockSpec((1, D_A), lambda b, c: (0, 0)),
        ],
        out_specs=[
            pl.BlockSpec((1, L, D_A), lambda b, c: (b, c, 0)),
            pl.BlockSpec((1, H_A, DK_A, DK_A), lambda b, c: (b, 0, 0, 0)),
            pl.BlockSpec((1, H_A, DK_A), lambda b, c: (b, 0, 0)),
            pl.BlockSpec((1, 8, 128), lambda b, c: (b, 0, 0)),
        ],
        out_shape=[
            jax.ShapeDtypeStruct((B, T, D_A), BF16),
            jax.ShapeDtypeStruct((B, H_A, DK_A, DK_A), F32),
            jax.ShapeDtypeStruct((B, H_A, DK_A), F32),
            jax.ShapeDtypeStruct((B, 8, 128), F32),
        ],
        scratch_shapes=[
            pltpu.VMEM((H_A, DK_A, DK_A), F32),
            pltpu.VMEM((H_A, DK_A), F32),
            pltpu.VMEM((8, 128), F32),
        ],
        compiler_params=_cparams(("arbitrary", "arbitrary")),
        name="mlstm",
    )(qkv, ogu, gc, gr, c0, n0, m0, gain)


def _pool_out_kernel(hh_ref, u_ref, prev_ref, wp_ref, ps_ref, wo_ref, x_ref, o_ref, e_s, *, tm, nt, pos0):
    t = pl.program_id(1)
    H = POOL_HDR

    @pl.when(t == 0)
    def _():
        e_s[0:H, :] = prev_ref[0]

    if nt > 1:
        @pl.when(t > 0)
        def _():
            e_s[0:H, :] = e_s[tm:tm + H, :]

    e_s[H:H + tm, :] = u_ref[0]
    pos = pos0 + t * tm + lax.broadcasted_iota(jnp.int32, (tm, 1), 0)
    ys = []
    for g, w in enumerate(POOL_WINDOWS):
        sl = slice(g * G_B, (g + 1) * G_B)
        cur = e_s[H:H + tm, sl]
        win = cur
        for j in range(1, w):
            win = win + e_s[H - j:H - j + tm, sl]
        cnt = jnp.minimum(pos + 1, w).astype(F32)
        pooled = win / cnt - cur
        ys.append(_dot(pooled.astype(BF16), wp_ref[g]))
    yb = jnp.concatenate(ys, axis=-1) * ps_ref[...]
    o_ref[0] = x_ref[0] + _dot(hh_ref[0], wo_ref[0:D_A, :]) + _dot(yb.astype(BF16), wo_ref[D_A:, :])


def pool_out(hh, ogu, prev16, w_pool, pool_scale, w_out, x, tm, pos0):
    B, T = x.shape[:2]
    nt = T // tm
    return pl.pallas_call(
        functools.partial(_pool_out_kernel, tm=tm, nt=nt, pos0=pos0),
        grid=(B, nt),
        in_specs=[
            pl.BlockSpec((1, tm, D_A), lambda b, t: (b, t, 0)),
            pl.BlockSpec((1, tm, D_B), lambda b, t: (b, t, 1)),
            pl.BlockSpec((1, POOL_HDR, D_B), lambda b, t: (b, 0, 0)),
            _const_spec((len(POOL_WINDOWS), G_B, G_B)),
            _const_spec((1, D_B)),
            _const_spec((D_A + D_B, D_MODEL)),
            pl.BlockSpec((1, tm, D_MODEL), lambda b, t: (b, t, 0)),
        ],
        out_specs=pl.BlockSpec((1, tm, D_MODEL), lambda b, t: (b, t, 0)),
        out_shape=jax.ShapeDtypeStruct((B, T, D_MODEL), F32),
        scratch_shapes=[pltpu.VMEM((POOL_HDR + tm, D_B), F32)],
        compiler_params=_cparams(("arbitrary", "arbitrary")),
        name="pool_out",
    )(hh, ogu, prev16, w_pool, pool_scale, w_out, x)


def _gelu_tanh(y):
    return 0.5 * y * (1.0 + jnp.tanh(math.sqrt(2.0 / math.pi) * (y + 0.044715 * (y * y * y))))


def _ffn_kernel(x_ref, g_ref, wup_ref, cw_ref, cb_ref, wdn_ref, p1_ref, p2_ref, gf_ref,
                o_ref, st_ref, carry_s, *, tm, seq_len, carried, final_norm):
    i = pl.program_id(0)
    x = x_ref[...]
    h = _rms(x, g_ref[...]).astype(BF16)
    a = _dot(h, wup_ref[:, 0:D_FF])
    gate = _dot(h, wup_ref[:, D_FF:])
    t = lax.broadcasted_iota(jnp.int32, (tm, 1), 0) % seq_len
    s1 = jnp.where(t >= 1, pltpu.roll(a, 1, 0), 0.0)
    s2 = jnp.where(t >= 2, pltpu.roll(a, 2, 0), 0.0)
    if carried:
        @pl.when(i == 0)
        def _():
            carry_s[...] = jnp.zeros_like(carry_s)
        prev0 = carry_s[6:7, :]
        prev1 = carry_s[7:8, :]
        s1 = s1 + jnp.where(t == 0, prev1, 0.0)
        s2 = s2 + jnp.where(t == 0, prev0, 0.0) + jnp.where(t == 1, prev1, 0.0)
        carry_s[...] = a[tm - 8:tm, :]
        st_ref[...] = a[tm - 8:tm, :]
    else:
        s1 = s1 + p1_ref[...]
        s2 = s2 + p2_ref[...]
        st_ref[...] = a
    y = cb_ref[...] + cw_ref[0:1, :] * s2 + cw_ref[1:2, :] * s1 + cw_ref[2:3, :] * a
    act = (_gelu_tanh(y) * gate).astype(BF16)
    out = x + _dot(act, wdn_ref[...])
    if final_norm:
        out = _rms(out, gf_ref[...])
    o_ref[...] = out


def ffn(x, g, w_up, conv_w, conv_b, w_down, p1, p2, g_final, tm, seq_len, carried, final_norm):
    M = x.shape[0]
    st_rows = 8 if carried else tm
    st_total = 8 if carried else M
    row_spec = lambda n: pl.BlockSpec((tm, n), lambda i: (i, 0))
    p_spec = _const_spec((8, D_FF)) if carried else row_spec(D_FF)
    return pl.pallas_call(
        functools.partial(_ffn_kernel, tm=tm, seq_len=seq_len, carried=carried, final_norm=final_norm),
        grid=(M // tm,),
        in_specs=[
            row_spec(D_MODEL),
            _const_spec((1, D_MODEL)),
            _const_spec((D_MODEL, 2 * D_FF)),
            _const_spec((CONV_W, D_FF)),
            _const_spec((1, D_FF)),
            _const_spec((D_FF, D_MODEL)),
            p_spec,
            p_spec,
            _const_spec((1, D_MODEL)),
        ],
        out_specs=[
            row_spec(D_MODEL),
            pl.BlockSpec((st_rows, D_FF), (lambda i: (0, 0)) if carried else (lambda i: (i, 0))),
        ],
        out_shape=[
            jax.ShapeDtypeStruct((M, D_MODEL), F32),
            jax.ShapeDtypeStruct((st_total, D_FF), F32),
        ],
        scratch_shapes=[pltpu.VMEM((8, D_FF), F32)],
        compiler_params=_cparams(("arbitrary",)),
        name="ffn",
    )(x, g, w_up, conv_w, conv_b, w_down, p1, p2, g_final)


def _proj_odd_kernel(x_ref, g_ref, w_ref, q_ref, kf_ref, vf_ref, kb_ref, vb_ref):
    h = _rms(x_ref[...], g_ref[...]).astype(BF16)
    q = _dot(h, w_ref[:, 0:D_MODEL]) * SCORE_SCALE
    lane = lax.broadcasted_iota(jnp.int32, q.shape, 1) % DV_C
    q_ref[0] = jnp.where(lane < DC, q, 0.0).astype(BF16)
    q_ref[1] = jnp.where(lane >= DC, q, 0.0).astype(BF16)
    k = _dot(h, w_ref[:, D_MODEL:2 * D_MODEL])
    kf_ref[...] = k
    kb_ref[...] = k.astype(BF16)
    v = _dot(h, w_ref[:, 2 * D_MODEL:])
    vf_ref[...] = v
    vb_ref[...] = v.astype(BF16)


def proj_odd(x, g, w, tm):
    M = x.shape[0]
    row_spec = pl.BlockSpec((tm, D_MODEL), lambda i: (i, 0))
    return pl.pallas_call(
        _proj_odd_kernel,
        grid=(M // tm,),
        in_specs=[row_spec, _const_spec((1, D_MODEL)), _const_spec((D_MODEL, 3 * D_MODEL))],
        out_specs=[pl.BlockSpec((2, tm, D_MODEL), lambda i: (0, i, 0))] + [row_spec] * 4,
        out_shape=[
            jax.ShapeDtypeStruct((2, M, D_MODEL), BF16),
            jax.ShapeDtypeStruct((M, D_MODEL), F32),
            jax.ShapeDtypeStruct((M, D_MODEL), F32),
            jax.ShapeDtypeStruct((M, D_MODEL), BF16),
            jax.ShapeDtypeStruct((M, D_MODEL), BF16),
        ],
        compiler_params=_cparams(("arbitrary",)),
        name="proj_odd",
    )(x, g, w)


ATTN_STRIP = 256
ATTN_KEY_CHUNK = 256
ATTN_ONES_ROWS = 16


def _attn_prompt_kernel(it_ref, jt_ref, lam_ref, q_ref, k_ref, vt_ref, bias_ref, gain_ref, o_ref,
                        m_s, acc_s, s_s, *, tq, out_scale):
    i = it_ref[pl.program_id(1)]
    j = jt_ref[pl.program_id(1)]
    W = ATTN_STRIP
    KC = ATTN_KEY_CHUNK
    nstrip = tq // W

    @pl.when(j == 0)
    def _():
        m_s[...] = jnp.full_like(m_s, NEG_INF)
        acc_s[...] = jnp.zeros_like(acc_s)

    def tile(kind):
        strips = [(mp, rb) for mp in range(2) for rb in range(nstrip)]

        def nkeys(rb):
            return (rb + 1) * W if kind == 0 else tq

        def scores(idx):
            mp, rb = strips[idx]
            qs = q_ref[mp, rb * W:(rb + 1) * W, :]
            chunks = [_dot_nt(k_ref[c * KC:(c + 1) * KC, :], qs) for c in range(nkeys(rb) // KC)]

            def add_bias(key_block, b):
                for c in range(key_block * W // KC, (key_block + 1) * W // KC):
                    off = c * KC - key_block * W
                    chunks[c] = chunks[c] + b[off:off + KC, :]

            if kind == 0:
                add_bias(rb, bias_ref[0, 0])
                if rb >= 1:
                    add_bias(rb - 1, bias_ref[0, 1])
            elif kind == 1 and rb == 0:
                add_bias(nstrip - 1, bias_ref[0, 1])
            for c, s in enumerate(chunks):
                s_s[idx % 2, c * KC:(c + 1) * KC, :] = s

        def consume(idx):
            mp, rb = strips[idx]
            nchunks = nkeys(rb) // KC
            cols = slice(mp * tq + rb * W, mp * tq + (rb + 1) * W)
            chunks = [s_s[idx % 2, c * KC:(c + 1) * KC, :] for c in range(nchunks)]
            m_old = m_s[:, cols]
            m_new = m_old
            for s in chunks:
                m_new = jnp.maximum(m_new, jnp.max(s, axis=0, keepdims=True))
            alpha = jnp.exp2(m_old - m_new)
            ps = [jnp.exp2(s - m_new).astype(BF16) for s in chunks]
            pcat = jnp.concatenate(ps, axis=0) if nchunks > 1 else ps[0]
            acc_s[:, cols] = alpha * acc_s[:, cols] + _dot(vt_ref[:, 0:nchunks * KC], pcat)
            m_s[:, cols] = m_new

        scores(0)
        for idx in range(len(strips)):
            if idx + 1 < len(strips):
                scores(idx + 1)
            consume(idx)

    @pl.when(j < i - 1)
    def _():
        tile(2)

    @pl.when(j == i - 1)
    def _():
        tile(1)

    @pl.when(j == i)
    def _():
        tile(0)
        n = acc_s[0:DV_C, :] / acc_s[DV_C:DV_C + 1, :]
        o = n[:, 0:tq] - lam_ref[0] * n[:, tq:]
        o = o * lax.rsqrt(jnp.mean(o * o, axis=0, keepdims=True) + EPS) * gain_ref[...] * out_scale
        o_ref[...] = o.T.astype(BF16)


def attn_prompt(lam, q2, k, vt, bias, gain_col, tq, out_scale):
    T = k.shape[0]
    nq = T // tq
    pairs = [(i, j) for i in range(nq) for j in range(i + 1)]
    itab = jnp.asarray(np.array([p[0] for p in pairs], np.int32))
    jtab = jnp.asarray(np.array([p[1] for p in pairs], np.int32))
    grid_spec = pltpu.PrefetchScalarGridSpec(
        num_scalar_prefetch=2,
        grid=(H_C, len(pairs)),
        in_specs=[
            pl.BlockSpec(memory_space=pltpu.SMEM),
            pl.BlockSpec((2, tq, DV_C), lambda h, p, it, jt: (0, it[p], h)),
            pl.BlockSpec((tq, DV_C), lambda h, p, it, jt: (jt[p], h)),
            pl.BlockSpec((DV_C + ATTN_ONES_ROWS, tq), lambda h, p, it, jt: (h, jt[p])),
            pl.BlockSpec((1, 2, ATTN_STRIP, ATTN_STRIP), lambda h, p, it, jt: (h, 0, 0, 0)),
            pl.BlockSpec((DV_C, 1), lambda h, p, it, jt: (0, 0)),
        ],
        out_specs=pl.BlockSpec((tq, DV_C), lambda h, p, it, jt: (it[p], h)),
        scratch_shapes=[
            pltpu.VMEM((1, 2 * tq), F32),
            pltpu.VMEM((DV_C + ATTN_ONES_ROWS, 2 * tq), F32),
            pltpu.VMEM((2, tq, ATTN_STRIP), F32),
        ],
    )
    return pl.pallas_call(
        functools.partial(_attn_prompt_kernel, tq=tq, out_scale=out_scale),
        grid_spec=grid_spec,
        out_shape=jax.ShapeDtypeStruct((T, H_C * DV_C), BF16),
        compiler_params=_cparams(("arbitrary", "arbitrary")),
        name="attn_prompt",
    )(itab, jtab, lam, q2, k, vt, bias, gain_col)


PAGES_PER_STEP = 8
PAGE_GROUP = 2
ROWS_PER_HEAD = 16


def _attn_sample_kernel(pt_ref, lam_ref, q_ref, *refs, out_scale):
    P = PAGES_PER_STEP
    G = PAGE_GROUP
    R = ROWS_PER_HEAD
    ngroups = P // G
    k_refs = refs[0:P]
    v_refs = refs[P:2 * P]
    kn_ref, vn_ref, bias_last_ref, bias_new_ref, gain_ref, o_ref, m_s, l_s, acc_s, s_s = refs[2 * P:]
    j = pl.program_id(1)
    nj = pl.num_programs(1)

    @pl.when(j == 0)
    def _():
        m_s[...] = jnp.full_like(m_s, NEG_INF)
        l_s[...] = jnp.zeros_like(l_s)
        acc_s[...] = jnp.zeros_like(acc_s)

    def head_rows(page_refs, p, h):
        return page_refs[p][0, pl.ds(h, PAGE_SIZE, stride=H_C), :].astype(BF16)

    def scores(g):
        for h in range(H_C):
            kcat = jnp.concatenate([head_rows(k_refs, g * G + t, h) for t in range(G)], axis=0)
            s_s[g, h * R:(h + 1) * R, :] = _dot_nt(q_ref[0, h * R:(h + 1) * R, :], kcat)

    def update(s, v_of_head):
        m_old = m_s[...]
        m_new = jnp.maximum(m_old, jnp.max(s, axis=-1, keepdims=True))
        alpha = jnp.exp2(m_old - m_new)
        p = jnp.exp2(s - m_new).astype(BF16)
        l_s[...] = alpha * l_s[...] + jnp.sum(p.astype(F32), axis=-1, keepdims=True)
        for h in range(H_C):
            rows = slice(h * R, (h + 1) * R)
            acc_s[rows, :] = alpha[rows] * acc_s[rows, :] + _dot(p[rows, :], v_of_head(h))
        m_s[...] = m_new

    def v_group(g):
        return lambda h: jnp.concatenate([head_rows(v_refs, g * G + t, h) for t in range(G)], axis=0)

    @pl.when(j < nj - 1)
    def _():
        for g in range(ngroups):
            scores(g)
        for g in range(ngroups):
            update(s_s[g], v_group(g))

    @pl.when(j == nj - 1)
    def _():
        for g in range(ngroups):
            scores(g)
        for g in range(ngroups):
            s = s_s[g]
            if g == ngroups - 1:
                s = s + bias_last_ref[...]
            update(s, v_group(g))
        for h in range(H_C):
            kh = kn_ref[0, :, h * DV_C:(h + 1) * DV_C]
            s_s[0, h * R:(h + 1) * R, 0:PAGE_SIZE] = _dot_nt(q_ref[0, h * R:(h + 1) * R, :], kh)
        update(s_s[0, :, 0:PAGE_SIZE] + bias_new_ref[...], lambda h: vn_ref[0, :, h * DV_C:(h + 1) * DV_C])
        n = acc_s[...] / l_s[...]
        for h in range(H_C):
            o = n[h * R:h * R + 8, :] - lam_ref[0] * n[h * R + 8:(h + 1) * R, :]
            o_ref[0, h * 8:(h + 1) * 8, :] = _rms(o, gain_ref[...]) * out_scale


def attn_sample(page_table, lam, qm, cache_k, cache_v, k_new, v_new, bias_last, bias_new, gain, out_scale):
    B = qm.shape[0]
    P = PAGES_PER_STEP
    n_pages = page_table.shape[1]
    nj = n_pages // P
    rows = H_C * ROWS_PER_HEAD
    page_rows = PAGE_SIZE * H_C

    def page_spec(p):
        return pl.BlockSpec((1, page_rows, DV_C), lambda b, j, pt, p=p: (pt[b, j * P + p], 0, 0))

    grid_spec = pltpu.PrefetchScalarGridSpec(
        num_scalar_prefetch=1,
        grid=(B, nj),
        in_specs=[
            pl.BlockSpec(memory_space=pltpu.SMEM),
            pl.BlockSpec((1, rows, DV_C), lambda b, j, pt: (b, 0, 0)),
            *[page_spec(p) for p in range(P)],
            *[page_spec(p) for p in range(P)],
            pl.BlockSpec((1, PAGE_SIZE, H_C * DV_C), lambda b, j, pt: (b, 0, 0)),
            pl.BlockSpec((1, PAGE_SIZE, H_C * DV_C), lambda b, j, pt: (b, 0, 0)),
            pl.BlockSpec((rows, PAGE_GROUP * PAGE_SIZE), lambda b, j, pt: (0, 0)),
            pl.BlockSpec((rows, PAGE_SIZE), lambda b, j, pt: (0, 0)),
            pl.BlockSpec((1, DV_C), lambda b, j, pt: (0, 0)),
        ],
        out_specs=pl.BlockSpec((1, H_C * 8, DV_C), lambda b, j, pt: (b, 0, 0)),
        scratch_shapes=[
            pltpu.VMEM((rows, 1), F32),
            pltpu.VMEM((rows, 1), F32),
            pltpu.VMEM((rows, DV_C), F32),
            pltpu.VMEM((P // PAGE_GROUP, rows, PAGE_GROUP * PAGE_SIZE), F32),
        ],
    )
    return pl.pallas_call(
        functools.partial(_attn_sample_kernel, out_scale=out_scale),
        grid_spec=grid_spec,
        out_shape=jax.ShapeDtypeStruct((B, H_C * 8, DV_C), F32),
        compiler_params=_cparams(("arbitrary", "arbitrary")),
        name="attn_sample",
    )(page_table, lam, qm, *([cache_k] * P), *([cache_v] * P), k_new, v_new, bias_last, bias_new, gain)


def _out_proj_kernel(a_ref, w_ref, x_ref, o_ref):
    o_ref[...] = x_ref[...] + _dot(a_ref[...], w_ref[...])


def out_proj(a, w, x, tm):
    M = x.shape[0]
    row_spec = pl.BlockSpec((tm, D_MODEL), lambda i: (i, 0))
    return pl.pallas_call(
        _out_proj_kernel,
        grid=(M // tm,),
        in_specs=[row_spec, _const_spec((D_MODEL, D_MODEL)), row_spec],
        out_specs=row_spec,
        out_shape=jax.ShapeDtypeStruct((M, D_MODEL), F32),
        compiler_params=_cparams(("arbitrary",)),
        name="out_proj",
    )(a, w, x)


def _t5_bucket_table():
    n = np.arange(MAX_DIST + 1)
    max_exact = N_BUCKETS // 2
    nf = np.maximum(n, 1).astype(np.float32)
    large = max_exact + (np.log(nf / max_exact) / math.log(MAX_DIST / max_exact) * (N_BUCKETS - max_exact)).astype(np.int32)
    large = np.minimum(large, N_BUCKETS - 1)
    return np.where(n < max_exact, n, large).astype(np.int32)


def _rel_bias_minus_far(rel_bias, rel):
    tab = _near_bias_table(rel_bias)
    vals = tab[np.clip(rel, 0, MAX_DIST)]
    vals = jnp.where(jnp.asarray(rel >= 0)[..., None], vals, NEG_INF)
    return jnp.moveaxis(vals, -1, 0).astype(F32)


def _near_bias_table(rel_bias):
    tab = rel_bias[_t5_bucket_table()]
    return ((tab - tab[MAX_DIST][None, :]) * LOG2E).astype(F32)


def _bias_blocks_t(rel_bias):
    W = ATTN_STRIP
    H = rel_bias.shape[1]
    f = jnp.concatenate([_near_bias_table(rel_bias), jnp.zeros((W - MAX_DIST - 1, H), F32)], axis=0).T
    g0 = jnp.concatenate([f, jnp.full((H, W), NEG_INF, F32)], axis=1)
    g1 = jnp.concatenate([jnp.zeros((H, W), F32), f], axis=1)
    g = jnp.stack([g0, g1], axis=1)
    rep = jnp.tile(g, (1, 1, W))[:, :, :W * (2 * W - 1)].reshape(H, 2, W, 2 * W - 1)
    return rep[:, :, :, :W]


TM_PROMPT = 512
TM_FFN = 256
TQ = 2048


def kernel(x_prompt, x_sample, state_mlstm_C, state_mlstm_n, state_mlstm_m, state_pool, cache_k, cache_v, state_ffn_conv, page_table, norm_mix, norm_ffn, norm_final, w_in_e, b_gate_e, mlstm_gain, w_pool, pool_scale, w_out_e, w_in_o, lambda_q1, lambda_k1, lambda_q2, lambda_k2, subln_gain, rel_bias, w_out_o, w_up, conv_w, conv_b, w_down):
    Bp, Tp = x_prompt.shape[:2]
    Bs, Ts = x_sample.shape[:2]
    assert Bp == 1
    Ms = Bs * Ts
    xp = x_prompt.reshape(Tp, D_MODEL)
    xs = x_sample.reshape(Ms, D_MODEL)
    row = lambda a: a.reshape(1, -1)

    w_in = w_in_e[0]
    n_gate = 2 * H_A
    w_main = jnp.concatenate([w_in[:, :4 * D_A], w_in[:, 4 * D_A + n_gate:], w_in[:, 4 * D_A:4 * D_A + n_gate],
                              jnp.zeros((D_MODEL, 128 - n_gate), F32)], axis=1).astype(BF16)
    b_col = jnp.pad(b_gate_e[0], (0, 128 - n_gate)).reshape(1, 128)
    g_mix0 = row(norm_mix[0])
    wp_b = w_pool[0].astype(BF16)
    wo_e = w_out_e[0].astype(BF16)
    gain_e = row(mlstm_gain[0])
    ps_e = row(pool_scale[0])

    qkv_p, ogu_p, gc_p, gr_p = proj_even(xp, g_mix0, w_main, b_col, TM_PROMPT)
    zc = jnp.zeros((1, H_A, DK_A, DK_A), F32)
    zn = jnp.zeros((1, H_A, DK_A), F32)
    zm = jnp.zeros((1, 8, 128), F32)
    hh_p, C_p, n_p, m_p = mlstm(qkv_p[None], ogu_p[None], gc_p[None], gr_p[None], zc, zn, zm, gain_e, MLSTM_CHUNK)
    xp = pool_out(hh_p, ogu_p[None], jnp.zeros((1, POOL_HDR, D_B), F32), wp_b, ps_e, wo_e, xp[None], TM_PROMPT, 0)[0]
    pool_p = ogu_p[Tp - POOL_BUF:, D_A:][None]

    L = MLSTM_CHUNK
    qkv_s, ogu_s, gc_s, gr_s = proj_even(xs, g_mix0, w_main, b_col, Ms)
    pad_t = lambda a, n: jnp.pad(a.reshape(Bs, Ts, a.shape[-1]), ((0, 0), (0, n - Ts), (0, 0)))
    gr_s3 = jnp.pad(gr_s.reshape(8, Bs, Ts).transpose(1, 0, 2), ((0, 0), (0, 0), (0, L - Ts)))
    m0_s = jnp.broadcast_to(jnp.pad(state_mlstm_m[0], ((0, 0), (0, 8 - H_A)))[:, :, None], (Bs, 8, 128))
    hh_s, C_s, n_s, m_s = mlstm(pad_t(qkv_s, L), pad_t(ogu_s, L), pad_t(gc_s, L), gr_s3,
                                state_mlstm_C[0], state_mlstm_n[0], m0_s, gain_e, Ts)
    prev16 = jnp.pad(state_pool[0], ((0, 0), (POOL_HDR - POOL_BUF, 0), (0, 0)))
    xs = pool_out(hh_s[:, :16], pad_t(ogu_s, 16), prev16, wp_b, ps_e, wo_e, pad_t(xs, 16), 16, PAST_LEN)[:, :Ts].reshape(Ms, D_MODEL)
    pool_s = jnp.concatenate([state_pool[0], ogu_s[:, D_A:].reshape(Bs, Ts, D_B)], axis=1)[:, -POOL_BUF:]

    def run_ffn(l, xp, xs, final_norm):
        g = row(norm_ffn[l])
        wu = w_up[l].astype(BF16)
        wd = w_down[l].astype(BF16)
        cb = row(conv_b[l])
        gf = row(norm_final)
        zp = jnp.zeros((8, D_FF), F32)
        xp, st_p = ffn(xp, g, wu, conv_w[l], cb, wd, zp, zp, gf, TM_FFN, TM_FFN, True, final_norm)
        st = state_ffn_conv[l]
        z1 = jnp.zeros((Bs, 1, D_FF), F32)
        p1 = jnp.concatenate([st[:, 1:2], z1, z1, z1], axis=1).reshape(Ms, D_FF)
        p2 = jnp.concatenate([st[:, 0:1], st[:, 1:2], z1, z1], axis=1).reshape(Ms, D_FF)
        xs, a_s = ffn(xs, g, wu, conv_w[l], cb, wd, p1, p2, gf, Ms, Ts, False, final_norm)
        conv_p = st_p[8 - (CONV_W - 1):][None]
        conv_s = a_s.reshape(Bs, Ts, D_FF)[:, Ts - (CONV_W - 1):]
        return xp, xs, conv_p, conv_s

    xp, xs, conv_p0, conv_s0 = run_ffn(0, xp, xs, False)

    lam_init = 0.8 - 0.6 * math.exp(-0.3 * 1)
    lam = (jnp.exp(jnp.sum(lambda_q1[0] * lambda_k1[0])) - jnp.exp(jnp.sum(lambda_q2[0] * lambda_k2[0])) + lam_init).astype(F32).reshape(1)
    out_scale = 1.0 - lam_init
    g_mix1 = row(norm_mix[1])
    w_qkv = w_in_o[0].astype(BF16)
    wo_o = w_out_o[0].astype(BF16)
    gain_o = row(subln_gain[0])

    q2_p, kf_p, vf_p, kb_p, vb_p = proj_odd(xp, g_mix1, w_qkv, TM_PROMPT)
    vt_p = jnp.concatenate([vb_p.T.reshape(H_C, DV_C, Tp), jnp.ones((H_C, ATTN_ONES_ROWS, Tp), BF16)], axis=1)
    vt_p = vt_p.reshape(H_C * (DV_C + ATTN_ONES_ROWS), Tp)
    o_p = attn_prompt(lam, q2_p, kb_p, vt_p, _bias_blocks_t(rel_bias), gain_o.reshape(DV_C, 1), TQ, out_scale)
    xp = out_proj(o_p, wo_o, xp, TM_PROMPT)

    q2_s, kf_s, vf_s, kb_s, vb_s = proj_odd(xs, g_mix1, w_qkv, Ms)
    qm = q2_s.reshape(2, Bs, Ts, H_C, DV_C).transpose(1, 3, 0, 2, 4)
    qm = jnp.pad(qm, ((0, 0), (0, 0), (0, 0), (0, 8 - Ts), (0, 0)))
    qm = qm.reshape(Bs, H_C * ROWS_PER_HEAD, DV_C)
    tok = np.minimum(np.arange(8), Ts - 1)
    tok = np.tile(tok, 2)
    ccol = np.arange(PAGE_SIZE)
    rel_last = PAGE_SIZE + tok[:, None] - ccol[None, :]
    rel_new = np.where(ccol[None, :] < Ts, tok[:, None] - ccol[None, :], -1)
    bias_last = _rel_bias_minus_far(rel_bias, rel_last).reshape(H_C * ROWS_PER_HEAD, PAGE_SIZE)
    bias_last = jnp.pad(bias_last, ((0, 0), ((PAGE_GROUP - 1) * PAGE_SIZE, 0)))
    bias_new = _rel_bias_minus_far(rel_bias, rel_new).reshape(H_C * ROWS_PER_HEAD, PAGE_SIZE)
    n_phys = cache_k.shape[1]
    ck = cache_k[0].reshape(n_phys, PAGE_SIZE * H_C, DV_C)
    cv = cache_v[0].reshape(n_phys, PAGE_SIZE * H_C, DV_C)
    kn = jnp.pad(kb_s.reshape(Bs, Ts, D_MODEL), ((0, 0), (0, PAGE_SIZE - Ts), (0, 0)))
    vn = jnp.pad(vb_s.reshape(Bs, Ts, D_MODEL), ((0, 0), (0, PAGE_SIZE - Ts), (0, 0)))
    o_s = attn_sample(page_table, lam, qm, ck, cv, kn, vn, bias_last, bias_new, gain_o, out_scale)
    o_s = o_s.reshape(Bs, H_C, 8, DV_C)[:, :, :Ts].transpose(0, 2, 1, 3).reshape(Ms, D_MODEL).astype(BF16)
    xs = out_proj(o_s, wo_o, xs, Ms)

    yp, ys, conv_p1, conv_s1 = run_ffn(1, xp, xs, True)

    y_prompt = yp.reshape(Bp, Tp, D_MODEL)
    y_sample = ys.reshape(Bs, Ts, D_MODEL)
    new_m_p = m_p[:, :H_A, 0]
    new_m_s = m_s[:, :H_A, 0]
    new_k_p = kf_p.reshape(1, Bp, Tp, H_C, DV_C)
    new_v_p = vf_p.reshape(1, Bp, Tp, H_C, DV_C)
    new_k_s = kf_s.reshape(1, Bs, Ts, H_C, DV_C)
    new_v_s = vf_s.reshape(1, Bs, Ts, H_C, DV_C)
    return (y_prompt, y_sample,
            C_p[None], n_p[None], new_m_p[None], pool_p[None], new_k_p, new_v_p,
            jnp.stack([conv_p0, conv_p1]),
            C_s[None], n_s[None], new_m_s[None], pool_s[None], new_k_s, new_v_s,
            jnp.stack([conv_s0, conv_s1]))
```

```python
import functools
import math

import numpy as np
import jax
import jax.numpy as jnp
from jax import lax
from jax.experimental import pallas as pl
from jax.experimental.pallas import tpu as pltpu

F32 = jnp.float32
BF16 = jnp.bfloat16
HIGHEST = lax.Precision.HIGHEST

D_MODEL = 1024
PAST_LEN = 16384
PAGE_SIZE = 128
D_A = 512
H_A = 4
DK_A = 128
MLSTM_CHUNK = 128
D_B = 512
POOL_WINDOWS = (2, 4, 8, 16)
G_B = 128
POOL_BUF = 15
POOL_HDR = 16
H_C = 8
DC = 64
DV_C = 128
N_BUCKETS = 32
MAX_DIST = 128
LOG2E = math.log2(math.e)
SCORE_SCALE = DC ** -0.5 * LOG2E
D_FF = 2816
CONV_W = 3
EPS = 1e-6

VMEM_LIMIT = 56 * 1024 * 1024
NEG_INF = float("-inf")


def _cparams(sem):
    return pltpu.CompilerParams(dimension_semantics=sem, vmem_limit_bytes=VMEM_LIMIT)


def _const_spec(shape):
    nd = len(shape)
    return pl.BlockSpec(shape, lambda *_: (0,) * nd, pipeline_mode=pl.Buffered(1))


def _rms(x, g):
    return x * lax.rsqrt(jnp.mean(x * x, axis=-1, keepdims=True) + EPS) * g


def _dot(a, b):
    return jnp.dot(a, b, preferred_element_type=F32)


def _dot_nt(a, b):
    return lax.dot_general(a, b, (((1,), (1,)), ((), ())), preferred_element_type=F32)


def _dot_tn(a, b):
    return lax.dot_general(a, b, (((0,), (0,)), ((), ())), preferred_element_type=F32)


def _log_sigmoid(x):
    return jnp.minimum(x, 0.0) - jnp.log1p(jnp.exp(-jnp.abs(x)))


def _sigmoid(x):
    return 1.0 / (1.0 + jnp.exp(-x))


def _proj_even_kernel(x_ref, g_ref, w_ref, bc_ref, qkv_ref, ogu_ref, gc_ref, gr_ref):
    h = _rms(x_ref[...], g_ref[...])
    z = _dot(h.astype(BF16), w_ref[...])
    qkv_ref[:, 0:D_A] = z[:, 0:D_A].astype(BF16)
    qkv_ref[:, D_A:2 * D_A] = (z[:, D_A:2 * D_A] * (DK_A ** -0.5)).astype(BF16)
    qkv_ref[:, 2 * D_A:3 * D_A] = z[:, 2 * D_A:3 * D_A].astype(BF16)
    ogu_ref[...] = z[:, 3 * D_A:3 * D_A + D_A + D_B]
    gc = z[:, 3 * D_A + D_A + D_B:] + bc_ref[...]
    gc_ref[...] = gc
    gr_ref[...] = gc.T[0:8, :]


def proj_even(x, g, w_main, b_col, tm):
    M = x.shape[0]
    n_main = w_main.shape[1]
    return pl.pallas_call(
        _proj_even_kernel,
        grid=(M // tm,),
        in_specs=[
            pl.BlockSpec((tm, D_MODEL), lambda i: (i, 0)),
            _const_spec((1, D_MODEL)),
            _const_spec((D_MODEL, n_main)),
            _const_spec((1, 128)),
        ],
        out_specs=[
            pl.BlockSpec((tm, 3 * D_A), lambda i: (i, 0)),
            pl.BlockSpec((tm, D_A + D_B), lambda i: (i, 0)),
            pl.BlockSpec((tm, 128), lambda i: (i, 0)),
            pl.BlockSpec((8, tm), lambda i: (0, i)),
        ],
        out_shape=[
            jax.ShapeDtypeStruct((M, 3 * D_A), BF16),
            jax.ShapeDtypeStruct((M, D_A + D_B), F32),
            jax.ShapeDtypeStruct((M, 128), F32),
            jax.ShapeDtypeStruct((8, M), F32),
        ],
        compiler_params=_cparams(("arbitrary",)),
        name="proj_even",
    )(x, g, w_main, b_col)


def _mlstm_kernel(qkv_ref, og_ref, gc_ref, gr_ref, c0_ref, n0_ref, m0_ref, gain_ref,
                  hh_ref, c_out_ref, n_out_ref, m_out_ref, c_s, n_s, m_s, *, L, valid):
    c = pl.program_id(1)

    @pl.when(c == 0)
    def _():
        c_s[...] = c0_ref[0]
        n_s[...] = n0_ref[0]
        m_s[...] = m0_ref[0]

    row = lax.broadcasted_iota(jnp.int32, (L, L), 0)
    col = lax.broadcasted_iota(jnp.int32, (L, L), 1)
    tri = (col <= row).astype(F32)
    mask = (col <= row) & (col < valid)
    rvalid = lax.broadcasted_iota(jnp.int32, (L, 1), 0) < valid
    cvalid = lax.broadcasted_iota(jnp.int32, (1, L), 1) < valid

    gcol = gc_ref[0]
    grow = gr_ref[0]
    lf_col = jnp.where(rvalid, _log_sigmoid(gcol), 0.0)
    lf_row = jnp.where(cvalid, _log_sigmoid(grow), 0.0)
    b_col_all = jnp.dot(tri, lf_col, precision=HIGHEST, preferred_element_type=F32)
    b_row_all = lax.dot_general(lf_row, tri, (((1,), (1,)), ((), ())), precision=HIGHEST,
                                preferred_element_type=F32)

    for h in range(H_A):
        bc = b_col_all[:, H_A + h:H_A + h + 1]
        br = b_row_all[H_A + h:H_A + h + 1, :]
        igc = gcol[:, h:h + 1]
        igr = grow[h:h + 1, :]
        m0 = m_s[h:h + 1, 0:1]
        logd = jnp.where(mask, bc - br + igr, NEG_INF)
        log_inter = bc + m0
        m_t = jnp.maximum(log_inter, jnp.max(logd, axis=-1, keepdims=True))
        dm = jnp.exp(logd - m_t)
        w_inter = jnp.exp(log_inter - m_t)
        q = qkv_ref[0, :, h * DK_A:(h + 1) * DK_A]
        k = qkv_ref[0, :, D_A + h * DK_A:D_A + (h + 1) * DK_A]
        v = qkv_ref[0, :, 2 * D_A + h * DK_A:2 * D_A + (h + 1) * DK_A]
        s = _dot_nt(q, k) * dm
        c_old = c_s[h]
        n_old = n_s[h:h + 1, :]
        num = _dot(s.astype(BF16), v) + w_inter * _dot_nt(q, c_old.astype(BF16))
        qn = jnp.sum(q.astype(F32) * n_old, axis=-1, keepdims=True)
        ndot = jnp.sum(s, axis=-1, keepdims=True) + w_inter * qn
        denom = jnp.maximum(jnp.abs(ndot), jnp.exp(-m_t))
        hh = num / denom
        y = _rms(hh, gain_ref[:, h * DK_A:(h + 1) * DK_A]) * _sigmoid(og_ref[0, :, h * DK_A:(h + 1) * DK_A])
        hh_ref[0, :, h * DK_A:(h + 1) * DK_A] = y.astype(BF16)
        m_new = m_t[valid - 1:valid, :]
        b_last = bc[valid - 1:valid, :]
        w_s = jnp.where(rvalid, jnp.exp(b_last - bc + igc - m_new), 0.0)
        decay = jnp.exp(b_last + m0 - m_new)
        kf = k.astype(F32)
        vw = (v.astype(F32) * w_s).astype(BF16)
        c_s[h] = decay * c_old + _dot_tn(vw, k)
        n_s[h:h + 1, :] = decay * n_old + jnp.sum(kf * w_s, axis=0, keepdims=True)
        m_s[h:h + 1, :] = jnp.broadcast_to(m_new, (1, 128))

    @pl.when(c == pl.num_programs(1) - 1)
    def _():
        c_out_ref[0] = c_s[...]
        n_out_ref[0] = n_s[...]
        m_out_ref[0] = m_s[...]


def mlstm(qkv, ogu, gc, gr, c0, n0, m0, gain, L, valid):
    B, T = qkv.shape[:2]
    nc = T // L
    return pl.pallas_call(
        functools.partial(_mlstm_kernel, L=L, valid=valid),
        grid=(B, nc),
        in_specs=[
            pl.BlockSpec((1, L, 3 * D_A), lambda b, c: (b, c, 0)),
            pl.BlockSpec((1, L, D_A), lambda b, c: (b, c, 0)),
            pl.BlockSpec((1, L, 128), lambda b, c: (b, c, 0)),
            pl.BlockSpec((1, 8, L), lambda b, c: (b, 0, c)),
            pl.BlockSpec((1, H_A, DK_A, DK_A), lambda b, c: (b, 0, 0, 0)),
            pl.BlockSpec((1, H_A, DK_A), lambda b, c: (b, 0, 0)),
            pl.BlockSpec((1, 8, 128), lambda b, c: (b, 0, 0)),
            pl.BlockSpec((1, D_A), lambda b, c: (0, 0)),
        ],
        out_specs=[
            pl.BlockSpec((1, L, D_A), lambda b, c: (b, c, 0)),
            pl.BlockSpec((1, H_A, DK_A, DK_A), lambda b, c: (b, 0, 0, 0)),
            pl.BlockSpec((1, H_A, DK_A), lambda b, c: (b, 0, 0)),
            pl.BlockSpec((1, 8, 128), lambda b, c: (b, 0, 0)),
        ],
        out_shape=[
            jax.ShapeDtypeStruct((B, T, D_A), BF16),
            jax.ShapeDtypeStruct((B, H_A, DK_A, DK_A), F32),
            jax.ShapeDtypeStruct((B, H_A, DK_A), F32),
            jax.ShapeDtypeStruct((B, 8, 128), F32),
        ],
        scratch_shapes=[
            pltpu.VMEM((H_A, DK_A, DK_A), F32),
            pltpu.VMEM((H_A, DK_A), F32),
            pltpu.VMEM((8, 128), F32),
        ],
        compiler_params=_cparams(("arbitrary", "arbitrary")),
        name="mlstm",
    )(qkv, ogu, gc, gr, c0, n0, m0, gain)


def _pool_out_kernel(hh_ref, u_ref, prev_ref, wp_ref, ps_ref, wo_ref, x_ref, o_ref, e_s, *, tm, nt, pos0):
    t = pl.program_id(1)
    H = POOL_HDR

    @pl.when(t == 0)
    def _():
        e_s[0:H, :] = prev_ref[0]

    if nt > 1:
        @pl.when(t > 0)
        def _():
            e_s[0:H, :] = e_s[tm:tm + H, :]

    e_s[H:H + tm, :] = u_ref[0]
    pos = pos0 + t * tm + lax.broadcasted_iota(jnp.int32, (tm, 1), 0)
    ys = []
    for g, w in enumerate(POOL_WINDOWS):
        sl = slice(g * G_B, (g + 1) * G_B)
        cur = e_s[H:H + tm, sl]
        win = cur
        for j in range(1, w):
            win = win + e_s[H - j:H - j + tm, sl]
        cnt = jnp.minimum(pos + 1, w).astype(F32)
        pooled = win / cnt - cur
        ys.append(_dot(pooled.astype(BF16), wp_ref[g]))
    yb = jnp.concatenate(ys, axis=-1) * ps_ref[...]
    o_ref[0] = x_ref[0] + _dot(hh_ref[0], wo_ref[0:D_A, :]) + _dot(yb.astype(BF16), wo_ref[D_A:, :])


def pool_out(hh, ogu, prev16, w_pool, pool_scale, w_out, x, tm, pos0):
    B, T = x.shape[:2]
    nt = T // tm
    return pl.pallas_call(
        functools.partial(_pool_out_kernel, tm=tm, nt=nt, pos0=pos0),
        grid=(B, nt),
        in_specs=[
            pl.BlockSpec((1, tm, D_A), lambda b, t: (b, t, 0)),
            pl.BlockSpec((1, tm, D_B), lambda b, t: (b, t, 1)),
            pl.BlockSpec((1, POOL_HDR, D_B), lambda b, t: (b, 0, 0)),
            _const_spec((len(POOL_WINDOWS), G_B, G_B)),
            _const_spec((1, D_B)),
            _const_spec((D_A + D_B, D_MODEL)),
            pl.BlockSpec((1, tm, D_MODEL), lambda b, t: (b, t, 0)),
        ],
        out_specs=pl.BlockSpec((1, tm, D_MODEL), lambda b, t: (b, t, 0)),
        out_shape=jax.ShapeDtypeStruct((B, T, D_MODEL), F32),
        scratch_shapes=[pltpu.VMEM((POOL_HDR + tm, D_B), F32)],
        compiler_params=_cparams(("arbitrary", "arbitrary")),
        name="pool_out",
    )(hh, ogu, prev16, w_pool, pool_scale, w_out, x)


def _gelu_tanh(y):
    return 0.5 * y * (1.0 + jnp.tanh(math.sqrt(2.0 / math.pi) * (y + 0.044715 * (y * y * y))))


FFN_CHUNKS = (1024, 1024, 768)
assert sum(FFN_CHUNKS) == D_FF


def _ffn_kernel(x_ref, g_ref, wup_ref, cw_ref, cb_ref, wdn_ref, p1_ref, p2_ref, gf_ref,
                o_ref, st_ref, carry_s, a_s, g_s, *, tm, seq_len, carried, final_norm):
    i = pl.program_id(0)
    offs = [sum(FFN_CHUNKS[:c]) for c in range(len(FFN_CHUNKS) + 1)]
    nch = len(FFN_CHUNKS)
    x = x_ref[...]
    h = _rms(x, g_ref[...]).astype(BF16)
    t = lax.broadcasted_iota(jnp.int32, (tm, 1), 0) % seq_len

    if carried:
        @pl.when(i == 0)
        def _():
            carry_s[...] = jnp.zeros_like(carry_s)

    def up(c):
        w = FFN_CHUNKS[c]
        a_s[c % 2, :, 0:w] = _dot(h, wup_ref[:, offs[c]:offs[c + 1]])
        g_s[c % 2, :, 0:w] = _dot(h, wup_ref[:, D_FF + offs[c]:D_FF + offs[c + 1]])

    def act_down(c):
        cols = slice(offs[c], offs[c + 1])
        w = FFN_CHUNKS[c]
        a = a_s[c % 2, :, 0:w]
        s1 = jnp.where(t >= 1, pltpu.roll(a, 1, 0), 0.0)
        s2 = jnp.where(t >= 2, pltpu.roll(a, 2, 0), 0.0)
        if carried:
            prev0 = carry_s[6:7, cols]
            prev1 = carry_s[7:8, cols]
            s1 = s1 + jnp.where(t == 0, prev1, 0.0)
            s2 = s2 + jnp.where(t == 0, prev0, 0.0) + jnp.where(t == 1, prev1, 0.0)
            carry_s[:, cols] = a[tm - 8:tm, :]
            st_ref[:, cols] = a[tm - 8:tm, :]
        else:
            s1 = s1 + p1_ref[:, cols]
            s2 = s2 + p2_ref[:, cols]
            st_ref[:, cols] = a
        y = cb_ref[:, cols] + cw_ref[0:1, cols] * s2 + cw_ref[1:2, cols] * s1 + cw_ref[2:3, cols] * a
        act = (_gelu_tanh(y) * g_s[c % 2, :, 0:w]).astype(BF16)
        return _dot(act, wdn_ref[cols, :])

    up(0)
    for c in range(nch):
        if c + 1 < nch:
            up(c + 1)
        d = act_down(c)
        if c == 0:
            o_ref[...] = x + d
        else:
            o_ref[...] += d
    if final_norm:
        o_ref[...] = _rms(o_ref[...], gf_ref[...])


def ffn(x, g, w_up, conv_w, conv_b, w_down, p1, p2, g_final, tm, seq_len, carried, final_norm):
    M = x.shape[0]
    st_rows = 8 if carried else tm
    st_total = 8 if carried else M
    row_spec = lambda n: pl.BlockSpec((tm, n), lambda i: (i, 0))
    p_spec = _const_spec((8, D_FF)) if carried else row_spec(D_FF)
    return pl.pallas_call(
        functools.partial(_ffn_kernel, tm=tm, seq_len=seq_len, carried=carried, final_norm=final_norm),
        grid=(M // tm,),
        in_specs=[
            row_spec(D_MODEL),
            _const_spec((1, D_MODEL)),
            _const_spec((D_MODEL, 2 * D_FF)),
            _const_spec((CONV_W, D_FF)),
            _const_spec((1, D_FF)),
            _const_spec((D_FF, D_MODEL)),
            p_spec,
            p_spec,
            _const_spec((1, D_MODEL)),
        ],
        out_specs=[
            row_spec(D_MODEL),
            pl.BlockSpec((st_rows, D_FF), (lambda i: (0, 0)) if carried else (lambda i: (i, 0))),
        ],
        out_shape=[
            jax.ShapeDtypeStruct((M, D_MODEL), F32),
            jax.ShapeDtypeStruct((st_total, D_FF), F32),
        ],
        scratch_shapes=[
            pltpu.VMEM((8, D_FF), F32),
            pltpu.VMEM((2, tm, max(FFN_CHUNKS)), F32),
            pltpu.VMEM((2, tm, max(FFN_CHUNKS)), F32),
        ],
        compiler_params=_cparams(("arbitrary",)),
        name="ffn",
    )(x, g, w_up, conv_w, conv_b, w_down, p1, p2, g_final)


def _proj_odd_kernel(x_ref, g_ref, w_ref, q_ref, kf_ref, vf_ref, kb_ref, vb_ref):
    h = _rms(x_ref[...], g_ref[...]).astype(BF16)
    q = _dot(h, w_ref[:, 0:D_MODEL]) * SCORE_SCALE
    lane = lax.broadcasted_iota(jnp.int32, q.shape, 1) % DV_C
    q_ref[0] = jnp.where(lane < DC, q, 0.0).astype(BF16)
    q_ref[1] = jnp.where(lane >= DC, q, 0.0).astype(BF16)
    k = _dot(h, w_ref[:, D_MODEL:2 * D_MODEL])
    kf_ref[...] = k
    kb_ref[...] = k.astype(BF16)
    v = _dot(h, w_ref[:, 2 * D_MODEL:])
    vf_ref[...] = v
    vb_ref[...] = v.astype(BF16)


def proj_odd(x, g, w, tm):
    M = x.shape[0]
    row_spec = pl.BlockSpec((tm, D_MODEL), lambda i: (i, 0))
    return pl.pallas_call(
        _proj_odd_kernel,
        grid=(M // tm,),
        in_specs=[row_spec, _const_spec((1, D_MODEL)), _const_spec((D_MODEL, 3 * D_MODEL))],
        out_specs=[pl.BlockSpec((2, tm, D_MODEL), lambda i: (0, i, 0))] + [row_spec] * 4,
        out_shape=[
            jax.ShapeDtypeStruct((2, M, D_MODEL), BF16),
            jax.ShapeDtypeStruct((M, D_MODEL), F32),
            jax.ShapeDtypeStruct((M, D_MODEL), F32),
            jax.ShapeDtypeStruct((M, D_MODEL), BF16),
            jax.ShapeDtypeStruct((M, D_MODEL), BF16),
        ],
        compiler_params=_cparams(("arbitrary",)),
        name="proj_odd",
    )(x, g, w)


ATTN_STRIP = 256
ATTN_KEY_CHUNK = 256
ATTN_ONES_ROWS = 16


def _attn_prompt_kernel(it_ref, jt_ref, lam_ref, q_ref, k_ref, vt_ref, bias_ref, gain_ref, o_ref,
                        m_s, acc_s, s_s, *, tq, out_scale):
    i = it_ref[pl.program_id(1)]
    j = jt_ref[pl.program_id(1)]
    W = ATTN_STRIP
    KC = ATTN_KEY_CHUNK
    nstrip = tq // W

    @pl.when(j == 0)
    def _():
        m_s[...] = jnp.full_like(m_s, NEG_INF)
        acc_s[...] = jnp.zeros_like(acc_s)

    def tile(kind):
        strips = [(mp, rb) for mp in range(2) for rb in range(nstrip)]

        def nkeys(rb):
            return (rb + 1) * W if kind == 0 else tq

        def scores(idx):
            mp, rb = strips[idx]
            qs = q_ref[mp, rb * W:(rb + 1) * W, :]
            chunks = [_dot_nt(k_ref[c * KC:(c + 1) * KC, :], qs) for c in range(nkeys(rb) // KC)]

            def add_bias(key_block, b):
                for c in range(key_block * W // KC, (key_block + 1) * W // KC):
                    off = c * KC - key_block * W
                    chunks[c] = chunks[c] + b[off:off + KC, :]

            if kind == 0:
                add_bias(rb, bias_ref[0, 0])
                if rb >= 1:
                    add_bias(rb - 1, bias_ref[0, 1])
            elif kind == 1 and rb == 0:
                add_bias(nstrip - 1, bias_ref[0, 1])
            for c, s in enumerate(chunks):
                s_s[idx % 2, c * KC:(c + 1) * KC, :] = s

        def consume(idx):
            mp, rb = strips[idx]
            nchunks = nkeys(rb) // KC
            cols = slice(mp * tq + rb * W, mp * tq + (rb + 1) * W)
            chunks = [s_s[idx % 2, c * KC:(c + 1) * KC, :] for c in range(nchunks)]
            m_old = m_s[:, cols]
            m_new = m_old
            for s in chunks:
                m_new = jnp.maximum(m_new, jnp.max(s, axis=0, keepdims=True))
            alpha = jnp.exp2(m_old - m_new)
            ps = [jnp.exp2(s - m_new).astype(BF16) for s in chunks]
            pcat = jnp.concatenate(ps, axis=0) if nchunks > 1 else ps[0]
            acc_s[:, cols] = alpha * acc_s[:, cols] + _dot(vt_ref[:, 0:nchunks * KC], pcat)
            m_s[:, cols] = m_new

        scores(0)
        for idx in range(len(strips)):
            if idx + 1 < len(strips):
                scores(idx + 1)
            consume(idx)

    @pl.when(j < i - 1)
    def _():
        tile(2)

    @pl.when(j == i - 1)
    def _():
        tile(1)

    @pl.when(j == i)
    def _():
        tile(0)
        n = acc_s[0:DV_C, :] / acc_s[DV_C:DV_C + 1, :]
        o = n[:, 0:tq] - lam_ref[0] * n[:, tq:]
        o = o * lax.rsqrt(jnp.mean(o * o, axis=0, keepdims=True) + EPS) * gain_ref[...] * out_scale
        o_ref[...] = o.T.astype(BF16)


def attn_prompt(lam, q2, k, vt, bias, gain_col, tq, out_scale):
    T = k.shape[0]
    nq = T // tq
    pairs = [(i, j) for i in range(nq) for j in range(i + 1)]
    itab = jnp.asarray(np.array([p[0] for p in pairs], np.int32))
    jtab = jnp.asarray(np.array([p[1] for p in pairs], np.int32))
    grid_spec = pltpu.PrefetchScalarGridSpec(
        num_scalar_prefetch=2,
        grid=(H_C, len(pairs)),
        in_specs=[
            pl.BlockSpec(memory_space=pltpu.SMEM),
            pl.BlockSpec((2, tq, DV_C), lambda h, p, it, jt: (0, it[p], h)),
            pl.BlockSpec((tq, DV_C), lambda h, p, it, jt: (jt[p], h)),
            pl.BlockSpec((DV_C + ATTN_ONES_ROWS, tq), lambda h, p, it, jt: (h, jt[p])),
            pl.BlockSpec((1, 2, ATTN_STRIP, ATTN_STRIP), lambda h, p, it, jt: (h, 0, 0, 0)),
            pl.BlockSpec((DV_C, 1), lambda h, p, it, jt: (0, 0)),
        ],
        out_specs=pl.BlockSpec((tq, DV_C), lambda h, p, it, jt: (it[p], h)),
        scratch_shapes=[
            pltpu.VMEM((1, 2 * tq), F32),
            pltpu.VMEM((DV_C + ATTN_ONES_ROWS, 2 * tq), F32),
            pltpu.VMEM((2, tq, ATTN_STRIP), F32),
        ],
    )
    return pl.pallas_call(
        functools.partial(_attn_prompt_kernel, tq=tq, out_scale=out_scale),
        grid_spec=grid_spec,
        out_shape=jax.ShapeDtypeStruct((T, H_C * DV_C), BF16),
        compiler_params=_cparams(("arbitrary", "arbitrary")),
        name="attn_prompt",
    )(itab, jtab, lam, q2, k, vt, bias, gain_col)


PAGES_PER_STEP = 8
PAGE_GROUP = 2
ROWS_PER_HEAD = 16


def _attn_sample_kernel(pt_ref, lam_ref, q_ref, *refs, out_scale):
    P = PAGES_PER_STEP
    G = PAGE_GROUP
    R = ROWS_PER_HEAD
    ngroups = P // G
    k_refs = refs[0:P]
    v_refs = refs[P:2 * P]
    kn_ref, vn_ref, bias_last_ref, bias_new_ref, gain_ref, o_ref, m_s, l_s, acc_s, s_s = refs[2 * P:]
    j = pl.program_id(1)
    nj = pl.num_programs(1)

    @pl.when(j == 0)
    def _():
        m_s[...] = jnp.full_like(m_s, NEG_INF)
        l_s[...] = jnp.zeros_like(l_s)
        acc_s[...] = jnp.zeros_like(acc_s)

    def head_rows(page_refs, p, h):
        return page_refs[p][0, pl.ds(h, PAGE_SIZE, stride=H_C), :].astype(BF16)

    def scores(g):
        for h in range(H_C):
            kcat = jnp.concatenate([head_rows(k_refs, g * G + t, h) for t in range(G)], axis=0)
            s_s[g, h * R:(h + 1) * R, :] = _dot_nt(q_ref[0, h * R:(h + 1) * R, :], kcat)

    def update(s, v_of_head):
        m_old = m_s[...]
        m_new = jnp.maximum(m_old, jnp.max(s, axis=-1, keepdims=True))
        alpha = jnp.exp2(m_old - m_new)
        p = jnp.exp2(s - m_new).astype(BF16)
        l_s[...] = alpha * l_s[...] + jnp.sum(p.astype(F32), axis=-1, keepdims=True)
        for h in range(H_C):
            rows = slice(h * R, (h + 1) * R)
            acc_s[rows, :] = alpha[rows] * acc_s[rows, :] + _dot(p[rows, :], v_of_head(h))
        m_s[...] = m_new

    def v_group(g):
        return lambda h: jnp.concatenate([head_rows(v_refs, g * G + t, h) for t in range(G)], axis=0)

    @pl.when(j < nj - 1)
    def _():
        for g in range(ngroups):
            scores(g)
        for g in range(ngroups):
            update(s_s[g], v_group(g))

    @pl.when(j == nj - 1)
    def _():
        for g in range(ngroups):
            scores(g)
        for g in range(ngroups):
            s = s_s[g]
            if g == ngroups - 1:
                s = s + bias_last_ref[...]
            update(s, v_group(g))
        for h in range(H_C):
            kh = kn_ref[0, :, h * DV_C:(h + 1) * DV_C]
            s_s[0, h * R:(h + 1) * R, 0:PAGE_SIZE] = _dot_nt(q_ref[0, h * R:(h + 1) * R, :], kh)
        update(s_s[0, :, 0:PAGE_SIZE] + bias_new_ref[...], lambda h: vn_ref[0, :, h * DV_C:(h + 1) * DV_C])
        n = acc_s[...] / l_s[...]
        for h in range(H_C):
            o = n[h * R:h * R + 8, :] - lam_ref[0] * n[h * R + 8:(h + 1) * R, :]
            o_ref[0, h * 8:(h + 1) * 8, :] = _rms(o, gain_ref[...]) * out_scale


def attn_sample(page_table, lam, qm, cache_k, cache_v, k_new, v_new, bias_last, bias_new, gain, out_scale):
    B = qm.shape[0]
    P = PAGES_PER_STEP
    n_pages = page_table.shape[1]
    nj = n_pages // P
    rows = H_C * ROWS_PER_HEAD
    page_rows = PAGE_SIZE * H_C

    def page_spec(p):
        return pl.BlockSpec((1, page_rows, DV_C), lambda b, j, pt, p=p: (pt[b, j * P + p], 0, 0))

    grid_spec = pltpu.PrefetchScalarGridSpec(
        num_scalar_prefetch=1,
        grid=(B, nj),
        in_specs=[
            pl.BlockSpec(memory_space=pltpu.SMEM),
            pl.BlockSpec((1, rows, DV_C), lambda b, j, pt: (b, 0, 0)),
            *[page_spec(p) for p in range(P)],
            *[page_spec(p) for p in range(P)],
            pl.BlockSpec((1, PAGE_SIZE, H_C * DV_C), lambda b, j, pt: (b, 0, 0)),
            pl.BlockSpec((1, PAGE_SIZE, H_C * DV_C), lambda b, j, pt: (b, 0, 0)),
            pl.BlockSpec((rows, PAGE_GROUP * PAGE_SIZE), lambda b, j, pt: (0, 0)),
            pl.BlockSpec((rows, PAGE_SIZE), lambda b, j, pt: (0, 0)),
            pl.BlockSpec((1, DV_C), lambda b, j, pt: (0, 0)),
        ],
        out_specs=pl.BlockSpec((1, H_C * 8, DV_C), lambda b, j, pt: (b, 0, 0)),
        scratch_shapes=[
            pltpu.VMEM((rows, 1), F32),
            pltpu.VMEM((rows, 1), F32),
            pltpu.VMEM((rows, DV_C), F32),
            pltpu.VMEM((P // PAGE_GROUP, rows, PAGE_GROUP * PAGE_SIZE), F32),
        ],
    )
    return pl.pallas_call(
        functools.partial(_attn_sample_kernel, out_scale=out_scale),
        grid_spec=grid_spec,
        out_shape=jax.ShapeDtypeStruct((B, H_C * 8, DV_C), F32),
        compiler_params=_cparams(("arbitrary", "arbitrary")),
        name="attn_sample",
    )(page_table, lam, qm, *([cache_k] * P), *([cache_v] * P), k_new, v_new, bias_last, bias_new, gain)


def _out_proj_kernel(a_ref, w_ref, x_ref, o_ref):
    o_ref[...] = x_ref[...] + _dot(a_ref[...], w_ref[...])


def out_proj(a, w, x, tm):
    M = x.shape[0]
    row_spec = pl.BlockSpec((tm, D_MODEL), lambda i: (i, 0))
    return pl.pallas_call(
        _out_proj_kernel,
        grid=(M // tm,),
        in_specs=[row_spec, _const_spec((D_MODEL, D_MODEL)), row_spec],
        out_specs=row_spec,
        out_shape=jax.ShapeDtypeStruct((M, D_MODEL), F32),
        compiler_params=_cparams(("arbitrary",)),
        name="out_proj",
    )(a, w, x)


def _t5_bucket_table():
    n = np.arange(MAX_DIST + 1)
    max_exact = N_BUCKETS // 2
    nf = np.maximum(n, 1).astype(np.float32)
    large = max_exact + (np.log(nf / max_exact) / math.log(MAX_DIST / max_exact) * (N_BUCKETS - max_exact)).astype(np.int32)
    large = np.minimum(large, N_BUCKETS - 1)
    return np.where(n < max_exact, n, large).astype(np.int32)


def _rel_bias_minus_far(rel_bias, rel):
    tab = _near_bias_table(rel_bias)
    vals = tab[np.clip(rel, 0, MAX_DIST)]
    vals = jnp.where(jnp.asarray(rel >= 0)[..., None], vals, NEG_INF)
    return jnp.moveaxis(vals, -1, 0).astype(F32)


def _near_bias_table(rel_bias):
    tab = rel_bias[_t5_bucket_table()]
    return ((tab - tab[MAX_DIST][None, :]) * LOG2E).astype(F32)


def _bias_blocks_t(rel_bias):
    W = ATTN_STRIP
    H = rel_bias.shape[1]
    f = jnp.concatenate([_near_bias_table(rel_bias), jnp.zeros((W - MAX_DIST - 1, H), F32)], axis=0).T
    g0 = jnp.concatenate([f, jnp.full((H, W), NEG_INF, F32)], axis=1)
    g1 = jnp.concatenate([jnp.zeros((H, W), F32), f], axis=1)
    g = jnp.stack([g0, g1], axis=1)
    rep = jnp.tile(g, (1, 1, W))[:, :, :W * (2 * W - 1)].reshape(H, 2, W, 2 * W - 1)
    return rep[:, :, :, :W]


TM_PROMPT = 512
TM_FFN = 512
SAMPLE_PAD = 16
TQ = 2048


def kernel(x_prompt, x_sample, state_mlstm_C, state_mlstm_n, state_mlstm_m, state_pool, cache_k, cache_v, state_ffn_conv, page_table, norm_mix, norm_ffn, norm_final, w_in_e, b_gate_e, mlstm_gain, w_pool, pool_scale, w_out_e, w_in_o, lambda_q1, lambda_k1, lambda_q2, lambda_k2, subln_gain, rel_bias, w_out_o, w_up, conv_w, conv_b, w_down):
    Bp, Tp = x_prompt.shape[:2]
    Bs, Ts = x_sample.shape[:2]
    assert Bp == 1
    Ms = Bs * Ts
    xp = x_prompt.reshape(Tp, D_MODEL)
    xs = x_sample.reshape(Ms, D_MODEL)
    row = lambda a: a.reshape(1, -1)

    w_in = w_in_e[0]
    n_gate = 2 * H_A
    w_main = jnp.concatenate([w_in[:, :4 * D_A], w_in[:, 4 * D_A + n_gate:], w_in[:, 4 * D_A:4 * D_A + n_gate],
                              jnp.zeros((D_MODEL, 128 - n_gate), F32)], axis=1).astype(BF16)
    b_col = jnp.pad(b_gate_e[0], (0, 128 - n_gate)).reshape(1, 128)
    g_mix0 = row(norm_mix[0])
    wp_b = w_pool[0].astype(BF16)
    wo_e = w_out_e[0].astype(BF16)
    gain_e = row(mlstm_gain[0])
    ps_e = row(pool_scale[0])

    qkv_p, ogu_p, gc_p, gr_p = proj_even(xp, g_mix0, w_main, b_col, TM_PROMPT)
    zc = jnp.zeros((1, H_A, DK_A, DK_A), F32)
    zn = jnp.zeros((1, H_A, DK_A), F32)
    zm = jnp.zeros((1, 8, 128), F32)
    hh_p, C_p, n_p, m_p = mlstm(qkv_p[None], ogu_p[None], gc_p[None], gr_p[None], zc, zn, zm, gain_e,
                                MLSTM_CHUNK, MLSTM_CHUNK)
    xp = pool_out(hh_p, ogu_p[None], jnp.zeros((1, POOL_HDR, D_B), F32), wp_b, ps_e, wo_e, xp[None], TM_PROMPT, 0)[0]
    pool_p = ogu_p[Tp - POOL_BUF:, D_A:][None]

    L = SAMPLE_PAD
    qkv_s, ogu_s, gc_s, gr_s = proj_even(xs, g_mix0, w_main, b_col, Ms)
    pad_t = lambda a, n: jnp.pad(a.reshape(Bs, Ts, a.shape[-1]), ((0, 0), (0, n - Ts), (0, 0)))
    gr_s3 = jnp.pad(gr_s.reshape(8, Bs, Ts).transpose(1, 0, 2), ((0, 0), (0, 0), (0, L - Ts)))
    m0_s = jnp.broadcast_to(jnp.pad(state_mlstm_m[0], ((0, 0), (0, 8 - H_A)))[:, :, None], (Bs, 8, 128))
    ogu_s3 = pad_t(ogu_s, L)
    hh_s, C_s, n_s, m_s = mlstm(pad_t(qkv_s, L), ogu_s3, pad_t(gc_s, L), gr_s3,
                                state_mlstm_C[0], state_mlstm_n[0], m0_s, gain_e, L, Ts)
    prev16 = jnp.pad(state_pool[0], ((0, 0), (POOL_HDR - POOL_BUF, 0), (0, 0)))
    xs = pool_out(hh_s, ogu_s3, prev16, wp_b, ps_e, wo_e, pad_t(xs, L), L, PAST_LEN)[:, :Ts].reshape(Ms, D_MODEL)
    pool_s = jnp.concatenate([state_pool[0], ogu_s[:, D_A:].reshape(Bs, Ts, D_B)], axis=1)[:, -POOL_BUF:]

    def run_ffn(l, xp, xs, final_norm):
        g = row(norm_ffn[l])
        wu = w_up[l].astype(BF16)
        wd = w_down[l].astype(BF16)
        cb = row(conv_b[l])
        gf = row(norm_final)
        zp = jnp.zeros((8, D_FF), F32)
        xp, st_p = ffn(xp, g, wu, conv_w[l], cb, wd, zp, zp, gf, TM_FFN, TM_FFN, True, final_norm)
        st = state_ffn_conv[l]
        z1 = jnp.zeros((Bs, 1, D_FF), F32)
        p1 = jnp.concatenate([st[:, 1:2], z1, z1, z1], axis=1).reshape(Ms, D_FF)
        p2 = jnp.concatenate([st[:, 0:1], st[:, 1:2], z1, z1], axis=1).reshape(Ms, D_FF)
        xs, a_s = ffn(xs, g, wu, conv_w[l], cb, wd, p1, p2, gf, Ms, Ts, False, final_norm)
        conv_p = st_p[8 - (CONV_W - 1):][None]
        conv_s = a_s.reshape(Bs, Ts, D_FF)[:, Ts - (CONV_W - 1):]
        return xp, xs, conv_p, conv_s

    xp, xs, conv_p0, conv_s0 = run_ffn(0, xp, xs, False)

    lam_init = 0.8 - 0.6 * math.exp(-0.3 * 1)
    lam = (jnp.exp(jnp.sum(lambda_q1[0] * lambda_k1[0])) - jnp.exp(jnp.sum(lambda_q2[0] * lambda_k2[0])) + lam_init).astype(F32).reshape(1)
    out_scale = 1.0 - lam_init
    g_mix1 = row(norm_mix[1])
    w_qkv = w_in_o[0].astype(BF16)
    wo_o = w_out_o[0].astype(BF16)
    gain_o = row(subln_gain[0])

    q2_p, kf_p, vf_p, kb_p, vb_p = proj_odd(xp, g_mix1, w_qkv, TM_PROMPT)
    vt_p = jnp.concatenate([vb_p.T.reshape(H_C, DV_C, Tp), jnp.ones((H_C, ATTN_ONES_ROWS, Tp), BF16)], axis=1)
    vt_p = vt_p.reshape(H_C * (DV_C + ATTN_ONES_ROWS), Tp)
    o_p = attn_prompt(lam, q2_p, kb_p, vt_p, _bias_blocks_t(rel_bias), gain_o.reshape(DV_C, 1), TQ, out_scale)
    xp = out_proj(o_p, wo_o, xp, TM_PROMPT)

    q2_s, kf_s, vf_s, kb_s, vb_s = proj_odd(xs, g_mix1, w_qkv, Ms)
    qm = q2_s.reshape(2, Bs, Ts, H_C, DV_C).transpose(1, 3, 0, 2, 4)
    qm = jnp.pad(qm, ((0, 0), (0, 0), (0, 0), (0, 8 - Ts), (0, 0)))
    qm = qm.reshape(Bs, H_C * ROWS_PER_HEAD, DV_C)
    tok = np.minimum(np.arange(8), Ts - 1)
    tok = np.tile(tok, 2)
    ccol = np.arange(PAGE_SIZE)
    rel_last = PAGE_SIZE + tok[:, None] - ccol[None, :]
    rel_new = np.where(ccol[None, :] < Ts, tok[:, None] - ccol[None, :], -1)
    bias_last = _rel_bias_minus_far(rel_bias, rel_last).reshape(H_C * ROWS_PER_HEAD, PAGE_SIZE)
    bias_last = jnp.pad(bias_last, ((0, 0), ((PAGE_GROUP - 1) * PAGE_SIZE, 0)))
    bias_new = _rel_bias_minus_far(rel_bias, rel_new).reshape(H_C * ROWS_PER_HEAD, PAGE_SIZE)
    n_phys = cache_k.shape[1]
    ck = cache_k[0].reshape(n_phys, PAGE_SIZE * H_C, DV_C)
    cv = cache_v[0].reshape(n_phys, PAGE_SIZE * H_C, DV_C)
    kn = jnp.pad(kb_s.reshape(Bs, Ts, D_MODEL), ((0, 0), (0, PAGE_SIZE - Ts), (0, 0)))
    vn = jnp.pad(vb_s.reshape(Bs, Ts, D_MODEL), ((0, 0), (0, PAGE_SIZE - Ts), (0, 0)))
    o_s = attn_sample(page_table, lam, qm, ck, cv, kn, vn, bias_last, bias_new, gain_o, out_scale)
    o_s = o_s.reshape(Bs, H_C, 8, DV_C)[:, :, :Ts].transpose(0, 2, 1, 3).reshape(Ms, D_MODEL).astype(BF16)
    xs = out_proj(o_s, wo_o, xs, Ms)

    yp, ys, conv_p1, conv_s1 = run_ffn(1, xp, xs, True)

    y_prompt = yp.reshape(Bp, Tp, D_MODEL)
    y_sample = ys.reshape(Bs, Ts, D_MODEL)
    new_m_p = m_p[:, :H_A, 0]
    new_m_s = m_s[:, :H_A, 0]
    new_k_p = kf_p.reshape(1, Bp, Tp, H_C, DV_C)
    new_v_p = vf_p.reshape(1, Bp, Tp, H_C, DV_C)
    new_k_s = kf_s.reshape(1, Bs, Ts, H_C, DV_C)
    new_v_s = vf_s.reshape(1, Bs, Ts, H_C, DV_C)
    return (y_prompt, y_sample,
            C_p[None], n_p[None], new_m_p[None], pool_p[None], new_k_p, new_v_p,
            jnp.stack([conv_p0, conv_p1]),
            C_s[None], n_s[None], new_m_s[None], pool_s[None], new_k_s, new_v_s,
            jnp.stack([conv_s0, conv_s1]))
```

```python
import functools
import math

import numpy as np
import jax
import jax.numpy as jnp
from jax import lax
from jax.experimental import pallas as pl
from jax.experimental.pallas import tpu as pltpu

F32 = jnp.float32
BF16 = jnp.bfloat16
HIGHEST = lax.Precision.HIGHEST

D_MODEL = 1024
PAST_LEN = 16384
PAGE_SIZE = 128
D_A = 512
H_A = 4
DK_A = 128
MLSTM_CHUNK = 128
MLSTM_CHUNKS_PER_STEP = 2
D_B = 512
POOL_WINDOWS = (2, 4, 8, 16)
G_B = 128
POOL_BUF = 15
POOL_HDR = 16
H_C = 8
DC = 64
DV_C = 128
N_BUCKETS = 32
MAX_DIST = 128
LOG2E = math.log2(math.e)
SCORE_SCALE = DC ** -0.5 * LOG2E
D_FF = 2816
CONV_W = 3
EPS = 1e-6

VMEM_LIMIT = 56 * 1024 * 1024
NEG_INF = float("-inf")


def _cparams(sem):
    return pltpu.CompilerParams(dimension_semantics=sem, vmem_limit_bytes=VMEM_LIMIT)


def _const_spec(shape):
    nd = len(shape)
    return pl.BlockSpec(shape, lambda *_: (0,) * nd, pipeline_mode=pl.Buffered(1))


def _rms(x, g):
    return x * lax.rsqrt(jnp.mean(x * x, axis=-1, keepdims=True) + EPS) * g


def _dot(a, b):
    return jnp.dot(a, b, preferred_element_type=F32)


def _dot_nt(a, b):
    return lax.dot_general(a, b, (((1,), (1,)), ((), ())), preferred_element_type=F32)


def _dot_tn(a, b):
    return lax.dot_general(a, b, (((0,), (0,)), ((), ())), preferred_element_type=F32)


def _log_sigmoid(x):
    return jnp.minimum(x, 0.0) - jnp.log1p(jnp.exp(-jnp.abs(x)))


def _sigmoid(x):
    return 1.0 / (1.0 + jnp.exp(-x))


def _proj_even_kernel(x_ref, g_ref, w_ref, bc_ref, qkv_ref, ogu_ref, gc_ref, gr_ref):
    h = _rms(x_ref[...], g_ref[...])
    z = _dot(h.astype(BF16), w_ref[...])
    qkv_ref[:, 0:D_A] = z[:, 0:D_A].astype(BF16)
    qkv_ref[:, D_A:2 * D_A] = (z[:, D_A:2 * D_A] * (DK_A ** -0.5)).astype(BF16)
    qkv_ref[:, 2 * D_A:3 * D_A] = z[:, 2 * D_A:3 * D_A].astype(BF16)
    ogu_ref[...] = z[:, 3 * D_A:3 * D_A + D_A + D_B]
    gc = z[:, 3 * D_A + D_A + D_B:] + bc_ref[...]
    gc_ref[...] = gc
    gr_ref[...] = gc.T[0:8, :]


def proj_even(x, g, w_main, b_col, tm):
    M = x.shape[0]
    n_main = w_main.shape[1]
    return pl.pallas_call(
        _proj_even_kernel,
        grid=(M // tm,),
        in_specs=[
            pl.BlockSpec((tm, D_MODEL), lambda i: (i, 0)),
            _const_spec((1, D_MODEL)),
            _const_spec((D_MODEL, n_main)),
            _const_spec((1, 128)),
        ],
        out_specs=[
            pl.BlockSpec((tm, 3 * D_A), lambda i: (i, 0)),
            pl.BlockSpec((tm, D_A + D_B), lambda i: (i, 0)),
            pl.BlockSpec((tm, 128), lambda i: (i, 0)),
            pl.BlockSpec((8, tm), lambda i: (0, i)),
        ],
        out_shape=[
            jax.ShapeDtypeStruct((M, 3 * D_A), BF16),
            jax.ShapeDtypeStruct((M, D_A + D_B), F32),
            jax.ShapeDtypeStruct((M, 128), F32),
            jax.ShapeDtypeStruct((8, M), F32),
        ],
        compiler_params=_cparams(("arbitrary",)),
        name="proj_even",
    )(x, g, w_main, b_col)


def _mlstm_kernel(qkv_ref, og_ref, gc_ref, gr_ref, c0_ref, n0_ref, m0_ref, gain_ref,
                  hh_ref, c_out_ref, n_out_ref, m_out_ref, c_s, n_s, m_s, *, L, valid, chunks):
    c = pl.program_id(1)

    @pl.when(c == 0)
    def _():
        c_s[...] = c0_ref[0]
        n_s[...] = n0_ref[0]
        m_s[...] = m0_ref[0]

    row = lax.broadcasted_iota(jnp.int32, (L, L), 0)
    col = lax.broadcasted_iota(jnp.int32, (L, L), 1)
    tri = (col <= row).astype(F32)
    mask = (col <= row) & (col < valid)
    rvalid = lax.broadcasted_iota(jnp.int32, (L, 1), 0) < valid
    cvalid = lax.broadcasted_iota(jnp.int32, (1, L), 1) < valid

    c_st = [c_s[h] for h in range(H_A)]
    n_st = [n_s[h:h + 1, :] for h in range(H_A)]
    m_st = [m_s[h:h + 1, 0:1] for h in range(H_A)]

    for ci in range(chunks):
        rows = slice(ci * L, (ci + 1) * L)
        gcol = gc_ref[0, rows, :]
        grow = gr_ref[0, :, rows]
        lf_col = jnp.where(rvalid, _log_sigmoid(gcol), 0.0)
        lf_row = jnp.where(cvalid, _log_sigmoid(grow), 0.0)
        b_col_all = jnp.dot(tri, lf_col, precision=HIGHEST, preferred_element_type=F32)
        b_row_all = lax.dot_general(lf_row, tri, (((1,), (1,)), ((), ())), precision=HIGHEST,
                                    preferred_element_type=F32)
        for h in range(H_A):
            bc = b_col_all[:, H_A + h:H_A + h + 1]
            br = b_row_all[H_A + h:H_A + h + 1, :]
            igc = gcol[:, h:h + 1]
            igr = grow[h:h + 1, :]
            m0 = m_st[h]
            logd = jnp.where(mask, bc - br + igr, NEG_INF)
            log_inter = bc + m0
            m_t = jnp.maximum(log_inter, jnp.max(logd, axis=-1, keepdims=True))
            dm = jnp.exp(logd - m_t)
            w_inter = jnp.exp(log_inter - m_t)
            q = qkv_ref[0, rows, h * DK_A:(h + 1) * DK_A]
            k = qkv_ref[0, rows, D_A + h * DK_A:D_A + (h + 1) * DK_A]
            v = qkv_ref[0, rows, 2 * D_A + h * DK_A:2 * D_A + (h + 1) * DK_A]
            s = _dot_nt(q, k) * dm
            c_old = c_st[h]
            n_old = n_st[h]
            num = _dot(s.astype(BF16), v) + w_inter * _dot_nt(q, c_old.astype(BF16))
            qn = jnp.sum(q.astype(F32) * n_old, axis=-1, keepdims=True)
            ndot = jnp.sum(s, axis=-1, keepdims=True) + w_inter * qn
            denom = jnp.maximum(jnp.abs(ndot), jnp.exp(-m_t))
            hh = num / denom
            y = _rms(hh, gain_ref[:, h * DK_A:(h + 1) * DK_A]) * _sigmoid(og_ref[0, rows, h * DK_A:(h + 1) * DK_A])
            hh_ref[0, rows, h * DK_A:(h + 1) * DK_A] = y.astype(BF16)
            m_new = m_t[valid - 1:valid, :]
            b_last = bc[valid - 1:valid, :]
            w_s = jnp.where(rvalid, jnp.exp(b_last - bc + igc - m_new), 0.0)
            decay = jnp.exp(b_last + m0 - m_new)
            kf = k.astype(F32)
            vw = (v.astype(F32) * w_s).astype(BF16)
            c_st[h] = decay * c_old + _dot_tn(vw, k)
            n_st[h] = decay * n_old + jnp.sum(kf * w_s, axis=0, keepdims=True)
            m_st[h] = m_new

    for h in range(H_A):
        c_s[h] = c_st[h]
        n_s[h:h + 1, :] = n_st[h]
        m_s[h:h + 1, :] = jnp.broadcast_to(m_st[h], (1, 128))

    @pl.when(c == pl.num_programs(1) - 1)
    def _():
        c_out_ref[0] = c_s[...]
        n_out_ref[0] = n_s[...]
        m_out_ref[0] = m_s[...]


def mlstm(qkv, ogu, gc, gr, c0, n0, m0, gain, L, valid, chunks):
    B, T = qkv.shape[:2]
    LB = L * chunks
    nc = T // LB
    return pl.pallas_call(
        functools.partial(_mlstm_kernel, L=L, valid=valid, chunks=chunks),
        grid=(B, nc),
        in_specs=[
            pl.BlockSpec((1, LB, 3 * D_A), lambda b, c: (b, c, 0)),
            pl.BlockSpec((1, LB, D_A), lambda b, c: (b, c, 0)),
            pl.BlockSpec((1, LB, 128), lambda b, c: (b, c, 0)),
            pl.BlockSpec((1, 8, LB), lambda b, c: (b, 0, c)),
            pl.BlockSpec((1, H_A, DK_A, DK_A), lambda b, c: (b, 0, 0, 0)),
            pl.BlockSpec((1, H_A, DK_A), lambda b, c: (b, 0, 0)),
            pl.BlockSpec((1, 8, 128), lambda b, c: (b, 0, 0)),
            pl.BlockSpec((1, D_A), lambda b, c: (0, 0)),
        ],
        out_specs=[
            pl.BlockSpec((1, LB, D_A), lambda b, c: (b, c, 0)),
            pl.BlockSpec((1, H_A, DK_A, DK_A), lambda b, c: (b, 0, 0, 0)),
            pl.BlockSpec((1, H_A, DK_A), lambda b, c: (b, 0, 0)),
            pl.BlockSpec((1, 8, 128), lambda b, c: (b, 0, 0)),
        ],
        out_shape=[
            jax.ShapeDtypeStruct((B, T, D_A), BF16),
            jax.ShapeDtypeStruct((B, H_A, DK_A, DK_A), F32),
            jax.ShapeDtypeStruct((B, H_A, DK_A), F32),
            jax.ShapeDtypeStruct((B, 8, 128), F32),
        ],
        scratch_shapes=[
            pltpu.VMEM((H_A, DK_A, DK_A), F32),
            pltpu.VMEM((H_A, DK_A), F32),
            pltpu.VMEM((8, 128), F32),
        ],
        compiler_params=_cparams(("arbitrary", "arbitrary")),
        name="mlstm",
    )(qkv, ogu, gc, gr, c0, n0, m0, gain)


def _pool_out_kernel(hh_ref, u_ref, prev_ref, wp_ref, ps_ref, wo_ref, x_ref, o_ref, e_s, *, tm, nt, pos0):
    t = pl.program_id(1)
    H = POOL_HDR

    @pl.when(t == 0)
    def _():
        e_s[0:H, :] = prev_ref[0]

    if nt > 1:
        @pl.when(t > 0)
        def _():
            e_s[0:H, :] = e_s[tm:tm + H, :]

    e_s[H:H + tm, :] = u_ref[0]
    pos = pos0 + t * tm + lax.broadcasted_iota(jnp.int32, (tm, 1), 0)
    ys = []
    for g, w in enumerate(POOL_WINDOWS):
        sl = slice(g * G_B, (g + 1) * G_B)
        cur = e_s[H:H + tm, sl]
        win = cur
        for j in range(1, w):
            win = win + e_s[H - j:H - j + tm, sl]
        cnt = jnp.minimum(pos + 1, w).astype(F32)
        pooled = win / cnt - cur
        ys.append(_dot(pooled.astype(BF16), wp_ref[g]))
    yb = jnp.concatenate(ys, axis=-1) * ps_ref[...]
    o_ref[0] = x_ref[0] + _dot(hh_ref[0], wo_ref[0:D_A, :]) + _dot(yb.astype(BF16), wo_ref[D_A:, :])


def pool_out(hh, ogu, prev16, w_pool, pool_scale, w_out, x, tm, pos0):
    B, T = x.shape[:2]
    nt = T // tm
    return pl.pallas_call(
        functools.partial(_pool_out_kernel, tm=tm, nt=nt, pos0=pos0),
        grid=(B, nt),
        in_specs=[
            pl.BlockSpec((1, tm, D_A), lambda b, t: (b, t, 0)),
            pl.BlockSpec((1, tm, D_B), lambda b, t: (b, t, 1)),
            pl.BlockSpec((1, POOL_HDR, D_B), lambda b, t: (b, 0, 0)),
            _const_spec((len(POOL_WINDOWS), G_B, G_B)),
            _const_spec((1, D_B)),
            _const_spec((D_A + D_B, D_MODEL)),
            pl.BlockSpec((1, tm, D_MODEL), lambda b, t: (b, t, 0)),
        ],
        out_specs=pl.BlockSpec((1, tm, D_MODEL), lambda b, t: (b, t, 0)),
        out_shape=jax.ShapeDtypeStruct((B, T, D_MODEL), F32),
        scratch_shapes=[pltpu.VMEM((POOL_HDR + tm, D_B), F32)],
        compiler_params=_cparams(("arbitrary", "arbitrary")),
        name="pool_out",
    )(hh, ogu, prev16, w_pool, pool_scale, w_out, x)


def _gelu_tanh(y):
    return 0.5 * y * (1.0 + jnp.tanh(math.sqrt(2.0 / math.pi) * (y + 0.044715 * (y * y * y))))


FFN_CHUNKS = (1024, 1024, 768)
assert sum(FFN_CHUNKS) == D_FF


def _ffn_kernel(x_ref, g_ref, wup_ref, cw_ref, cb_ref, wdn_ref, p1_ref, p2_ref, gf_ref, *refs,
                tm, seq_len, carried, final_norm, mixer_proj):
    if mixer_proj:
        mix_ref, wmix_ref, o_ref, st_ref, carry_s, a_s, g_s = refs
    else:
        o_ref, st_ref, carry_s, a_s, g_s = refs
    i = pl.program_id(0)
    offs = [sum(FFN_CHUNKS[:c]) for c in range(len(FFN_CHUNKS) + 1)]
    nch = len(FFN_CHUNKS)
    x = x_ref[...]
    if mixer_proj:
        x = x + _dot(mix_ref[...], wmix_ref[...])
    h = _rms(x, g_ref[...]).astype(BF16)
    t = lax.broadcasted_iota(jnp.int32, (tm, 1), 0) % seq_len

    if carried:
        @pl.when(i == 0)
        def _():
            carry_s[...] = jnp.zeros_like(carry_s)

    def up(c):
        w = FFN_CHUNKS[c]
        a_s[c % 2, :, 0:w] = _dot(h, wup_ref[:, offs[c]:offs[c + 1]])
        g_s[c % 2, :, 0:w] = _dot(h, wup_ref[:, D_FF + offs[c]:D_FF + offs[c + 1]])

    def act_down(c):
        cols = slice(offs[c], offs[c + 1])
        w = FFN_CHUNKS[c]
        a = a_s[c % 2, :, 0:w]
        s1 = jnp.where(t >= 1, pltpu.roll(a, 1, 0), 0.0)
        s2 = jnp.where(t >= 2, pltpu.roll(a, 2, 0), 0.0)
        if carried:
            prev0 = carry_s[6:7, cols]
            prev1 = carry_s[7:8, cols]
            s1 = s1 + jnp.where(t == 0, prev1, 0.0)
            s2 = s2 + jnp.where(t == 0, prev0, 0.0) + jnp.where(t == 1, prev1, 0.0)
            carry_s[:, cols] = a[tm - 8:tm, :]
            st_ref[:, cols] = a[tm - 8:tm, :]
        else:
            s1 = s1 + p1_ref[:, cols]
            s2 = s2 + p2_ref[:, cols]
            st_ref[:, cols] = a
        y = cb_ref[:, cols] + cw_ref[0:1, cols] * s2 + cw_ref[1:2, cols] * s1 + cw_ref[2:3, cols] * a
        act = (_gelu_tanh(y) * g_s[c % 2, :, 0:w]).astype(BF16)
        return _dot(act, wdn_ref[cols, :])

    up(0)
    for c in range(nch):
        if c + 1 < nch:
            up(c + 1)
        d = act_down(c)
        if c == 0:
            o_ref[...] = x + d
        else:
            o_ref[...] += d
    if final_norm:
        o_ref[...] = _rms(o_ref[...], gf_ref[...])


def ffn(x, g, w_up, conv_w, conv_b, w_down, p1, p2, g_final, tm, seq_len, carried, final_norm, mix=None, w_mix=None):
    M = x.shape[0]
    st_rows = 8 if carried else tm
    st_total = 8 if carried else M
    row_spec = lambda n: pl.BlockSpec((tm, n), lambda i: (i, 0))
    p_spec = _const_spec((8, D_FF)) if carried else row_spec(D_FF)
    mixer_proj = mix is not None
    mix_specs = [row_spec(D_MODEL), _const_spec((D_MODEL, D_MODEL))] if mixer_proj else []
    mix_args = [mix, w_mix] if mixer_proj else []
    return pl.pallas_call(
        functools.partial(_ffn_kernel, tm=tm, seq_len=seq_len, carried=carried, final_norm=final_norm,
                          mixer_proj=mixer_proj),
        grid=(M // tm,),
        in_specs=[
            row_spec(D_MODEL),
            _const_spec((1, D_MODEL)),
            _const_spec((D_MODEL, 2 * D_FF)),
            _const_spec((CONV_W, D_FF)),
            _const_spec((1, D_FF)),
            _const_spec((D_FF, D_MODEL)),
            p_spec,
            p_spec,
            _const_spec((1, D_MODEL)),
            *mix_specs,
        ],
        out_specs=[
            row_spec(D_MODEL),
            pl.BlockSpec((st_rows, D_FF), (lambda i: (0, 0)) if carried else (lambda i: (i, 0))),
        ],
        out_shape=[
            jax.ShapeDtypeStruct((M, D_MODEL), F32),
            jax.ShapeDtypeStruct((st_total, D_FF), F32),
        ],
        scratch_shapes=[
            pltpu.VMEM((8, D_FF), F32),
            pltpu.VMEM((2, tm, max(FFN_CHUNKS)), F32),
            pltpu.VMEM((2, tm, max(FFN_CHUNKS)), F32),
        ],
        compiler_params=_cparams(("arbitrary",)),
        name="ffn",
    )(x, g, w_up, conv_w, conv_b, w_down, p1, p2, g_final, *mix_args)


def _proj_odd_kernel(x_ref, g_ref, w_ref, q_ref, kf_ref, vf_ref, kb_ref, vb_ref):
    h = _rms(x_ref[...], g_ref[...]).astype(BF16)
    q = _dot(h, w_ref[:, 0:D_MODEL]) * SCORE_SCALE
    lane = lax.broadcasted_iota(jnp.int32, q.shape, 1) % DV_C
    q_ref[0] = jnp.where(lane < DC, q, 0.0).astype(BF16)
    q_ref[1] = jnp.where(lane >= DC, q, 0.0).astype(BF16)
    k = _dot(h, w_ref[:, D_MODEL:2 * D_MODEL])
    kf_ref[...] = k
    kb_ref[...] = k.astype(BF16)
    v = _dot(h, w_ref[:, 2 * D_MODEL:])
    vf_ref[...] = v
    vb_ref[...] = v.astype(BF16)


def proj_odd(x, g, w, tm):
    M = x.shape[0]
    row_spec = pl.BlockSpec((tm, D_MODEL), lambda i: (i, 0))
    return pl.pallas_call(
        _proj_odd_kernel,
        grid=(M // tm,),
        in_specs=[row_spec, _const_spec((1, D_MODEL)), _const_spec((D_MODEL, 3 * D_MODEL))],
        out_specs=[pl.BlockSpec((2, tm, D_MODEL), lambda i: (0, i, 0))] + [row_spec] * 4,
        out_shape=[
            jax.ShapeDtypeStruct((2, M, D_MODEL), BF16),
            jax.ShapeDtypeStruct((M, D_MODEL), F32),
            jax.ShapeDtypeStruct((M, D_MODEL), F32),
            jax.ShapeDtypeStruct((M, D_MODEL), BF16),
            jax.ShapeDtypeStruct((M, D_MODEL), BF16),
        ],
        compiler_params=_cparams(("arbitrary",)),
        name="proj_odd",
    )(x, g, w)


ATTN_STRIP = 256
ATTN_KEY_CHUNK = 256
ATTN_ONES_ROWS = 16


def _attn_prompt_kernel(it_ref, jt_ref, lam_ref, q_ref, k_ref, vt_ref, bias_ref, gain_ref, o_ref,
                        m_s, acc_s, s_s, *, tq, out_scale):
    i = it_ref[pl.program_id(1)]
    j = jt_ref[pl.program_id(1)]
    W = ATTN_STRIP
    KC = ATTN_KEY_CHUNK
    nstrip = tq // W

    @pl.when(j == 0)
    def _():
        m_s[...] = jnp.full_like(m_s, NEG_INF)
        acc_s[...] = jnp.zeros_like(acc_s)

    def tile(kind):
        strips = [(mp, rb) for mp in range(2) for rb in range(nstrip)]

        def nkeys(rb):
            return (rb + 1) * W if kind == 0 else tq

        def scores(idx):
            mp, rb = strips[idx]
            qs = q_ref[mp, rb * W:(rb + 1) * W, :]
            chunks = [_dot_nt(k_ref[c * KC:(c + 1) * KC, :], qs) for c in range(nkeys(rb) // KC)]

            def add_bias(key_block, b):
                for c in range(key_block * W // KC, (key_block + 1) * W // KC):
                    off = c * KC - key_block * W
                    chunks[c] = chunks[c] + b[off:off + KC, :]

            if kind == 0:
                add_bias(rb, bias_ref[0, 0])
                if rb >= 1:
                    add_bias(rb - 1, bias_ref[0, 1])
            elif kind == 1 and rb == 0:
                add_bias(nstrip - 1, bias_ref[0, 1])
            for c, s in enumerate(chunks):
                s_s[idx % 2, c * KC:(c + 1) * KC, :] = s

        def consume(idx):
            mp, rb = strips[idx]
            nchunks = nkeys(rb) // KC
            cols = slice(mp * tq + rb * W, mp * tq + (rb + 1) * W)
            chunks = [s_s[idx % 2, c * KC:(c + 1) * KC, :] for c in range(nchunks)]
            m_old = m_s[:, cols]
            m_new = m_old
            for s in chunks:
                m_new = jnp.maximum(m_new, jnp.max(s, axis=0, keepdims=True))
            alpha = jnp.exp2(m_old - m_new)
            ps = [jnp.exp2(s - m_new).astype(BF16) for s in chunks]
            pcat = jnp.concatenate(ps, axis=0) if nchunks > 1 else ps[0]
            acc_s[:, cols] = alpha * acc_s[:, cols] + _dot(vt_ref[:, 0:nchunks * KC], pcat)
            m_s[:, cols] = m_new

        scores(0)
        for idx in range(len(strips)):
            if idx + 1 < len(strips):
                scores(idx + 1)
            consume(idx)

    @pl.when(j < i - 1)
    def _():
        tile(2)

    @pl.when(j == i - 1)
    def _():
        tile(1)

    @pl.when(j == i)
    def _():
        tile(0)
        n = acc_s[0:DV_C, :] / acc_s[DV_C:DV_C + 1, :]
        o = n[:, 0:tq] - lam_ref[0] * n[:, tq:]
        o = o * lax.rsqrt(jnp.mean(o * o, axis=0, keepdims=True) + EPS) * gain_ref[...] * out_scale
        o_ref[...] = o.T.astype(BF16)


def attn_prompt(lam, q2, k, vt, bias, gain_col, tq, out_scale):
    T = k.shape[0]
    nq = T // tq
    pairs = [(i, j) for i in range(nq) for j in range(i + 1)]
    itab = jnp.asarray(np.array([p[0] for p in pairs], np.int32))
    jtab = jnp.asarray(np.array([p[1] for p in pairs], np.int32))
    grid_spec = pltpu.PrefetchScalarGridSpec(
        num_scalar_prefetch=2,
        grid=(H_C, len(pairs)),
        in_specs=[
            pl.BlockSpec(memory_space=pltpu.SMEM),
            pl.BlockSpec((2, tq, DV_C), lambda h, p, it, jt: (0, it[p], h)),
            pl.BlockSpec((tq, DV_C), lambda h, p, it, jt: (jt[p], h)),
            pl.BlockSpec((DV_C + ATTN_ONES_ROWS, tq), lambda h, p, it, jt: (h, jt[p])),
            pl.BlockSpec((1, 2, ATTN_STRIP, ATTN_STRIP), lambda h, p, it, jt: (h, 0, 0, 0)),
            pl.BlockSpec((DV_C, 1), lambda h, p, it, jt: (0, 0)),
        ],
        out_specs=pl.BlockSpec((tq, DV_C), lambda h, p, it, jt: (it[p], h)),
        scratch_shapes=[
            pltpu.VMEM((1, 2 * tq), F32),
            pltpu.VMEM((DV_C + ATTN_ONES_ROWS, 2 * tq), F32),
            pltpu.VMEM((2, tq, ATTN_STRIP), F32),
        ],
    )
    return pl.pallas_call(
        functools.partial(_attn_prompt_kernel, tq=tq, out_scale=out_scale),
        grid_spec=grid_spec,
        out_shape=jax.ShapeDtypeStruct((T, H_C * DV_C), BF16),
        compiler_params=_cparams(("arbitrary", "arbitrary")),
        name="attn_prompt",
    )(itab, jtab, lam, q2, k, vt, bias, gain_col)


PAGES_PER_STEP = 16
PAGE_GROUP = 2
ROWS_PER_HEAD = 16


def _attn_sample_kernel(pt_ref, lam_ref, q_ref, *refs, out_scale):
    P = PAGES_PER_STEP
    G = PAGE_GROUP
    R = ROWS_PER_HEAD
    ngroups = P // G
    k_refs = refs[0:P]
    v_refs = refs[P:2 * P]
    kn_ref, vn_ref, bias_last_ref, bias_new_ref, gain_ref, o_ref, m_s, l_s, acc_s, s_s = refs[2 * P:]
    j = pl.program_id(1)
    nj = pl.num_programs(1)

    @pl.when(j == 0)
    def _():
        m_s[...] = jnp.full_like(m_s, NEG_INF)
        l_s[...] = jnp.zeros_like(l_s)
        acc_s[...] = jnp.zeros_like(acc_s)

    def head_rows(page_refs, p, h):
        return page_refs[p][0, pl.ds(h, PAGE_SIZE, stride=H_C), :].astype(BF16)

    def scores(g):
        for h in range(H_C):
            kcat = jnp.concatenate([head_rows(k_refs, g * G + t, h) for t in range(G)], axis=0)
            s_s[g, h * R:(h + 1) * R, :] = _dot_nt(q_ref[0, h * R:(h + 1) * R, :], kcat)

    def update(s, v_of_head):
        m_old = m_s[...]
        m_new = jnp.maximum(m_old, jnp.max(s, axis=-1, keepdims=True))
        alpha = jnp.exp2(m_old - m_new)
        p = jnp.exp2(s - m_new).astype(BF16)
        l_s[...] = alpha * l_s[...] + jnp.sum(p.astype(F32), axis=-1, keepdims=True)
        for h in range(H_C):
            rows = slice(h * R, (h + 1) * R)
            acc_s[rows, :] = alpha[rows] * acc_s[rows, :] + _dot(p[rows, :], v_of_head(h))
        m_s[...] = m_new

    def v_group(g):
        return lambda h: jnp.concatenate([head_rows(v_refs, g * G + t, h) for t in range(G)], axis=0)

    @pl.when(j < nj - 1)
    def _():
        for g in range(ngroups):
            scores(g)
        for g in range(ngroups):
            update(s_s[g], v_group(g))

    @pl.when(j == nj - 1)
    def _():
        for g in range(ngroups):
            scores(g)
        for g in range(ngroups):
            s = s_s[g]
            if g == ngroups - 1:
                s = s + bias_last_ref[...]
            update(s, v_group(g))
        for h in range(H_C):
            kh = kn_ref[0, :, h * DV_C:(h + 1) * DV_C]
            s_s[0, h * R:(h + 1) * R, 0:PAGE_SIZE] = _dot_nt(q_ref[0, h * R:(h + 1) * R, :], kh)
        update(s_s[0, :, 0:PAGE_SIZE] + bias_new_ref[...], lambda h: vn_ref[0, :, h * DV_C:(h + 1) * DV_C])
        n = acc_s[...] / l_s[...]
        for h in range(H_C):
            o = n[h * R:h * R + 8, :] - lam_ref[0] * n[h * R + 8:(h + 1) * R, :]
            o_ref[0, h * 8:(h + 1) * 8, :] = _rms(o, gain_ref[...]) * out_scale


def attn_sample(page_table, lam, qm, cache_k, cache_v, k_new, v_new, bias_last, bias_new, gain, out_scale):
    B = qm.shape[0]
    P = PAGES_PER_STEP
    n_pages = page_table.shape[1]
    nj = n_pages // P
    rows = H_C * ROWS_PER_HEAD
    page_rows = PAGE_SIZE * H_C

    def page_spec(p):
        return pl.BlockSpec((1, page_rows, DV_C), lambda b, j, pt, p=p: (pt[b, j * P + p], 0, 0))

    grid_spec = pltpu.PrefetchScalarGridSpec(
        num_scalar_prefetch=1,
        grid=(B, nj),
        in_specs=[
            pl.BlockSpec(memory_space=pltpu.SMEM),
            pl.BlockSpec((1, rows, DV_C), lambda b, j, pt: (b, 0, 0)),
            *[page_spec(p) for p in range(P)],
            *[page_spec(p) for p in range(P)],
            pl.BlockSpec((1, PAGE_SIZE, H_C * DV_C), lambda b, j, pt: (b, 0, 0)),
            pl.BlockSpec((1, PAGE_SIZE, H_C * DV_C), lambda b, j, pt: (b, 0, 0)),
            pl.BlockSpec((rows, PAGE_GROUP * PAGE_SIZE), lambda b, j, pt: (0, 0)),
            pl.BlockSpec((rows, PAGE_SIZE), lambda b, j, pt: (0, 0)),
            pl.BlockSpec((1, DV_C), lambda b, j, pt: (0, 0)),
        ],
        out_specs=pl.BlockSpec((1, H_C * 8, DV_C), lambda b, j, pt: (b, 0, 0)),
        scratch_shapes=[
            pltpu.VMEM((rows, 1), F32),
            pltpu.VMEM((rows, 1), F32),
            pltpu.VMEM((rows, DV_C), F32),
            pltpu.VMEM((P // PAGE_GROUP, rows, PAGE_GROUP * PAGE_SIZE), F32),
        ],
    )
    return pl.pallas_call(
        functools.partial(_attn_sample_kernel, out_scale=out_scale),
        grid_spec=grid_spec,
        out_shape=jax.ShapeDtypeStruct((B, H_C * 8, DV_C), F32),
        compiler_params=_cparams(("arbitrary", "arbitrary")),
        name="attn_sample",
    )(page_table, lam, qm, *([cache_k] * P), *([cache_v] * P), k_new, v_new, bias_last, bias_new, gain)


def _t5_bucket_table():
    n = np.arange(MAX_DIST + 1)
    max_exact = N_BUCKETS // 2
    nf = np.maximum(n, 1).astype(np.float32)
    large = max_exact + (np.log(nf / max_exact) / math.log(MAX_DIST / max_exact) * (N_BUCKETS - max_exact)).astype(np.int32)
    large = np.minimum(large, N_BUCKETS - 1)
    return np.where(n < max_exact, n, large).astype(np.int32)


def _rel_bias_minus_far(rel_bias, rel):
    tab = _near_bias_table(rel_bias)
    vals = tab[np.clip(rel, 0, MAX_DIST)]
    vals = jnp.where(jnp.asarray(rel >= 0)[..., None], vals, NEG_INF)
    return jnp.moveaxis(vals, -1, 0).astype(F32)


def _near_bias_table(rel_bias):
    tab = rel_bias[_t5_bucket_table()]
    return ((tab - tab[MAX_DIST][None, :]) * LOG2E).astype(F32)


def _bias_blocks_t(rel_bias):
    W = ATTN_STRIP
    H = rel_bias.shape[1]
    f = jnp.concatenate([_near_bias_table(rel_bias), jnp.zeros((W - MAX_DIST - 1, H), F32)], axis=0).T
    g0 = jnp.concatenate([f, jnp.full((H, W), NEG_INF, F32)], axis=1)
    g1 = jnp.concatenate([jnp.zeros((H, W), F32), f], axis=1)
    g = jnp.stack([g0, g1], axis=1)
    rep = jnp.tile(g, (1, 1, W))[:, :, :W * (2 * W - 1)].reshape(H, 2, W, 2 * W - 1)
    return rep[:, :, :, :W]


TM_PROMPT = 512
TM_FFN = 512
SAMPLE_PAD = 16
TQ = 2048


def kernel(x_prompt, x_sample, state_mlstm_C, state_mlstm_n, state_mlstm_m, state_pool, cache_k, cache_v, state_ffn_conv, page_table, norm_mix, norm_ffn, norm_final, w_in_e, b_gate_e, mlstm_gain, w_pool, pool_scale, w_out_e, w_in_o, lambda_q1, lambda_k1, lambda_q2, lambda_k2, subln_gain, rel_bias, w_out_o, w_up, conv_w, conv_b, w_down):
    Bp, Tp = x_prompt.shape[:2]
    Bs, Ts = x_sample.shape[:2]
    assert Bp == 1
    Ms = Bs * Ts
    xp = x_prompt.reshape(Tp, D_MODEL)
    xs = x_sample.reshape(Ms, D_MODEL)
    row = lambda a: a.reshape(1, -1)

    w_in = w_in_e[0]
    n_gate = 2 * H_A
    w_main = jnp.concatenate([w_in[:, :4 * D_A], w_in[:, 4 * D_A + n_gate:], w_in[:, 4 * D_A:4 * D_A + n_gate],
                              jnp.zeros((D_MODEL, 128 - n_gate), F32)], axis=1).astype(BF16)
    b_col = jnp.pad(b_gate_e[0], (0, 128 - n_gate)).reshape(1, 128)
    g_mix0 = row(norm_mix[0])
    wp_b = w_pool[0].astype(BF16)
    wo_e = w_out_e[0].astype(BF16)
    gain_e = row(mlstm_gain[0])
    ps_e = row(pool_scale[0])

    qkv_p, ogu_p, gc_p, gr_p = proj_even(xp, g_mix0, w_main, b_col, TM_PROMPT)
    zc = jnp.zeros((1, H_A, DK_A, DK_A), F32)
    zn = jnp.zeros((1, H_A, DK_A), F32)
    zm = jnp.zeros((1, 8, 128), F32)
    hh_p, C_p, n_p, m_p = mlstm(qkv_p[None], ogu_p[None], gc_p[None], gr_p[None], zc, zn, zm, gain_e,
                                MLSTM_CHUNK, MLSTM_CHUNK, MLSTM_CHUNKS_PER_STEP)
    xp = pool_out(hh_p, ogu_p[None], jnp.zeros((1, POOL_HDR, D_B), F32), wp_b, ps_e, wo_e, xp[None], TM_PROMPT, 0)[0]
    pool_p = ogu_p[Tp - POOL_BUF:, D_A:][None]

    L = SAMPLE_PAD
    qkv_s, ogu_s, gc_s, gr_s = proj_even(xs, g_mix0, w_main, b_col, Ms)
    pad_t = lambda a, n: jnp.pad(a.reshape(Bs, Ts, a.shape[-1]), ((0, 0), (0, n - Ts), (0, 0)))
    gr_s3 = jnp.pad(gr_s.reshape(8, Bs, Ts).transpose(1, 0, 2), ((0, 0), (0, 0), (0, L - Ts)))
    m0_s = jnp.broadcast_to(jnp.pad(state_mlstm_m[0], ((0, 0), (0, 8 - H_A)))[:, :, None], (Bs, 8, 128))
    ogu_s3 = pad_t(ogu_s, L)
    hh_s, C_s, n_s, m_s = mlstm(pad_t(qkv_s, L), ogu_s3, pad_t(gc_s, L), gr_s3,
                                state_mlstm_C[0], state_mlstm_n[0], m0_s, gain_e, L, Ts, 1)
    prev16 = jnp.pad(state_pool[0], ((0, 0), (POOL_HDR - POOL_BUF, 0), (0, 0)))
    xs = pool_out(hh_s, ogu_s3, prev16, wp_b, ps_e, wo_e, pad_t(xs, L), L, PAST_LEN)[:, :Ts].reshape(Ms, D_MODEL)
    pool_s = jnp.concatenate([state_pool[0], ogu_s[:, D_A:].reshape(Bs, Ts, D_B)], axis=1)[:, -POOL_BUF:]

    def run_ffn(l, xp, xs, final_norm, mix_p=None, mix_s=None, w_mix=None):
        g = row(norm_ffn[l])
        wu = w_up[l].astype(BF16)
        wd = w_down[l].astype(BF16)
        cb = row(conv_b[l])
        gf = row(norm_final)
        zp = jnp.zeros((8, D_FF), F32)
        xp, st_p = ffn(xp, g, wu, conv_w[l], cb, wd, zp, zp, gf, TM_FFN, TM_FFN, True, final_norm, mix_p, w_mix)
        st = state_ffn_conv[l]
        z1 = jnp.zeros((Bs, 1, D_FF), F32)
        p1 = jnp.concatenate([st[:, 1:2], z1, z1, z1], axis=1).reshape(Ms, D_FF)
        p2 = jnp.concatenate([st[:, 0:1], st[:, 1:2], z1, z1], axis=1).reshape(Ms, D_FF)
        xs, a_s = ffn(xs, g, wu, conv_w[l], cb, wd, p1, p2, gf, Ms, Ts, False, final_norm, mix_s, w_mix)
        conv_p = st_p[8 - (CONV_W - 1):][None]
        conv_s = a_s.reshape(Bs, Ts, D_FF)[:, Ts - (CONV_W - 1):]
        return xp, xs, conv_p, conv_s

    xp, xs, conv_p0, conv_s0 = run_ffn(0, xp, xs, False)

    lam_init = 0.8 - 0.6 * math.exp(-0.3 * 1)
    lam = (jnp.exp(jnp.sum(lambda_q1[0] * lambda_k1[0])) - jnp.exp(jnp.sum(lambda_q2[0] * lambda_k2[0])) + lam_init).astype(F32).reshape(1)
    out_scale = 1.0 - lam_init
    g_mix1 = row(norm_mix[1])
    w_qkv = w_in_o[0].astype(BF16)
    wo_o = w_out_o[0].astype(BF16)
    gain_o = row(subln_gain[0])

    q2_p, kf_p, vf_p, kb_p, vb_p = proj_odd(xp, g_mix1, w_qkv, TM_PROMPT)
    vt_p = jnp.concatenate([vb_p.T.reshape(H_C, DV_C, Tp), jnp.ones((H_C, ATTN_ONES_ROWS, Tp), BF16)], axis=1)
    vt_p = vt_p.reshape(H_C * (DV_C + ATTN_ONES_ROWS), Tp)
    o_p = attn_prompt(lam, q2_p, kb_p, vt_p, _bias_blocks_t(rel_bias), gain_o.reshape(DV_C, 1), TQ, out_scale)

    q2_s, kf_s, vf_s, kb_s, vb_s = proj_odd(xs, g_mix1, w_qkv, Ms)
    qm = q2_s.reshape(2, Bs, Ts, H_C, DV_C).transpose(1, 3, 0, 2, 4)
    qm = jnp.pad(qm, ((0, 0), (0, 0), (0, 0), (0, 8 - Ts), (0, 0)))
    qm = qm.reshape(Bs, H_C * ROWS_PER_HEAD, DV_C)
    tok = np.minimum(np.arange(8), Ts - 1)
    tok = np.tile(tok, 2)
    ccol = np.arange(PAGE_SIZE)
    rel_last = PAGE_SIZE + tok[:, None] - ccol[None, :]
    rel_new = np.where(ccol[None, :] < Ts, tok[:, None] - ccol[None, :], -1)
    bias_last = _rel_bias_minus_far(rel_bias, rel_last).reshape(H_C * ROWS_PER_HEAD, PAGE_SIZE)
    bias_last = jnp.pad(bias_last, ((0, 0), ((PAGE_GROUP - 1) * PAGE_SIZE, 0)))
    bias_new = _rel_bias_minus_far(rel_bias, rel_new).reshape(H_C * ROWS_PER_HEAD, PAGE_SIZE)
    n_phys = cache_k.shape[1]
    ck = cache_k[0].reshape(n_phys, PAGE_SIZE * H_C, DV_C)
    cv = cache_v[0].reshape(n_phys, PAGE_SIZE * H_C, DV_C)
    kn = jnp.pad(kb_s.reshape(Bs, Ts, D_MODEL), ((0, 0), (0, PAGE_SIZE - Ts), (0, 0)))
    vn = jnp.pad(vb_s.reshape(Bs, Ts, D_MODEL), ((0, 0), (0, PAGE_SIZE - Ts), (0, 0)))
    o_s = attn_sample(page_table, lam, qm, ck, cv, kn, vn, bias_last, bias_new, gain_o, out_scale)
    o_s = o_s.reshape(Bs, H_C, 8, DV_C)[:, :, :Ts].transpose(0, 2, 1, 3).reshape(Ms, D_MODEL).astype(BF16)

    yp, ys, conv_p1, conv_s1 = run_ffn(1, xp, xs, True, o_p, o_s, wo_o)

    y_prompt = yp.reshape(Bp, Tp, D_MODEL)
    y_sample = ys.reshape(Bs, Ts, D_MODEL)
    new_m_p = m_p[:, :H_A, 0]
    new_m_s = m_s[:, :H_A, 0]
    new_k_p = kf_p.reshape(1, Bp, Tp, H_C, DV_C)
    new_v_p = vf_p.reshape(1, Bp, Tp, H_C, DV_C)
    new_k_s = kf_s.reshape(1, Bs, Ts, H_C, DV_C)
    new_v_s = vf_s.reshape(1, Bs, Ts, H_C, DV_C)
    return (y_prompt, y_sample,
            C_p[None], n_p[None], new_m_p[None], pool_p[None], new_k_p, new_v_p,
            jnp.stack([conv_p0, conv_p1]),
            C_s[None], n_s[None], new_m_s[None], pool_s[None], new_k_s, new_v_s,
            jnp.stack([conv_s0, conv_s1]))
```

```python
import functools
import math

import numpy as np
import jax
import jax.numpy as jnp
from jax import lax
from jax.experimental import pallas as pl
from jax.experimental.pallas import tpu as pltpu

F32 = jnp.float32
BF16 = jnp.bfloat16
HIGHEST = lax.Precision.HIGHEST

D_MODEL = 1024
PAST_LEN = 16384
PAGE_SIZE = 128
D_A = 512
H_A = 4
DK_A = 128
MLSTM_CHUNK = 128
MLSTM_CHUNKS_PER_STEP = 2
D_B = 512
POOL_WINDOWS = (2, 4, 8, 16)
G_B = 128
POOL_BUF = 15
POOL_HDR = 16
H_C = 8
DC = 64
DV_C = 128
N_BUCKETS = 32
MAX_DIST = 128
LOG2E = math.log2(math.e)
SCORE_SCALE = DC ** -0.5 * LOG2E
D_FF = 2816
CONV_W = 3
EPS = 1e-6

VMEM_LIMIT = 56 * 1024 * 1024
NEG_INF = float("-inf")


def _cparams(sem):
    return pltpu.CompilerParams(dimension_semantics=sem, vmem_limit_bytes=VMEM_LIMIT)


def _const_spec(shape):
    nd = len(shape)
    return pl.BlockSpec(shape, lambda *_: (0,) * nd, pipeline_mode=pl.Buffered(1))


def _rms(x, g):
    return x * lax.rsqrt(jnp.mean(x * x, axis=-1, keepdims=True) + EPS) * g


def _dot(a, b):
    return jnp.dot(a, b, preferred_element_type=F32)


def _dot_nt(a, b):
    return lax.dot_general(a, b, (((1,), (1,)), ((), ())), preferred_element_type=F32)


def _dot_tn(a, b):
    return lax.dot_general(a, b, (((0,), (0,)), ((), ())), preferred_element_type=F32)


def _log_sigmoid(x):
    return jnp.minimum(x, 0.0) - jnp.log1p(jnp.exp(-jnp.abs(x)))


def _sigmoid(x):
    return 1.0 / (1.0 + jnp.exp(-x))


def _proj_even_kernel(x_ref, g_ref, w_ref, bc_ref, qkv_ref, ogu_ref, gc_ref, gr_ref):
    h = _rms(x_ref[...], g_ref[...])
    z = _dot(h.astype(BF16), w_ref[...])
    qkv_ref[:, 0:D_A] = z[:, 0:D_A].astype(BF16)
    qkv_ref[:, D_A:2 * D_A] = (z[:, D_A:2 * D_A] * (DK_A ** -0.5)).astype(BF16)
    qkv_ref[:, 2 * D_A:3 * D_A] = z[:, 2 * D_A:3 * D_A].astype(BF16)
    ogu_ref[...] = z[:, 3 * D_A:3 * D_A + D_A + D_B]
    gc = z[:, 3 * D_A + D_A + D_B:] + bc_ref[...]
    gc_ref[...] = gc
    gr_ref[...] = gc.T[0:8, :]


def proj_even(x, g, w_main, b_col, tm):
    M = x.shape[0]
    n_main = w_main.shape[1]
    return pl.pallas_call(
        _proj_even_kernel,
        grid=(M // tm,),
        in_specs=[
            pl.BlockSpec((tm, D_MODEL), lambda i: (i, 0)),
            _const_spec((1, D_MODEL)),
            _const_spec((D_MODEL, n_main)),
            _const_spec((1, 128)),
        ],
        out_specs=[
            pl.BlockSpec((tm, 3 * D_A), lambda i: (i, 0)),
            pl.BlockSpec((tm, D_A + D_B), lambda i: (i, 0)),
            pl.BlockSpec((tm, 128), lambda i: (i, 0)),
            pl.BlockSpec((8, tm), lambda i: (0, i)),
        ],
        out_shape=[
            jax.ShapeDtypeStruct((M, 3 * D_A), BF16),
            jax.ShapeDtypeStruct((M, D_A + D_B), F32),
            jax.ShapeDtypeStruct((M, 128), F32),
            jax.ShapeDtypeStruct((8, M), F32),
        ],
        compiler_params=_cparams(("arbitrary",)),
        name="proj_even",
    )(x, g, w_main, b_col)


def _mlstm_kernel(qkv_ref, og_ref, gc_ref, gr_ref, c0_ref, n0_ref, m0_ref, gain_ref,
                  hh_ref, c_out_ref, n_out_ref, m_out_ref, c_s, n_s, m_s, *, L, valid, chunks):
    c = pl.program_id(1)

    @pl.when(c == 0)
    def _():
        c_s[...] = c0_ref[0]
        n_s[...] = n0_ref[0]
        m_s[...] = m0_ref[0]

    row = lax.broadcasted_iota(jnp.int32, (L, L), 0)
    col = lax.broadcasted_iota(jnp.int32, (L, L), 1)
    tri = (col <= row).astype(F32)
    mask = (col <= row) & (col < valid)
    rvalid = lax.broadcasted_iota(jnp.int32, (L, 1), 0) < valid
    cvalid = lax.broadcasted_iota(jnp.int32, (1, L), 1) < valid

    c_st = [c_s[h] for h in range(H_A)]
    n_st = [n_s[h:h + 1, :] for h in range(H_A)]
    m_st = [m_s[h:h + 1, 0:1] for h in range(H_A)]

    for ci in range(chunks):
        rows = slice(ci * L, (ci + 1) * L)
        gcol = gc_ref[0, rows, :]
        grow = gr_ref[0, :, rows]
        lf_col = jnp.where(rvalid, _log_sigmoid(gcol), 0.0)
        lf_row = jnp.where(cvalid, _log_sigmoid(grow), 0.0)
        b_col_all = jnp.dot(tri, lf_col, precision=HIGHEST, preferred_element_type=F32)
        b_row_all = lax.dot_general(lf_row, tri, (((1,), (1,)), ((), ())), precision=HIGHEST,
                                    preferred_element_type=F32)
        for h in range(H_A):
            bc = b_col_all[:, H_A + h:H_A + h + 1]
            br = b_row_all[H_A + h:H_A + h + 1, :]
            igc = gcol[:, h:h + 1]
            igr = grow[h:h + 1, :]
            m0 = m_st[h]
            logd = jnp.where(mask, bc - br + igr, NEG_INF)
            log_inter = bc + m0
            m_t = jnp.maximum(log_inter, jnp.max(logd, axis=-1, keepdims=True))
            dm = jnp.exp(logd - m_t)
            w_inter = jnp.exp(log_inter - m_t)
            q = qkv_ref[0, rows, h * DK_A:(h + 1) * DK_A]
            k = qkv_ref[0, rows, D_A + h * DK_A:D_A + (h + 1) * DK_A]
            v = qkv_ref[0, rows, 2 * D_A + h * DK_A:2 * D_A + (h + 1) * DK_A]
            s = _dot_nt(q, k) * dm
            c_old = c_st[h]
            n_old = n_st[h]
            num = _dot(s.astype(BF16), v) + w_inter * _dot_nt(q, c_old.astype(BF16))
            qn = jnp.sum(q.astype(F32) * n_old, axis=-1, keepdims=True)
            ndot = jnp.sum(s, axis=-1, keepdims=True) + w_inter * qn
            denom = jnp.maximum(jnp.abs(ndot), jnp.exp(-m_t))
            hh = num / denom
            y = _rms(hh, gain_ref[:, h * DK_A:(h + 1) * DK_A]) * _sigmoid(og_ref[0, rows, h * DK_A:(h + 1) * DK_A])
            hh_ref[0, rows, h * DK_A:(h + 1) * DK_A] = y.astype(BF16)
            m_new = m_t[valid - 1:valid, :]
            b_last = bc[valid - 1:valid, :]
            w_s = jnp.where(rvalid, jnp.exp(b_last - bc + igc - m_new), 0.0)
            decay = jnp.exp(b_last + m0 - m_new)
            kf = k.astype(F32)
            vw = (v.astype(F32) * w_s).astype(BF16)
            c_st[h] = decay * c_old + _dot_tn(vw, k)
            n_st[h] = decay * n_old + jnp.sum(kf * w_s, axis=0, keepdims=True)
            m_st[h] = m_new

    for h in range(H_A):
        c_s[h] = c_st[h]
        n_s[h:h + 1, :] = n_st[h]
        m_s[h:h + 1, :] = jnp.broadcast_to(m_st[h], (1, 128))

    @pl.when(c == pl.num_programs(1) - 1)
    def _():
        c_out_ref[0] = c_s[...]
        n_out_ref[0] = n_s[...]
        m_out_ref[0] = m_s[...]


def mlstm(qkv, ogu, gc, gr, c0, n0, m0, gain, L, valid, chunks):
    B, T = qkv.shape[:2]
    LB = L * chunks
    nc = T // LB
    return pl.pallas_call(
        functools.partial(_mlstm_kernel, L=L, valid=valid, chunks=chunks),
        grid=(B, nc),
        in_specs=[
            pl.BlockSpec((1, LB, 3 * D_A), lambda b, c: (b, c, 0)),
            pl.BlockSpec((1, LB, D_A), lambda b, c: (b, c, 0)),
            pl.BlockSpec((1, LB, 128), lambda b, c: (b, c, 0)),
            pl.BlockSpec((1, 8, LB), lambda b, c: (b, 0, c)),
            pl.BlockSpec((1, H_A, DK_A, DK_A), lambda b, c: (b, 0, 0, 0)),
            pl.BlockSpec((1, H_A, DK_A), lambda b, c: (b, 0, 0)),
            pl.BlockSpec((1, 8, 128), lambda b, c: (b, 0, 0)),
            pl.BlockSpec((1, D_A), lambda b, c: (0, 0)),
        ],
        out_specs=[
            pl.BlockSpec((1, LB, D_A), lambda b, c: (b, c, 0)),
            pl.BlockSpec((1, H_A, DK_A, DK_A), lambda b, c: (b, 0, 0, 0)),
            pl.BlockSpec((1, H_A, DK_A), lambda b, c: (b, 0, 0)),
            pl.BlockSpec((1, 8, 128), lambda b, c: (b, 0, 0)),
        ],
        out_shape=[
            jax.ShapeDtypeStruct((B, T, D_A), BF16),
            jax.ShapeDtypeStruct((B, H_A, DK_A, DK_A), F32),
            jax.ShapeDtypeStruct((B, H_A, DK_A), F32),
            jax.ShapeDtypeStruct((B, 8, 128), F32),
        ],
        scratch_shapes=[
            pltpu.VMEM((H_A, DK_A, DK_A), F32),
            pltpu.VMEM((H_A, DK_A), F32),
            pltpu.VMEM((8, 128), F32),
        ],
        compiler_params=_cparams(("arbitrary", "arbitrary")),
        name="mlstm",
    )(qkv, ogu, gc, gr, c0, n0, m0, gain)


def _even_mixer_out(hh, u, prev, wp_ref, ps_ref, wo_ref, e_s, t, *, tm, nt, pos0):
    H = POOL_HDR

    @pl.when(t == 0)
    def _():
        e_s[0:H, :] = prev

    if nt > 1:
        @pl.when(t > 0)
        def _():
            e_s[0:H, :] = e_s[tm:tm + H, :]

    e_s[H:H + tm, :] = u
    pos = pos0 + t * tm + lax.broadcasted_iota(jnp.int32, (tm, 1), 0)
    ys = []
    for g, w in enumerate(POOL_WINDOWS):
        sl = slice(g * G_B, (g + 1) * G_B)
        cur = e_s[H:H + tm, sl]
        win = cur
        for j in range(1, w):
            win = win + e_s[H - j:H - j + tm, sl]
        cnt = jnp.minimum(pos + 1, w).astype(F32)
        pooled = win / cnt - cur
        ys.append(_dot(pooled.astype(BF16), wp_ref[g]))
    yb = jnp.concatenate(ys, axis=-1) * ps_ref[...]
    return _dot(hh, wo_ref[0:D_A, :]) + _dot(yb.astype(BF16), wo_ref[D_A:, :])


def _pool_out_kernel(hh_ref, u_ref, prev_ref, wp_ref, ps_ref, wo_ref, x_ref, o_ref, e_s, *, tm, nt, pos0):
    y = _even_mixer_out(hh_ref[0], u_ref[0], prev_ref[0], wp_ref, ps_ref, wo_ref, e_s, pl.program_id(1),
                        tm=tm, nt=nt, pos0=pos0)
    o_ref[0] = x_ref[0] + y


def pool_out(hh, ogu, prev16, w_pool, pool_scale, w_out, x, tm, pos0):
    B, T = x.shape[:2]
    nt = T // tm
    return pl.pallas_call(
        functools.partial(_pool_out_kernel, tm=tm, nt=nt, pos0=pos0),
        grid=(B, nt),
        in_specs=[
            pl.BlockSpec((1, tm, D_A), lambda b, t: (b, t, 0)),
            pl.BlockSpec((1, tm, D_B), lambda b, t: (b, t, 1)),
            pl.BlockSpec((1, POOL_HDR, D_B), lambda b, t: (b, 0, 0)),
            _const_spec((len(POOL_WINDOWS), G_B, G_B)),
            _const_spec((1, D_B)),
            _const_spec((D_A + D_B, D_MODEL)),
            pl.BlockSpec((1, tm, D_MODEL), lambda b, t: (b, t, 0)),
        ],
        out_specs=pl.BlockSpec((1, tm, D_MODEL), lambda b, t: (b, t, 0)),
        out_shape=jax.ShapeDtypeStruct((B, T, D_MODEL), F32),
        scratch_shapes=[pltpu.VMEM((POOL_HDR + tm, D_B), F32)],
        compiler_params=_cparams(("arbitrary", "arbitrary")),
        name="pool_out",
    )(hh, ogu, prev16, w_pool, pool_scale, w_out, x)


def _gelu_tanh(y):
    return 0.5 * y * (1.0 + jnp.tanh(math.sqrt(2.0 / math.pi) * (y + 0.044715 * (y * y * y))))


FFN_CHUNKS = (1024, 1024, 768)
assert sum(FFN_CHUNKS) == D_FF


def _ffn_kernel(x_ref, g_ref, wup_ref, cw_ref, cb_ref, wdn_ref, p1_ref, p2_ref, gf_ref, *refs,
                tm, nt, seq_len, carried, final_norm, mixer):
    i = pl.program_id(0)
    x = x_ref[...]
    if mixer == "dense":
        mix_ref, wmix_ref, o_ref, st_ref, carry_s, a_s, g_s = refs
        x = x + _dot(mix_ref[...], wmix_ref[...])
    elif mixer == "pool":
        hh_ref, u_ref, prev_ref, wp_ref, ps_ref, wmix_ref, o_ref, st_ref, carry_s, a_s, g_s, e_s = refs
        x = x + _even_mixer_out(hh_ref[...], u_ref[...], prev_ref[...], wp_ref, ps_ref, wmix_ref, e_s, i,
                                tm=tm, nt=nt, pos0=0)
    else:
        o_ref, st_ref, carry_s, a_s, g_s = refs
    offs = [sum(FFN_CHUNKS[:c]) for c in range(len(FFN_CHUNKS) + 1)]
    nch = len(FFN_CHUNKS)
    h = _rms(x, g_ref[...]).astype(BF16)
    t = lax.broadcasted_iota(jnp.int32, (tm, 1), 0) % seq_len

    if carried:
        @pl.when(i == 0)
        def _():
            carry_s[...] = jnp.zeros_like(carry_s)

    def up(c):
        w = FFN_CHUNKS[c]
        a_s[c % 2, :, 0:w] = _dot(h, wup_ref[:, offs[c]:offs[c + 1]])
        g_s[c % 2, :, 0:w] = _dot(h, wup_ref[:, D_FF + offs[c]:D_FF + offs[c + 1]])

    def act_down(c):
        cols = slice(offs[c], offs[c + 1])
        w = FFN_CHUNKS[c]
        a = a_s[c % 2, :, 0:w]
        s1 = jnp.where(t >= 1, pltpu.roll(a, 1, 0), 0.0)
        s2 = jnp.where(t >= 2, pltpu.roll(a, 2, 0), 0.0)
        if carried:
            prev0 = carry_s[6:7, cols]
            prev1 = carry_s[7:8, cols]
            s1 = s1 + jnp.where(t == 0, prev1, 0.0)
            s2 = s2 + jnp.where(t == 0, prev0, 0.0) + jnp.where(t == 1, prev1, 0.0)
            carry_s[:, cols] = a[tm - 8:tm, :]
            st_ref[:, cols] = a[tm - 8:tm, :]
        else:
            s1 = s1 + p1_ref[:, cols]
            s2 = s2 + p2_ref[:, cols]
            st_ref[:, cols] = a
        y = cb_ref[:, cols] + cw_ref[0:1, cols] * s2 + cw_ref[1:2, cols] * s1 + cw_ref[2:3, cols] * a
        act = (_gelu_tanh(y) * g_s[c % 2, :, 0:w]).astype(BF16)
        return _dot(act, wdn_ref[cols, :])

    up(0)
    for c in range(nch):
        if c + 1 < nch:
            up(c + 1)
        d = act_down(c)
        if c == 0:
            o_ref[...] = x + d
        else:
            o_ref[...] += d
    if final_norm:
        o_ref[...] = _rms(o_ref[...], gf_ref[...])


def _layer_spec(shape, layer):
    nd = len(shape)
    return pl.BlockSpec((None,) + tuple(shape), lambda *_: (layer,) + (0,) * nd, pipeline_mode=pl.Buffered(1))


def ffn(x, g, w_up, conv_w, conv_b, w_down, p1, p2, g_final, tm, seq_len, carried, final_norm, layer,
        mix=None, w_mix=None, pool=None):
    M = x.shape[0]
    nt = M // tm
    st_rows = 8 if carried else tm
    st_total = 8 if carried else M
    row_spec = lambda n: pl.BlockSpec((tm, n), lambda i: (i, 0))
    p_spec = _const_spec((8, D_FF)) if carried else row_spec(D_FF)
    mixer, mix_specs, mix_args, mix_scratch = None, [], [], []
    if mix is not None:
        mixer = "dense"
        mix_specs = [row_spec(D_MODEL), _const_spec((D_MODEL, D_MODEL))]
        mix_args = [mix, w_mix]
    elif pool is not None:
        assert carried
        mixer = "pool"
        mix_specs = [
            row_spec(D_A),
            pl.BlockSpec((tm, D_B), lambda i: (i, 1)),
            _const_spec((POOL_HDR, D_B)),
            _const_spec((len(POOL_WINDOWS), G_B, G_B)),
            _const_spec((1, D_B)),
            _const_spec((D_A + D_B, D_MODEL)),
        ]
        mix_args = list(pool)
        mix_scratch = [pltpu.VMEM((POOL_HDR + tm, D_B), F32)]
    return pl.pallas_call(
        functools.partial(_ffn_kernel, tm=tm, nt=nt, seq_len=seq_len, carried=carried, final_norm=final_norm,
                          mixer=mixer),
        grid=(nt,),
        in_specs=[
            row_spec(D_MODEL),
            _const_spec((1, D_MODEL)),
            _layer_spec((D_MODEL, 2 * D_FF), layer),
            _const_spec((CONV_W, D_FF)),
            _const_spec((1, D_FF)),
            _layer_spec((D_FF, D_MODEL), layer),
            p_spec,
            p_spec,
            _const_spec((1, D_MODEL)),
            *mix_specs,
        ],
        out_specs=[
            row_spec(D_MODEL),
            pl.BlockSpec((st_rows, D_FF), (lambda i: (0, 0)) if carried else (lambda i: (i, 0))),
        ],
        out_shape=[
            jax.ShapeDtypeStruct((M, D_MODEL), F32),
            jax.ShapeDtypeStruct((st_total, D_FF), F32),
        ],
        scratch_shapes=[
            pltpu.VMEM((8, D_FF), F32),
            pltpu.VMEM((2, tm, max(FFN_CHUNKS)), F32),
            pltpu.VMEM((2, tm, max(FFN_CHUNKS)), F32),
            *mix_scratch,
        ],
        compiler_params=_cparams(("arbitrary",)),
        name="ffn",
    )(x, g, w_up, conv_w, conv_b, w_down, p1, p2, g_final, *mix_args)


def _proj_odd_kernel(x_ref, g_ref, w_ref, q_ref, kf_ref, vf_ref, kb_ref, vb_ref, *, v_transposed):
    tm = x_ref.shape[0]
    h = _rms(x_ref[...], g_ref[...]).astype(BF16)
    q = _dot(h, w_ref[:, 0:D_MODEL]) * SCORE_SCALE
    lane = lax.broadcasted_iota(jnp.int32, q.shape, 1) % DV_C
    q_ref[0] = jnp.where(lane < DC, q, 0.0).astype(BF16)
    q_ref[1] = jnp.where(lane >= DC, q, 0.0).astype(BF16)
    k = _dot(h, w_ref[:, D_MODEL:2 * D_MODEL])
    kf_ref[...] = k
    kb_ref[...] = k.astype(BF16)
    v = _dot(h, w_ref[:, 2 * D_MODEL:])
    vf_ref[...] = v
    if v_transposed:
        rows = DV_C + ATTN_ONES_ROWS
        for hd in range(H_C):
            vb_ref[hd * rows:hd * rows + DV_C, :] = v[:, hd * DV_C:(hd + 1) * DV_C].T.astype(BF16)
            vb_ref[hd * rows + DV_C:(hd + 1) * rows, :] = jnp.ones((ATTN_ONES_ROWS, tm), BF16)
    else:
        vb_ref[...] = v.astype(BF16)


def proj_odd(x, g, w, tm, v_transposed):
    M = x.shape[0]
    vt_rows = H_C * (DV_C + ATTN_ONES_ROWS)
    row_spec = pl.BlockSpec((tm, D_MODEL), lambda i: (i, 0))
    return pl.pallas_call(
        functools.partial(_proj_odd_kernel, v_transposed=v_transposed),
        grid=(M // tm,),
        in_specs=[row_spec, _const_spec((1, D_MODEL)), _const_spec((D_MODEL, 3 * D_MODEL))],
        out_specs=[pl.BlockSpec((2, tm, D_MODEL), lambda i: (0, i, 0))] + [row_spec] * 3
        + [pl.BlockSpec((vt_rows, tm), lambda i: (0, i)) if v_transposed else row_spec],
        out_shape=[
            jax.ShapeDtypeStruct((2, M, D_MODEL), BF16),
            jax.ShapeDtypeStruct((M, D_MODEL), F32),
            jax.ShapeDtypeStruct((M, D_MODEL), F32),
            jax.ShapeDtypeStruct((M, D_MODEL), BF16),
            jax.ShapeDtypeStruct((vt_rows, M) if v_transposed else (M, D_MODEL), BF16),
        ],
        compiler_params=_cparams(("arbitrary",)),
        name="proj_odd",
    )(x, g, w)


ATTN_STRIP = 256
ATTN_KEY_CHUNK = 256
ATTN_ONES_ROWS = 16


def _attn_prompt_kernel(it_ref, jt_ref, lam_ref, q_ref, k_ref, vt_ref, bias_ref, gain_ref, o_ref,
                        m_s, acc_s, s_s, *, tq, out_scale):
    i = it_ref[pl.program_id(1)]
    j = jt_ref[pl.program_id(1)]
    W = ATTN_STRIP
    KC = ATTN_KEY_CHUNK
    nstrip = tq // W

    @pl.when(j == 0)
    def _():
        m_s[...] = jnp.full_like(m_s, NEG_INF)
        acc_s[...] = jnp.zeros_like(acc_s)

    def tile(kind):
        strips = [(mp, rb) for mp in range(2) for rb in range(nstrip)]

        def nkeys(rb):
            return (rb + 1) * W if kind == 0 else tq

        def scores(idx):
            mp, rb = strips[idx]
            qs = q_ref[mp, rb * W:(rb + 1) * W, :]
            chunks = [_dot_nt(k_ref[c * KC:(c + 1) * KC, :], qs) for c in range(nkeys(rb) // KC)]

            def add_bias(key_block, b):
                for c in range(key_block * W // KC, (key_block + 1) * W // KC):
                    off = c * KC - key_block * W
                    chunks[c] = chunks[c] + b[off:off + KC, :]

            if kind == 0:
                add_bias(rb, bias_ref[0, 0])
                if rb >= 1:
                    add_bias(rb - 1, bias_ref[0, 1])
            elif kind == 1 and rb == 0:
                add_bias(nstrip - 1, bias_ref[0, 1])
            for c, s in enumerate(chunks):
                s_s[idx % 2, c * KC:(c + 1) * KC, :] = s

        def consume(idx):
            mp, rb = strips[idx]
            nchunks = nkeys(rb) // KC
            cols = slice(mp * tq + rb * W, mp * tq + (rb + 1) * W)
            chunks = [s_s[idx % 2, c * KC:(c + 1) * KC, :] for c in range(nchunks)]
            m_old = m_s[:, cols]
            m_new = m_old
            for s in chunks:
                m_new = jnp.maximum(m_new, jnp.max(s, axis=0, keepdims=True))
            alpha = jnp.exp2(m_old - m_new)
            ps = [jnp.exp2(s - m_new).astype(BF16) for s in chunks]
            pcat = jnp.concatenate(ps, axis=0) if nchunks > 1 else ps[0]
            acc_s[:, cols] = alpha * acc_s[:, cols] + _dot(vt_ref[:, 0:nchunks * KC], pcat)
            m_s[:, cols] = m_new

        scores(0)
        for idx in range(len(strips)):
            if idx + 1 < len(strips):
                scores(idx + 1)
            consume(idx)

    @pl.when(j < i - 1)
    def _():
        tile(2)

    @pl.when(j == i - 1)
    def _():
        tile(1)

    @pl.when(j == i)
    def _():
        tile(0)
        n = acc_s[0:DV_C, :] / acc_s[DV_C:DV_C + 1, :]
        o = n[:, 0:tq] - lam_ref[0] * n[:, tq:]
        o = o * lax.rsqrt(jnp.mean(o * o, axis=0, keepdims=True) + EPS) * gain_ref[...] * out_scale
        o_ref[...] = o.T.astype(BF16)


def attn_prompt(lam, q2, k, vt, bias, gain_col, tq, out_scale):
    T = k.shape[0]
    nq = T // tq
    pairs = [(i, j) for i in range(nq) for j in range(i + 1)]
    itab = jnp.asarray(np.array([p[0] for p in pairs], np.int32))
    jtab = jnp.asarray(np.array([p[1] for p in pairs], np.int32))
    grid_spec = pltpu.PrefetchScalarGridSpec(
        num_scalar_prefetch=2,
        grid=(H_C, len(pairs)),
        in_specs=[
            pl.BlockSpec(memory_space=pltpu.SMEM),
            pl.BlockSpec((2, tq, DV_C), lambda h, p, it, jt: (0, it[p], h)),
            pl.BlockSpec((tq, DV_C), lambda h, p, it, jt: (jt[p], h)),
            pl.BlockSpec((DV_C + ATTN_ONES_ROWS, tq), lambda h, p, it, jt: (h, jt[p])),
            pl.BlockSpec((1, 2, ATTN_STRIP, ATTN_STRIP), lambda h, p, it, jt: (h, 0, 0, 0)),
            pl.BlockSpec((DV_C, 1), lambda h, p, it, jt: (0, 0)),
        ],
        out_specs=pl.BlockSpec((tq, DV_C), lambda h, p, it, jt: (it[p], h)),
        scratch_shapes=[
            pltpu.VMEM((1, 2 * tq), F32),
            pltpu.VMEM((DV_C + ATTN_ONES_ROWS, 2 * tq), F32),
            pltpu.VMEM((2, tq, ATTN_STRIP), F32),
        ],
    )
    return pl.pallas_call(
        functools.partial(_attn_prompt_kernel, tq=tq, out_scale=out_scale),
        grid_spec=grid_spec,
        out_shape=jax.ShapeDtypeStruct((T, H_C * DV_C), BF16),
        compiler_params=_cparams(("arbitrary", "arbitrary")),
        name="attn_prompt",
    )(itab, jtab, lam, q2, k, vt, bias, gain_col)


PAGES_PER_STEP = 16
PAGE_GROUP = 2
ROWS_PER_HEAD = 16


def _attn_sample_kernel(pt_ref, lam_ref, q_ref, *refs, out_scale):
    P = PAGES_PER_STEP
    G = PAGE_GROUP
    R = ROWS_PER_HEAD
    ngroups = P // G
    k_refs = refs[0:P]
    v_refs = refs[P:2 * P]
    kn_ref, vn_ref, bias_last_ref, bias_new_ref, gain_ref, o_ref, m_s, l_s, acc_s, s_s = refs[2 * P:]
    j = pl.program_id(1)
    nj = pl.num_programs(1)

    @pl.when(j == 0)
    def _():
        m_s[...] = jnp.full_like(m_s, NEG_INF)
        l_s[...] = jnp.zeros_like(l_s)
        acc_s[...] = jnp.zeros_like(acc_s)

    def head_rows(page_refs, p, h):
        return page_refs[p][0, pl.ds(h, PAGE_SIZE, stride=H_C), :].astype(BF16)

    def scores(g):
        for h in range(H_C):
            kcat = jnp.concatenate([head_rows(k_refs, g * G + t, h) for t in range(G)], axis=0)
            s_s[g, h * R:(h + 1) * R, :] = _dot_nt(q_ref[0, h * R:(h + 1) * R, :], kcat)

    def update(s, v_of_head):
        m_old = m_s[...]
        m_new = jnp.maximum(m_old, jnp.max(s, axis=-1, keepdims=True))
        alpha = jnp.exp2(m_old - m_new)
        p = jnp.exp2(s - m_new).astype(BF16)
        l_s[...] = alpha * l_s[...] + jnp.sum(p.astype(F32), axis=-1, keepdims=True)
        for h in range(H_C):
            rows = slice(h * R, (h + 1) * R)
            acc_s[rows, :] = alpha[rows] * acc_s[rows, :] + _dot(p[rows, :], v_of_head(h))
        m_s[...] = m_new

    def v_group(g):
        return lambda h: jnp.concatenate([head_rows(v_refs, g * G + t, h) for t in range(G)], axis=0)

    @pl.when(j < nj - 1)
    def _():
        for g in range(ngroups):
            scores(g)
        for g in range(ngroups):
            update(s_s[g], v_group(g))

    @pl.when(j == nj - 1)
    def _():
        for g in range(ngroups):
            scores(g)
        for g in range(ngroups):
            s = s_s[g]
            if g == ngroups - 1:
                s = s + bias_last_ref[...]
            update(s, v_group(g))
        for h in range(H_C):
            kh = kn_ref[0, :, h * DV_C:(h + 1) * DV_C]
            s_s[0, h * R:(h + 1) * R, 0:PAGE_SIZE] = _dot_nt(q_ref[0, h * R:(h + 1) * R, :], kh)
        update(s_s[0, :, 0:PAGE_SIZE] + bias_new_ref[...], lambda h: vn_ref[0, :, h * DV_C:(h + 1) * DV_C])
        n = acc_s[...] / l_s[...]
        for h in range(H_C):
            o = n[h * R:h * R + 8, :] - lam_ref[0] * n[h * R + 8:(h + 1) * R, :]
            o_ref[0, h * 8:(h + 1) * 8, :] = _rms(o, gain_ref[...]) * out_scale


def attn_sample(page_table, lam, qm, cache_k, cache_v, k_new, v_new, bias_last, bias_new, gain, out_scale):
    B = qm.shape[0]
    P = PAGES_PER_STEP
    n_pages = page_table.shape[1]
    nj = n_pages // P
    rows = H_C * ROWS_PER_HEAD
    page_rows = PAGE_SIZE * H_C

    def page_spec(p):
        return pl.BlockSpec((1, page_rows, DV_C), lambda b, j, pt, p=p: (pt[b, j * P + p], 0, 0))

    grid_spec = pltpu.PrefetchScalarGridSpec(
        num_scalar_prefetch=1,
        grid=(B, nj),
        in_specs=[
            pl.BlockSpec(memory_space=pltpu.SMEM),
            pl.BlockSpec((1, rows, DV_C), lambda b, j, pt: (b, 0, 0)),
            *[page_spec(p) for p in range(P)],
            *[page_spec(p) for p in range(P)],
            pl.BlockSpec((1, PAGE_SIZE, H_C * DV_C), lambda b, j, pt: (b, 0, 0)),
            pl.BlockSpec((1, PAGE_SIZE, H_C * DV_C), lambda b, j, pt: (b, 0, 0)),
            pl.BlockSpec((rows, PAGE_GROUP * PAGE_SIZE), lambda b, j, pt: (0, 0)),
            pl.BlockSpec((rows, PAGE_SIZE), lambda b, j, pt: (0, 0)),
            pl.BlockSpec((1, DV_C), lambda b, j, pt: (0, 0)),
        ],
        out_specs=pl.BlockSpec((1, H_C * 8, DV_C), lambda b, j, pt: (b, 0, 0)),
        scratch_shapes=[
            pltpu.VMEM((rows, 1), F32),
            pltpu.VMEM((rows, 1), F32),
            pltpu.VMEM((rows, DV_C), F32),
            pltpu.VMEM((P // PAGE_GROUP, rows, PAGE_GROUP * PAGE_SIZE), F32),
        ],
    )
    return pl.pallas_call(
        functools.partial(_attn_sample_kernel, out_scale=out_scale),
        grid_spec=grid_spec,
        out_shape=jax.ShapeDtypeStruct((B, H_C * 8, DV_C), F32),
        compiler_params=_cparams(("arbitrary", "arbitrary")),
        name="attn_sample",
    )(page_table, lam, qm, *([cache_k] * P), *([cache_v] * P), k_new, v_new, bias_last, bias_new, gain)


def _t5_bucket_table():
    n = np.arange(MAX_DIST + 1)
    max_exact = N_BUCKETS // 2
    nf = np.maximum(n, 1).astype(np.float32)
    large = max_exact + (np.log(nf / max_exact) / math.log(MAX_DIST / max_exact) * (N_BUCKETS - max_exact)).astype(np.int32)
    large = np.minimum(large, N_BUCKETS - 1)
    return np.where(n < max_exact, n, large).astype(np.int32)


def _rel_bias_minus_far(rel_bias, rel):
    tab = _near_bias_table(rel_bias)
    vals = tab[np.clip(rel, 0, MAX_DIST)]
    vals = jnp.where(jnp.asarray(rel >= 0)[..., None], vals, NEG_INF)
    return jnp.moveaxis(vals, -1, 0).astype(F32)


def _near_bias_table(rel_bias):
    tab = rel_bias[_t5_bucket_table()]
    return ((tab - tab[MAX_DIST][None, :]) * LOG2E).astype(F32)


def _bias_blocks_t(rel_bias):
    W = ATTN_STRIP
    H = rel_bias.shape[1]
    f = jnp.concatenate([_near_bias_table(rel_bias), jnp.zeros((W - MAX_DIST - 1, H), F32)], axis=0).T
    g0 = jnp.concatenate([f, jnp.full((H, W), NEG_INF, F32)], axis=1)
    g1 = jnp.concatenate([jnp.zeros((H, W), F32), f], axis=1)
    g = jnp.stack([g0, g1], axis=1)
    rep = jnp.tile(g, (1, 1, W))[:, :, :W * (2 * W - 1)].reshape(H, 2, W, 2 * W - 1)
    return rep[:, :, :, :W]


TM_PROMPT = 512
TM_FFN = 512
SAMPLE_PAD = 16
TQ = 2048


def kernel(x_prompt, x_sample, state_mlstm_C, state_mlstm_n, state_mlstm_m, state_pool, cache_k, cache_v, state_ffn_conv, page_table, norm_mix, norm_ffn, norm_final, w_in_e, b_gate_e, mlstm_gain, w_pool, pool_scale, w_out_e, w_in_o, lambda_q1, lambda_k1, lambda_q2, lambda_k2, subln_gain, rel_bias, w_out_o, w_up, conv_w, conv_b, w_down):
    Bp, Tp = x_prompt.shape[:2]
    Bs, Ts = x_sample.shape[:2]
    assert Bp == 1
    Ms = Bs * Ts
    xp = x_prompt.reshape(Tp, D_MODEL)
    xs = x_sample.reshape(Ms, D_MODEL)
    row = lambda a: a.reshape(1, -1)

    w_in = w_in_e[0]
    n_gate = 2 * H_A
    w_main = jnp.concatenate([w_in[:, :4 * D_A], w_in[:, 4 * D_A + n_gate:], w_in[:, 4 * D_A:4 * D_A + n_gate],
                              jnp.zeros((D_MODEL, 128 - n_gate), F32)], axis=1).astype(BF16)
    b_col = jnp.pad(b_gate_e[0], (0, 128 - n_gate)).reshape(1, 128)
    g_mix0 = row(norm_mix[0])
    wp_b = w_pool[0].astype(BF16)
    wo_e = w_out_e[0].astype(BF16)
    gain_e = row(mlstm_gain[0])
    ps_e = row(pool_scale[0])

    qkv_p, ogu_p, gc_p, gr_p = proj_even(xp, g_mix0, w_main, b_col, TM_PROMPT)
    zc = jnp.zeros((1, H_A, DK_A, DK_A), F32)
    zn = jnp.zeros((1, H_A, DK_A), F32)
    zm = jnp.zeros((1, 8, 128), F32)
    hh_p, C_p, n_p, m_p = mlstm(qkv_p[None], ogu_p[None], gc_p[None], gr_p[None], zc, zn, zm, gain_e,
                                MLSTM_CHUNK, MLSTM_CHUNK, MLSTM_CHUNKS_PER_STEP)
    pool_args_p = (hh_p[0], ogu_p, jnp.zeros((POOL_HDR, D_B), F32), wp_b, ps_e, wo_e)
    pool_p = ogu_p[Tp - POOL_BUF:, D_A:][None]

    L = SAMPLE_PAD
    qkv_s, ogu_s, gc_s, gr_s = proj_even(xs, g_mix0, w_main, b_col, Ms)
    pad_t = lambda a, n: jnp.pad(a.reshape(Bs, Ts, a.shape[-1]), ((0, 0), (0, n - Ts), (0, 0)))
    gr_s3 = jnp.pad(gr_s.reshape(8, Bs, Ts).transpose(1, 0, 2), ((0, 0), (0, 0), (0, L - Ts)))
    m0_s = jnp.broadcast_to(jnp.pad(state_mlstm_m[0], ((0, 0), (0, 8 - H_A)))[:, :, None], (Bs, 8, 128))
    ogu_s3 = pad_t(ogu_s, L)
    hh_s, C_s, n_s, m_s = mlstm(pad_t(qkv_s, L), ogu_s3, pad_t(gc_s, L), gr_s3,
                                state_mlstm_C[0], state_mlstm_n[0], m0_s, gain_e, L, Ts, 1)
    prev16 = jnp.pad(state_pool[0], ((0, 0), (POOL_HDR - POOL_BUF, 0), (0, 0)))
    xs = pool_out(hh_s, ogu_s3, prev16, wp_b, ps_e, wo_e, pad_t(xs, L), L, PAST_LEN)[:, :Ts].reshape(Ms, D_MODEL)
    pool_s = jnp.concatenate([state_pool[0], ogu_s[:, D_A:].reshape(Bs, Ts, D_B)], axis=1)[:, -POOL_BUF:]

    wu = w_up.astype(BF16)
    wd = w_down.astype(BF16)

    def run_ffn(l, xp, xs, final_norm, mix_p=None, mix_s=None, w_mix=None, pool_p=None):
        g = row(norm_ffn[l])
        cb = row(conv_b[l])
        gf = row(norm_final)
        zp = jnp.zeros((8, D_FF), F32)
        xp, st_p = ffn(xp, g, wu, conv_w[l], cb, wd, zp, zp, gf, TM_FFN, TM_FFN, True, final_norm, l, mix_p, w_mix,
                       pool_p)
        st = state_ffn_conv[l]
        z1 = jnp.zeros((Bs, 1, D_FF), F32)
        p1 = jnp.concatenate([st[:, 1:2], z1, z1, z1], axis=1).reshape(Ms, D_FF)
        p2 = jnp.concatenate([st[:, 0:1], st[:, 1:2], z1, z1], axis=1).reshape(Ms, D_FF)
        xs, a_s = ffn(xs, g, wu, conv_w[l], cb, wd, p1, p2, gf, Ms, Ts, False, final_norm, l, mix_s, w_mix)
        conv_p = st_p[8 - (CONV_W - 1):][None]
        conv_s = a_s.reshape(Bs, Ts, D_FF)[:, Ts - (CONV_W - 1):]
        return xp, xs, conv_p, conv_s

    xp, xs, conv_p0, conv_s0 = run_ffn(0, xp, xs, False, pool_p=pool_args_p)

    lam_init = 0.8 - 0.6 * math.exp(-0.3 * 1)
    lam = (jnp.exp(jnp.sum(lambda_q1[0] * lambda_k1[0])) - jnp.exp(jnp.sum(lambda_q2[0] * lambda_k2[0])) + lam_init).astype(F32).reshape(1)
    out_scale = 1.0 - lam_init
    g_mix1 = row(norm_mix[1])
    w_qkv = w_in_o[0].astype(BF16)
    wo_o = w_out_o[0].astype(BF16)
    gain_o = row(subln_gain[0])

    q2_p, kf_p, vf_p, kb_p, vt_p = proj_odd(xp, g_mix1, w_qkv, TM_PROMPT, True)
    o_p = attn_prompt(lam, q2_p, kb_p, vt_p, _bias_blocks_t(rel_bias), gain_o.reshape(DV_C, 1), TQ, out_scale)

    q2_s, kf_s, vf_s, kb_s, vb_s = proj_odd(xs, g_mix1, w_qkv, Ms, False)
    qm = q2_s.reshape(2, Bs, Ts, H_C, DV_C).transpose(1, 3, 0, 2, 4)
    qm = jnp.pad(qm, ((0, 0), (0, 0), (0, 0), (0, 8 - Ts), (0, 0)))
    qm = qm.reshape(Bs, H_C * ROWS_PER_HEAD, DV_C)
    tok = np.minimum(np.arange(8), Ts - 1)
    tok = np.tile(tok, 2)
    ccol = np.arange(PAGE_SIZE)
    rel_last = PAGE_SIZE + tok[:, None] - ccol[None, :]
    rel_new = np.where(ccol[None, :] < Ts, tok[:, None] - ccol[None, :], -1)
    bias_last = _rel_bias_minus_far(rel_bias, rel_last).reshape(H_C * ROWS_PER_HEAD, PAGE_SIZE)
    bias_last = jnp.pad(bias_last, ((0, 0), ((PAGE_GROUP - 1) * PAGE_SIZE, 0)))
    bias_new = _rel_bias_minus_far(rel_bias, rel_new).reshape(H_C * ROWS_PER_HEAD, PAGE_SIZE)
    n_phys = cache_k.shape[1]
    ck = cache_k[0].reshape(n_phys, PAGE_SIZE * H_C, DV_C)
    cv = cache_v[0].reshape(n_phys, PAGE_SIZE * H_C, DV_C)
    kn = jnp.pad(kb_s.reshape(Bs, Ts, D_MODEL), ((0, 0), (0, PAGE_SIZE - Ts), (0, 0)))
    vn = jnp.pad(vb_s.reshape(Bs, Ts, D_MODEL), ((0, 0), (0, PAGE_SIZE - Ts), (0, 0)))
    o_s = attn_sample(page_table, lam, qm, ck, cv, kn, vn, bias_last, bias_new, gain_o, out_scale)
    o_s = o_s.reshape(Bs, H_C, 8, DV_C)[:, :, :Ts].transpose(0, 2, 1, 3).reshape(Ms, D_MODEL).astype(BF16)

    yp, ys, conv_p1, conv_s1 = run_ffn(1, xp, xs, True, o_p, o_s, wo_o)

    y_prompt = yp.reshape(Bp, Tp, D_MODEL)
    y_sample = ys.reshape(Bs, Ts, D_MODEL)
    new_m_p = m_p[:, :H_A, 0]
    new_m_s = m_s[:, :H_A, 0]
    new_k_p = kf_p.reshape(1, Bp, Tp, H_C, DV_C)
    new_v_p = vf_p.reshape(1, Bp, Tp, H_C, DV_C)
    new_k_s = kf_s.reshape(1, Bs, Ts, H_C, DV_C)
    new_v_s = vf_s.reshape(1, Bs, Ts, H_C, DV_C)
    return (y_prompt, y_sample,
            C_p[None], n_p[None], new_m_p[None], pool_p[None], new_k_p, new_v_p,
            jnp.stack([conv_p0, conv_p1]),
            C_s[None], n_s[None], new_m_s[None], pool_s[None], new_k_s, new_v_s,
            jnp.stack([conv_s0, conv_s1]))
```

```python
import functools
import math

import numpy as np
import jax
import jax.numpy as jnp
from jax import lax
from jax.experimental import pallas as pl
from jax.experimental.pallas import tpu as pltpu

F32 = jnp.float32
BF16 = jnp.bfloat16
HIGHEST = lax.Precision.HIGHEST

D_MODEL = 1024
PAST_LEN = 16384
PAGE_SIZE = 128
D_A = 512
H_A = 4
DK_A = 128
MLSTM_CHUNK = 128
MLSTM_CHUNKS_PER_STEP = 4
D_B = 512
POOL_WINDOWS = (2, 4, 8, 16)
G_B = 128
POOL_BUF = 15
POOL_HDR = 16
H_C = 8
DC = 64
DV_C = 128
N_BUCKETS = 32
MAX_DIST = 128
LOG2E = math.log2(math.e)
SCORE_SCALE = DC ** -0.5 * LOG2E
D_FF = 2816
CONV_W = 3
EPS = 1e-6

VMEM_LIMIT = 56 * 1024 * 1024
NEG_INF = float("-inf")


def _cparams(sem):
    return pltpu.CompilerParams(dimension_semantics=sem, vmem_limit_bytes=VMEM_LIMIT)


def _const_spec(shape):
    nd = len(shape)
    return pl.BlockSpec(shape, lambda *_: (0,) * nd, pipeline_mode=pl.Buffered(1))


def _rms(x, g):
    return x * lax.rsqrt(jnp.mean(x * x, axis=-1, keepdims=True) + EPS) * g


def _dot(a, b):
    return jnp.dot(a, b, preferred_element_type=F32)


def _dot_nt(a, b):
    return lax.dot_general(a, b, (((1,), (1,)), ((), ())), preferred_element_type=F32)


def _dot_tn(a, b):
    return lax.dot_general(a, b, (((0,), (0,)), ((), ())), preferred_element_type=F32)


def _log_sigmoid(x):
    return jnp.minimum(x, 0.0) - jnp.log1p(jnp.exp(-jnp.abs(x)))


def _sigmoid(x):
    return 1.0 / (1.0 + jnp.exp(-x))


def _proj_even_kernel(x_ref, g_ref, w_ref, bc_ref, qkv_ref, ogu_ref, gc_ref, gr_ref):
    h = _rms(x_ref[...], g_ref[...])
    z = _dot(h.astype(BF16), w_ref[...])
    qkv_ref[:, 0:D_A] = z[:, 0:D_A].astype(BF16)
    qkv_ref[:, D_A:2 * D_A] = (z[:, D_A:2 * D_A] * (DK_A ** -0.5)).astype(BF16)
    qkv_ref[:, 2 * D_A:3 * D_A] = z[:, 2 * D_A:3 * D_A].astype(BF16)
    ogu_ref[...] = z[:, 3 * D_A:3 * D_A + D_A + D_B]
    gc = z[:, 3 * D_A + D_A + D_B:] + bc_ref[...]
    gc_ref[...] = gc
    gr_ref[...] = gc.T[0:8, :]


def proj_even(x, g, w_main, b_col, tm):
    M = x.shape[0]
    n_main = w_main.shape[1]
    return pl.pallas_call(
        _proj_even_kernel,
        grid=(M // tm,),
        in_specs=[
            pl.BlockSpec((tm, D_MODEL), lambda i: (i, 0)),
            _const_spec((1, D_MODEL)),
            _const_spec((D_MODEL, n_main)),
            _const_spec((1, 128)),
        ],
        out_specs=[
            pl.BlockSpec((tm, 3 * D_A), lambda i: (i, 0)),
            pl.BlockSpec((tm, D_A + D_B), lambda i: (i, 0)),
            pl.BlockSpec((tm, 128), lambda i: (i, 0)),
            pl.BlockSpec((8, tm), lambda i: (0, i)),
        ],
        out_shape=[
            jax.ShapeDtypeStruct((M, 3 * D_A), BF16),
            jax.ShapeDtypeStruct((M, D_A + D_B), F32),
            jax.ShapeDtypeStruct((M, 128), F32),
            jax.ShapeDtypeStruct((8, M), F32),
        ],
        compiler_params=_cparams(("arbitrary",)),
        name="proj_even",
    )(x, g, w_main, b_col)


def _mlstm_kernel(qkv_ref, og_ref, gc_ref, gr_ref, c0_ref, n0_ref, m0_ref, gain_ref,
                  hh_ref, c_out_ref, n_out_ref, m_out_ref, c_s, n_s, m_s, *, L, valid, chunks):
    c = pl.program_id(1)

    @pl.when(c == 0)
    def _():
        c_s[...] = c0_ref[0]
        n_s[...] = n0_ref[0]
        m_s[...] = m0_ref[0]

    row = lax.broadcasted_iota(jnp.int32, (L, L), 0)
    col = lax.broadcasted_iota(jnp.int32, (L, L), 1)
    tri = (col <= row).astype(F32)
    mask = (col <= row) & (col < valid)
    rvalid = lax.broadcasted_iota(jnp.int32, (L, 1), 0) < valid
    cvalid = lax.broadcasted_iota(jnp.int32, (1, L), 1) < valid

    c_st = [c_s[h] for h in range(H_A)]
    n_st = [n_s[h:h + 1, :] for h in range(H_A)]
    m_st = [m_s[h:h + 1, 0:1] for h in range(H_A)]

    for ci in range(chunks):
        rows = slice(ci * L, (ci + 1) * L)
        gcol = gc_ref[0, rows, :]
        grow = gr_ref[0, :, rows]
        lf_col = jnp.where(rvalid, _log_sigmoid(gcol), 0.0)
        lf_row = jnp.where(cvalid, _log_sigmoid(grow), 0.0)
        b_col_all = jnp.dot(tri, lf_col, precision=HIGHEST, preferred_element_type=F32)
        b_row_all = lax.dot_general(lf_row, tri, (((1,), (1,)), ((), ())), precision=HIGHEST,
                                    preferred_element_type=F32)
        for h in range(H_A):
            bc = b_col_all[:, H_A + h:H_A + h + 1]
            br = b_row_all[H_A + h:H_A + h + 1, :]
            igc = gcol[:, h:h + 1]
            igr = grow[h:h + 1, :]
            m0 = m_st[h]
            logd = jnp.where(mask, bc - br + igr, NEG_INF)
            log_inter = bc + m0
            m_t = jnp.maximum(log_inter, jnp.max(logd, axis=-1, keepdims=True))
            dm = jnp.exp(logd - m_t)
            w_inter = jnp.exp(log_inter - m_t)
            q = qkv_ref[0, rows, h * DK_A:(h + 1) * DK_A]
            k = qkv_ref[0, rows, D_A + h * DK_A:D_A + (h + 1) * DK_A]
            v = qkv_ref[0, rows, 2 * D_A + h * DK_A:2 * D_A + (h + 1) * DK_A]
            s = _dot_nt(q, k) * dm
            c_old = c_st[h]
            n_old = n_st[h]
            v_ext = jnp.concatenate([v, jnp.ones((L, DK_A), BF16)], axis=1)
            c_ext = jnp.concatenate([c_old, jnp.broadcast_to(n_old, (DK_A, DK_A))], axis=0).astype(BF16)
            both = _dot(s.astype(BF16), v_ext) + w_inter * _dot_nt(q, c_ext)
            denom = jnp.maximum(jnp.abs(both[:, DK_A:]), jnp.exp(-m_t))
            hh = both[:, :DK_A] / denom
            y = _rms(hh, gain_ref[:, h * DK_A:(h + 1) * DK_A]) * _sigmoid(og_ref[0, rows, h * DK_A:(h + 1) * DK_A])
            hh_ref[0, rows, h * DK_A:(h + 1) * DK_A] = y.astype(BF16)
            m_new = m_t[valid - 1:valid, :]
            b_last = bc[valid - 1:valid, :]
            w_s = jnp.where(rvalid, jnp.exp(b_last - bc + igc - m_new), 0.0)
            decay = jnp.exp(b_last + m0 - m_new)
            kf = k.astype(F32)
            vw = (v.astype(F32) * w_s).astype(BF16)
            c_st[h] = decay * c_old + _dot_tn(vw, k)
            n_st[h] = decay * n_old + jnp.sum(kf * w_s, axis=0, keepdims=True)
            m_st[h] = m_new

    for h in range(H_A):
        c_s[h] = c_st[h]
        n_s[h:h + 1, :] = n_st[h]
        m_s[h:h + 1, :] = jnp.broadcast_to(m_st[h], (1, 128))

    @pl.when(c == pl.num_programs(1) - 1)
    def _():
        c_out_ref[0] = c_s[...]
        n_out_ref[0] = n_s[...]
        m_out_ref[0] = m_s[...]


def mlstm(qkv, ogu, gc, gr, c0, n0, m0, gain, L, valid, chunks):
    B, T = qkv.shape[:2]
    LB = L * chunks
    nc = T // LB
    return pl.pallas_call(
        functools.partial(_mlstm_kernel, L=L, valid=valid, chunks=chunks),
        grid=(B, nc),
        in_specs=[
            pl.BlockSpec((1, LB, 3 * D_A), lambda b, c: (b, c, 0)),
            pl.BlockSpec((1, LB, D_A), lambda b, c: (b, c, 0)),
            pl.BlockSpec((1, LB, 128), lambda b, c: (b, c, 0)),
            pl.BlockSpec((1, 8, LB), lambda b, c: (b, 0, c)),
            pl.BlockSpec((1, H_A, DK_A, DK_A), lambda b, c: (b, 0, 0, 0)),
            pl.BlockSpec((1, H_A, DK_A), lambda b, c: (b, 0, 0)),
            pl.BlockSpec((1, 8, 128), lambda b, c: (b, 0, 0)),
            pl.BlockSpec((1, D_A), lambda b, c: (0, 0)),
        ],
        out_specs=[
            pl.BlockSpec((1, LB, D_A), lambda b, c: (b, c, 0)),
            pl.BlockSpec((1, H_A, DK_A, DK_A), lambda b, c: (b, 0, 0, 0)),
            pl.BlockSpec((1, H_A, DK_A), lambda b, c: (b, 0, 0)),
            pl.BlockSpec((1, 8, 128), lambda b, c: (b, 0, 0)),
        ],
        out_shape=[
            jax.ShapeDtypeStruct((B, T, D_A), BF16),
            jax.ShapeDtypeStruct((B, H_A, DK_A, DK_A), F32),
            jax.ShapeDtypeStruct((B, H_A, DK_A), F32),
            jax.ShapeDtypeStruct((B, 8, 128), F32),
        ],
        scratch_shapes=[
            pltpu.VMEM((H_A, DK_A, DK_A), F32),
            pltpu.VMEM((H_A, DK_A), F32),
            pltpu.VMEM((8, 128), F32),
        ],
        compiler_params=_cparams(("arbitrary", "arbitrary")),
        name="mlstm",
    )(qkv, ogu, gc, gr, c0, n0, m0, gain)


POOL_LEAD = 8
assert POOL_WINDOWS == tuple(2 ** (g + 1) for g in range(len(POOL_WINDOWS)))


def _pool_scratch(tm):
    rows = POOL_LEAD + POOL_HDR + tm
    return [pltpu.VMEM((rows, D_B), F32), pltpu.VMEM((rows, D_B - G_B), F32), pltpu.VMEM((rows, D_B - 2 * G_B), F32)]


def _even_mixer_out(hh, u, prev, wp_ref, ps_ref, wo_ref, bufs, t, *, tm, nt, pos0):
    e_s, pa_s, pb_s = bufs
    H = POOL_HDR
    lo = POOL_LEAD
    top = lo + H + tm
    out0 = H

    @pl.when(t == 0)
    def _():
        e_s[0:lo, :] = jnp.zeros((lo, e_s.shape[1]), F32)
        pa_s[0:lo, :] = jnp.zeros((lo, pa_s.shape[1]), F32)
        pb_s[0:lo, :] = jnp.zeros((lo, pb_s.shape[1]), F32)
        e_s[lo:lo + H, :] = prev

    if nt > 1:
        @pl.when(t > 0)
        def _():
            e_s[lo:lo + H, :] = e_s[lo + tm:lo + tm + H, :]

    e_s[lo + H:top, :] = u
    pos = pos0 + t * tm + lax.broadcasted_iota(jnp.int32, (tm, 1), 0)
    ys = []
    src, dst = e_s, pa_s
    for g, w in enumerate(POOL_WINDOWS):
        ncol = D_B - g * G_B
        shift = w // 2
        sums = src[lo:top, 0:ncol] + src[lo - shift:top - shift, 0:ncol]
        if g + 1 < len(POOL_WINDOWS):
            dst[lo:top, 0:ncol - G_B] = sums[:, G_B:]
        cur = e_s[lo + H:top, g * G_B:(g + 1) * G_B]
        cnt = jnp.minimum(pos + 1, w).astype(F32)
        pooled = sums[out0:, 0:G_B] / cnt - cur
        ys.append(_dot(pooled.astype(BF16), wp_ref[g]))
        src, dst = dst, (pb_s if dst is pa_s else pa_s)
    yb = jnp.concatenate(ys, axis=-1) * ps_ref[...]
    return _dot(hh, wo_ref[0:D_A, :]) + _dot(yb.astype(BF16), wo_ref[D_A:, :])


def _pool_out_kernel(hh_ref, u_ref, prev_ref, wp_ref, ps_ref, wo_ref, x_ref, o_ref, *bufs, tm, nt, pos0):
    y = _even_mixer_out(hh_ref[0], u_ref[0], prev_ref[0], wp_ref, ps_ref, wo_ref, bufs, pl.program_id(1),
                        tm=tm, nt=nt, pos0=pos0)
    o_ref[0] = x_ref[0] + y


def pool_out(hh, ogu, prev16, w_pool, pool_scale, w_out, x, tm, pos0):
    B, T = x.shape[:2]
    nt = T // tm
    return pl.pallas_call(
        functools.partial(_pool_out_kernel, tm=tm, nt=nt, pos0=pos0),
        grid=(B, nt),
        in_specs=[
            pl.BlockSpec((1, tm, D_A), lambda b, t: (b, t, 0)),
            pl.BlockSpec((1, tm, D_B), lambda b, t: (b, t, 1)),
            pl.BlockSpec((1, POOL_HDR, D_B), lambda b, t: (b, 0, 0)),
            _const_spec((len(POOL_WINDOWS), G_B, G_B)),
            _const_spec((1, D_B)),
            _const_spec((D_A + D_B, D_MODEL)),
            pl.BlockSpec((1, tm, D_MODEL), lambda b, t: (b, t, 0)),
        ],
        out_specs=pl.BlockSpec((1, tm, D_MODEL), lambda b, t: (b, t, 0)),
        out_shape=jax.ShapeDtypeStruct((B, T, D_MODEL), F32),
        scratch_shapes=_pool_scratch(tm),
        compiler_params=_cparams(("arbitrary", "arbitrary")),
        name="pool_out",
    )(hh, ogu, prev16, w_pool, pool_scale, w_out, x)


def _gelu_tanh(y):
    return 0.5 * y * (1.0 + jnp.tanh(math.sqrt(2.0 / math.pi) * (y + 0.044715 * (y * y * y))))


FFN_CHUNKS = (1024, 1024, 768)
assert sum(FFN_CHUNKS) == D_FF


def _ffn_kernel(x_ref, g_ref, wup_ref, cw_ref, cb_ref, wdn_ref, p1_ref, p2_ref, gf_ref, *refs,
                tm, nt, seq_len, carried, final_norm, mixer):
    i = pl.program_id(0)
    x = x_ref[...]
    if mixer == "dense":
        mix_ref, wmix_ref, o_ref, st_ref, carry_s, a_s, g_s = refs
        x = x + _dot(mix_ref[...], wmix_ref[...])
    elif mixer == "pool":
        hh_ref, u_ref, prev_ref, wp_ref, ps_ref, wmix_ref, o_ref, st_ref, carry_s, a_s, g_s, *pool_bufs = refs
        x = x + _even_mixer_out(hh_ref[...], u_ref[...], prev_ref[...], wp_ref, ps_ref, wmix_ref, pool_bufs, i,
                                tm=tm, nt=nt, pos0=0)
    else:
        o_ref, st_ref, carry_s, a_s, g_s = refs
    offs = [sum(FFN_CHUNKS[:c]) for c in range(len(FFN_CHUNKS) + 1)]
    nch = len(FFN_CHUNKS)
    h = _rms(x, g_ref[...]).astype(BF16)
    t = lax.broadcasted_iota(jnp.int32, (tm, 1), 0) % seq_len

    if carried:
        @pl.when(i == 0)
        def _():
            carry_s[...] = jnp.zeros_like(carry_s)

    def up(c):
        w = FFN_CHUNKS[c]
        a_s[c % 2, :, 0:w] = _dot(h, wup_ref[:, offs[c]:offs[c + 1]])
        g_s[c % 2, :, 0:w] = _dot(h, wup_ref[:, D_FF + offs[c]:D_FF + offs[c + 1]])

    def act_down(c):
        cols = slice(offs[c], offs[c + 1])
        w = FFN_CHUNKS[c]
        a = a_s[c % 2, :, 0:w]
        s1 = jnp.where(t >= 1, pltpu.roll(a, 1, 0), 0.0)
        s2 = jnp.where(t >= 2, pltpu.roll(a, 2, 0), 0.0)
        if carried:
            prev0 = carry_s[6:7, cols]
            prev1 = carry_s[7:8, cols]
            s1 = s1 + jnp.where(t == 0, prev1, 0.0)
            s2 = s2 + jnp.where(t == 0, prev0, 0.0) + jnp.where(t == 1, prev1, 0.0)
            carry_s[:, cols] = a[tm - 8:tm, :]
            st_ref[:, cols] = a[tm - 8:tm, :]
        else:
            s1 = s1 + p1_ref[:, cols]
            s2 = s2 + p2_ref[:, cols]
            st_ref[:, cols] = a
        y = cb_ref[:, cols] + cw_ref[0:1, cols] * s2 + cw_ref[1:2, cols] * s1 + cw_ref[2:3, cols] * a
        act = (_gelu_tanh(y) * g_s[c % 2, :, 0:w]).astype(BF16)
        return _dot(act, wdn_ref[cols, :])

    up(0)
    for c in range(nch):
        if c + 1 < nch:
            up(c + 1)
        d = act_down(c)
        if c == 0:
            o_ref[...] = x + d
        else:
            o_ref[...] += d
    if final_norm:
        o_ref[...] = _rms(o_ref[...], gf_ref[...])


def _layer_spec(shape, layer):
    nd = len(shape)
    return pl.BlockSpec((None,) + tuple(shape), lambda *_: (layer,) + (0,) * nd, pipeline_mode=pl.Buffered(1))


def ffn(x, g, w_up, conv_w, conv_b, w_down, p1, p2, g_final, tm, seq_len, carried, final_norm, layer,
        mix=None, w_mix=None, pool=None):
    M = x.shape[0]
    nt = M // tm
    st_rows = 8 if carried else tm
    st_total = 8 if carried else M
    row_spec = lambda n: pl.BlockSpec((tm, n), lambda i: (i, 0))
    p_spec = _const_spec((8, D_FF)) if carried else row_spec(D_FF)
    mixer, mix_specs, mix_args, mix_scratch = None, [], [], []
    if mix is not None:
        mixer = "dense"
        mix_specs = [row_spec(D_MODEL), _const_spec((D_MODEL, D_MODEL))]
        mix_args = [mix, w_mix]
    elif pool is not None:
        assert carried
        mixer = "pool"
        mix_specs = [
            row_spec(D_A),
            pl.BlockSpec((tm, D_B), lambda i: (i, 1)),
            _const_spec((POOL_HDR, D_B)),
            _const_spec((len(POOL_WINDOWS), G_B, G_B)),
            _const_spec((1, D_B)),
            _const_spec((D_A + D_B, D_MODEL)),
        ]
        mix_args = list(pool)
        mix_scratch = _pool_scratch(tm)
    return pl.pallas_call(
        functools.partial(_ffn_kernel, tm=tm, nt=nt, seq_len=seq_len, carried=carried, final_norm=final_norm,
                          mixer=mixer),
        grid=(nt,),
        in_specs=[
            row_spec(D_MODEL),
            _const_spec((1, D_MODEL)),
            _layer_spec((D_MODEL, 2 * D_FF), layer),
            _const_spec((CONV_W, D_FF)),
            _const_spec((1, D_FF)),
            _layer_spec((D_FF, D_MODEL), layer),
            p_spec,
            p_spec,
            _const_spec((1, D_MODEL)),
            *mix_specs,
        ],
        out_specs=[
            row_spec(D_MODEL),
            pl.BlockSpec((st_rows, D_FF), (lambda i: (0, 0)) if carried else (lambda i: (i, 0))),
        ],
        out_shape=[
            jax.ShapeDtypeStruct((M, D_MODEL), F32),
            jax.ShapeDtypeStruct((st_total, D_FF), F32),
        ],
        scratch_shapes=[
            pltpu.VMEM((8, D_FF), F32),
            pltpu.VMEM((2, tm, max(FFN_CHUNKS)), F32),
            pltpu.VMEM((2, tm, max(FFN_CHUNKS)), F32),
            *mix_scratch,
        ],
        compiler_params=_cparams(("arbitrary",)),
        name="ffn",
    )(x, g, w_up, conv_w, conv_b, w_down, p1, p2, g_final, *mix_args)


def _proj_odd_kernel(x_ref, g_ref, w_ref, q_ref, kf_ref, vf_ref, kb_ref, vb_ref, *, v_transposed):
    tm = x_ref.shape[0]
    h = _rms(x_ref[...], g_ref[...]).astype(BF16)
    q = _dot(h, w_ref[:, 0:D_MODEL]) * SCORE_SCALE
    lane = lax.broadcasted_iota(jnp.int32, q.shape, 1) % DV_C
    q_ref[0] = jnp.where(lane < DC, q, 0.0).astype(BF16)
    q_ref[1] = jnp.where(lane >= DC, q, 0.0).astype(BF16)
    k = _dot(h, w_ref[:, D_MODEL:2 * D_MODEL])
    kf_ref[...] = k
    kb_ref[...] = k.astype(BF16)
    v = _dot(h, w_ref[:, 2 * D_MODEL:])
    vf_ref[...] = v
    if v_transposed:
        rows = DV_C + ATTN_ONES_ROWS
        for hd in range(H_C):
            vb_ref[hd * rows:hd * rows + DV_C, :] = v[:, hd * DV_C:(hd + 1) * DV_C].T.astype(BF16)
            vb_ref[hd * rows + DV_C:(hd + 1) * rows, :] = jnp.ones((ATTN_ONES_ROWS, tm), BF16)
    else:
        vb_ref[...] = v.astype(BF16)


def proj_odd(x, g, w, tm, v_transposed):
    M = x.shape[0]
    vt_rows = H_C * (DV_C + ATTN_ONES_ROWS)
    row_spec = pl.BlockSpec((tm, D_MODEL), lambda i: (i, 0))
    return pl.pallas_call(
        functools.partial(_proj_odd_kernel, v_transposed=v_transposed),
        grid=(M // tm,),
        in_specs=[row_spec, _const_spec((1, D_MODEL)), _const_spec((D_MODEL, 3 * D_MODEL))],
        out_specs=[pl.BlockSpec((2, tm, D_MODEL), lambda i: (0, i, 0))] + [row_spec] * 3
        + [pl.BlockSpec((vt_rows, tm), lambda i: (0, i)) if v_transposed else row_spec],
        out_shape=[
            jax.ShapeDtypeStruct((2, M, D_MODEL), BF16),
            jax.ShapeDtypeStruct((M, D_MODEL), F32),
            jax.ShapeDtypeStruct((M, D_MODEL), F32),
            jax.ShapeDtypeStruct((M, D_MODEL), BF16),
            jax.ShapeDtypeStruct((vt_rows, M) if v_transposed else (M, D_MODEL), BF16),
        ],
        compiler_params=_cparams(("arbitrary",)),
        name="proj_odd",
    )(x, g, w)


ATTN_STRIP = 256
ATTN_KEY_CHUNK = 256
ATTN_ONES_ROWS = 16


def _attn_prompt_kernel(it_ref, jt_ref, lam_ref, q_ref, k_ref, vt_ref, bias_ref, gain_ref, o_ref,
                        m_s, acc_s, s_s, *, tq, out_scale):
    i = it_ref[pl.program_id(1)]
    j = jt_ref[pl.program_id(1)]
    W = ATTN_STRIP
    KC = ATTN_KEY_CHUNK
    nstrip = tq // W

    @pl.when(j == 0)
    def _():
        m_s[...] = jnp.full_like(m_s, NEG_INF)
        acc_s[...] = jnp.zeros_like(acc_s)

    def tile(kind):
        strips = [(mp, rb) for mp in range(2) for rb in range(nstrip)]

        def nkeys(rb):
            return (rb + 1) * W if kind == 0 else tq

        def scores(idx):
            mp, rb = strips[idx]
            qs = q_ref[mp, rb * W:(rb + 1) * W, :]
            chunks = [_dot_nt(k_ref[c * KC:(c + 1) * KC, :], qs) for c in range(nkeys(rb) // KC)]

            def add_bias(key_block, b):
                for c in range(key_block * W // KC, (key_block + 1) * W // KC):
                    off = c * KC - key_block * W
                    chunks[c] = chunks[c] + b[off:off + KC, :]

            if kind == 0:
                add_bias(rb, bias_ref[0, 0])
                if rb >= 1:
                    add_bias(rb - 1, bias_ref[0, 1])
            elif kind == 1 and rb == 0:
                add_bias(nstrip - 1, bias_ref[0, 1])
            for c, s in enumerate(chunks):
                s_s[idx % 2, c * KC:(c + 1) * KC, :] = s

        def consume(idx):
            mp, rb = strips[idx]
            nchunks = nkeys(rb) // KC
            cols = slice(mp * tq + rb * W, mp * tq + (rb + 1) * W)
            chunks = [s_s[idx % 2, c * KC:(c + 1) * KC, :] for c in range(nchunks)]
            m_old = m_s[:, cols]
            m_new = m_old
            for s in chunks:
                m_new = jnp.maximum(m_new, jnp.max(s, axis=0, keepdims=True))
            alpha = jnp.exp2(m_old - m_new)
            ps = [jnp.exp2(s - m_new).astype(BF16) for s in chunks]
            pcat = jnp.concatenate(ps, axis=0) if nchunks > 1 else ps[0]
            acc_s[:, cols] = alpha * acc_s[:, cols] + _dot(vt_ref[:, 0:nchunks * KC], pcat)
            m_s[:, cols] = m_new

        scores(0)
        for idx in range(len(strips)):
            if idx + 1 < len(strips):
                scores(idx + 1)
            consume(idx)

    @pl.when(j < i - 1)
    def _():
        tile(2)

    @pl.when(j == i - 1)
    def _():
        tile(1)

    @pl.when(j == i)
    def _():
        tile(0)
        n = acc_s[0:DV_C, :] / acc_s[DV_C:DV_C + 1, :]
        o = n[:, 0:tq] - lam_ref[0] * n[:, tq:]
        o = o * lax.rsqrt(jnp.mean(o * o, axis=0, keepdims=True) + EPS) * gain_ref[...] * out_scale
        o_ref[...] = o.T.astype(BF16)


def attn_prompt(lam, q2, k, vt, bias, gain_col, tq, out_scale):
    T = k.shape[0]
    nq = T // tq
    pairs = [(i, j) for i in range(nq) for j in range(i + 1)]
    itab = jnp.asarray(np.array([p[0] for p in pairs], np.int32))
    jtab = jnp.asarray(np.array([p[1] for p in pairs], np.int32))
    grid_spec = pltpu.PrefetchScalarGridSpec(
        num_scalar_prefetch=2,
        grid=(H_C, len(pairs)),
        in_specs=[
            pl.BlockSpec(memory_space=pltpu.SMEM),
            pl.BlockSpec((2, tq, DV_C), lambda h, p, it, jt: (0, it[p], h)),
            pl.BlockSpec((tq, DV_C), lambda h, p, it, jt: (jt[p], h)),
            pl.BlockSpec((DV_C + ATTN_ONES_ROWS, tq), lambda h, p, it, jt: (h, jt[p])),
            pl.BlockSpec((1, 2, ATTN_STRIP, ATTN_STRIP), lambda h, p, it, jt: (h, 0, 0, 0)),
            pl.BlockSpec((DV_C, 1), lambda h, p, it, jt: (0, 0)),
        ],
        out_specs=pl.BlockSpec((tq, DV_C), lambda h, p, it, jt: (it[p], h)),
        scratch_shapes=[
            pltpu.VMEM((1, 2 * tq), F32),
            pltpu.VMEM((DV_C + ATTN_ONES_ROWS, 2 * tq), F32),
            pltpu.VMEM((2, tq, ATTN_STRIP), F32),
        ],
    )
    return pl.pallas_call(
        functools.partial(_attn_prompt_kernel, tq=tq, out_scale=out_scale),
        grid_spec=grid_spec,
        out_shape=jax.ShapeDtypeStruct((T, H_C * DV_C), BF16),
        compiler_params=_cparams(("arbitrary", "arbitrary")),
        name="attn_prompt",
    )(itab, jtab, lam, q2, k, vt, bias, gain_col)


PAGES_PER_STEP = 16
PAGE_GROUP = 2
ROWS_PER_HEAD = 16


def _attn_sample_kernel(pt_ref, lam_ref, q_ref, *refs, out_scale):
    P = PAGES_PER_STEP
    G = PAGE_GROUP
    R = ROWS_PER_HEAD
    ngroups = P // G
    k_refs = refs[0:P]
    v_refs = refs[P:2 * P]
    kn_ref, vn_ref, bias_last_ref, bias_new_ref, gain_ref, o_ref, m_s, l_s, acc_s, s_s = refs[2 * P:]
    j = pl.program_id(1)
    nj = pl.num_programs(1)

    @pl.when(j == 0)
    def _():
        m_s[...] = jnp.full_like(m_s, NEG_INF)
        l_s[...] = jnp.zeros_like(l_s)
        acc_s[...] = jnp.zeros_like(acc_s)

    def head_rows(page_refs, p, h):
        return page_refs[p][0, pl.ds(h, PAGE_SIZE, stride=H_C), :].astype(BF16)

    def scores(g):
        for h in range(H_C):
            kcat = jnp.concatenate([head_rows(k_refs, g * G + t, h) for t in range(G)], axis=0)
            s_s[g, h * R:(h + 1) * R, :] = _dot_nt(q_ref[0, h * R:(h + 1) * R, :], kcat)

    def update(s, v_of_head):
        m_old = m_s[...]
        m_new = jnp.maximum(m_old, jnp.max(s, axis=-1, keepdims=True))
        alpha = jnp.exp2(m_old - m_new)
        p = jnp.exp2(s - m_new).astype(BF16)
        l_s[...] = alpha * l_s[...] + jnp.sum(p.astype(F32), axis=-1, keepdims=True)
        for h in range(H_C):
            rows = slice(h * R, (h + 1) * R)
            acc_s[rows, :] = alpha[rows] * acc_s[rows, :] + _dot(p[rows, :], v_of_head(h))
        m_s[...] = m_new

    def v_group(g):
        return lambda h: jnp.concatenate([head_rows(v_refs, g * G + t, h) for t in range(G)], axis=0)

    @pl.when(j < nj - 1)
    def _():
        for g in range(ngroups):
            scores(g)
        for g in range(ngroups):
            update(s_s[g], v_group(g))

    @pl.when(j == nj - 1)
    def _():
        for g in range(ngroups):
            scores(g)
        for g in range(ngroups):
            s = s_s[g]
            if g == ngroups - 1:
                s = s + bias_last_ref[...]
            update(s, v_group(g))
        for h in range(H_C):
            kh = kn_ref[0, :, h * DV_C:(h + 1) * DV_C]
            s_s[0, h * R:(h + 1) * R, 0:PAGE_SIZE] = _dot_nt(q_ref[0, h * R:(h + 1) * R, :], kh)
        update(s_s[0, :, 0:PAGE_SIZE] + bias_new_ref[...], lambda h: vn_ref[0, :, h * DV_C:(h + 1) * DV_C])
        n = acc_s[...] / l_s[...]
        for h in range(H_C):
            o = n[h * R:h * R + 8, :] - lam_ref[0] * n[h * R + 8:(h + 1) * R, :]
            o_ref[0, h * 8:(h + 1) * 8, :] = _rms(o, gain_ref[...]) * out_scale


def attn_sample(page_table, lam, qm, cache_k, cache_v, k_new, v_new, bias_last, bias_new, gain, out_scale):
    B = qm.shape[0]
    P = PAGES_PER_STEP
    n_pages = page_table.shape[1]
    nj = n_pages // P
    rows = H_C * ROWS_PER_HEAD
    page_rows = PAGE_SIZE * H_C

    def page_spec(p):
        return pl.BlockSpec((1, page_rows, DV_C), lambda b, j, pt, p=p: (pt[b, j * P + p], 0, 0))

    grid_spec = pltpu.PrefetchScalarGridSpec(
        num_scalar_prefetch=1,
        grid=(B, nj),
        in_specs=[
            pl.BlockSpec(memory_space=pltpu.SMEM),
            pl.BlockSpec((1, rows, DV_C), lambda b, j, pt: (b, 0, 0)),
            *[page_spec(p) for p in range(P)],
            *[page_spec(p) for p in range(P)],
            pl.BlockSpec((1, PAGE_SIZE, H_C * DV_C), lambda b, j, pt: (b, 0, 0)),
            pl.BlockSpec((1, PAGE_SIZE, H_C * DV_C), lambda b, j, pt: (b, 0, 0)),
            pl.BlockSpec((rows, PAGE_GROUP * PAGE_SIZE), lambda b, j, pt: (0, 0)),
            pl.BlockSpec((rows, PAGE_SIZE), lambda b, j, pt: (0, 0)),
            pl.BlockSpec((1, DV_C), lambda b, j, pt: (0, 0)),
        ],
        out_specs=pl.BlockSpec((1, H_C * 8, DV_C), lambda b, j, pt: (b, 0, 0)),
        scratch_shapes=[
            pltpu.VMEM((rows, 1), F32),
            pltpu.VMEM((rows, 1), F32),
            pltpu.VMEM((rows, DV_C), F32),
            pltpu.VMEM((P // PAGE_GROUP, rows, PAGE_GROUP * PAGE_SIZE), F32),
        ],
    )
    return pl.pallas_call(
        functools.partial(_attn_sample_kernel, out_scale=out_scale),
        grid_spec=grid_spec,
        out_shape=jax.ShapeDtypeStruct((B, H_C * 8, DV_C), F32),
        compiler_params=_cparams(("arbitrary", "arbitrary")),
        name="attn_sample",
    )(page_table, lam, qm, *([cache_k] * P), *([cache_v] * P), k_new, v_new, bias_last, bias_new, gain)


def _t5_bucket_table():
    n = np.arange(MAX_DIST + 1)
    max_exact = N_BUCKETS // 2
    nf = np.maximum(n, 1).astype(np.float32)
    large = max_exact + (np.log(nf / max_exact) / math.log(MAX_DIST / max_exact) * (N_BUCKETS - max_exact)).astype(np.int32)
    large = np.minimum(large, N_BUCKETS - 1)
    return np.where(n < max_exact, n, large).astype(np.int32)


def _rel_bias_minus_far(rel_bias, rel):
    tab = _near_bias_table(rel_bias)
    vals = tab[np.clip(rel, 0, MAX_DIST)]
    vals = jnp.where(jnp.asarray(rel >= 0)[..., None], vals, NEG_INF)
    return jnp.moveaxis(vals, -1, 0).astype(F32)


def _near_bias_table(rel_bias):
    tab = rel_bias[_t5_bucket_table()]
    return ((tab - tab[MAX_DIST][None, :]) * LOG2E).astype(F32)


def _bias_blocks_t(rel_bias):
    W = ATTN_STRIP
    H = rel_bias.shape[1]
    f = jnp.concatenate([_near_bias_table(rel_bias), jnp.zeros((W - MAX_DIST - 1, H), F32)], axis=0).T
    g0 = jnp.concatenate([f, jnp.full((H, W), NEG_INF, F32)], axis=1)
    g1 = jnp.concatenate([jnp.zeros((H, W), F32), f], axis=1)
    g = jnp.stack([g0, g1], axis=1)
    rep = jnp.tile(g, (1, 1, W))[:, :, :W * (2 * W - 1)].reshape(H, 2, W, 2 * W - 1)
    return rep[:, :, :, :W]


TM_PROMPT = 512
TM_FFN = 512
SAMPLE_PAD = 16
TQ = 2048


def kernel(x_prompt, x_sample, state_mlstm_C, state_mlstm_n, state_mlstm_m, state_pool, cache_k, cache_v, state_ffn_conv, page_table, norm_mix, norm_ffn, norm_final, w_in_e, b_gate_e, mlstm_gain, w_pool, pool_scale, w_out_e, w_in_o, lambda_q1, lambda_k1, lambda_q2, lambda_k2, subln_gain, rel_bias, w_out_o, w_up, conv_w, conv_b, w_down):
    Bp, Tp = x_prompt.shape[:2]
    Bs, Ts = x_sample.shape[:2]
    assert Bp == 1
    Ms = Bs * Ts
    xp = x_prompt.reshape(Tp, D_MODEL)
    xs = x_sample.reshape(Ms, D_MODEL)
    row = lambda a: a.reshape(1, -1)

    w_in = w_in_e[0]
    n_gate = 2 * H_A
    w_main = jnp.concatenate([w_in[:, :4 * D_A], w_in[:, 4 * D_A + n_gate:], w_in[:, 4 * D_A:4 * D_A + n_gate],
                              jnp.zeros((D_MODEL, 128 - n_gate), F32)], axis=1).astype(BF16)
    b_col = jnp.pad(b_gate_e[0], (0, 128 - n_gate)).reshape(1, 128)
    g_mix0 = row(norm_mix[0])
    wp_b = w_pool[0].astype(BF16)
    wo_e = w_out_e[0].astype(BF16)
    gain_e = row(mlstm_gain[0])
    ps_e = row(pool_scale[0])

    qkv_p, ogu_p, gc_p, gr_p = proj_even(xp, g_mix0, w_main, b_col, TM_PROMPT)
    zc = jnp.zeros((1, H_A, DK_A, DK_A), F32)
    zn = jnp.zeros((1, H_A, DK_A), F32)
    zm = jnp.zeros((1, 8, 128), F32)
    hh_p, C_p, n_p, m_p = mlstm(qkv_p[None], ogu_p[None], gc_p[None], gr_p[None], zc, zn, zm, gain_e,
                                MLSTM_CHUNK, MLSTM_CHUNK, MLSTM_CHUNKS_PER_STEP)
    pool_args_p = (hh_p[0], ogu_p, jnp.zeros((POOL_HDR, D_B), F32), wp_b, ps_e, wo_e)
    pool_p = ogu_p[Tp - POOL_BUF:, D_A:][None]

    L = SAMPLE_PAD
    qkv_s, ogu_s, gc_s, gr_s = proj_even(xs, g_mix0, w_main, b_col, Ms)
    pad_t = lambda a, n: jnp.pad(a.reshape(Bs, Ts, a.shape[-1]), ((0, 0), (0, n - Ts), (0, 0)))
    gr_s3 = jnp.pad(gr_s.reshape(8, Bs, Ts).transpose(1, 0, 2), ((0, 0), (0, 0), (0, L - Ts)))
    m0_s = jnp.broadcast_to(jnp.pad(state_mlstm_m[0], ((0, 0), (0, 8 - H_A)))[:, :, None], (Bs, 8, 128))
    ogu_s3 = pad_t(ogu_s, L)
    hh_s, C_s, n_s, m_s = mlstm(pad_t(qkv_s, L), ogu_s3, pad_t(gc_s, L), gr_s3,
                                state_mlstm_C[0], state_mlstm_n[0], m0_s, gain_e, L, Ts, 1)
    prev16 = jnp.pad(state_pool[0], ((0, 0), (POOL_HDR - POOL_BUF, 0), (0, 0)))
    xs = pool_out(hh_s, ogu_s3, prev16, wp_b, ps_e, wo_e, pad_t(xs, L), L, PAST_LEN)[:, :Ts].reshape(Ms, D_MODEL)
    pool_s = jnp.concatenate([state_pool[0], ogu_s[:, D_A:].reshape(Bs, Ts, D_B)], axis=1)[:, -POOL_BUF:]

    wu = w_up.astype(BF16)
    wd = w_down.astype(BF16)

    def run_ffn(l, xp, xs, final_norm, mix_p=None, mix_s=None, w_mix=None, pool_p=None):
        g = row(norm_ffn[l])
        cb = row(conv_b[l])
        gf = row(norm_final)
        zp = jnp.zeros((8, D_FF), F32)
        xp, st_p = ffn(xp, g, wu, conv_w[l], cb, wd, zp, zp, gf, TM_FFN, TM_FFN, True, final_norm, l, mix_p, w_mix,
                       pool_p)
        st = state_ffn_conv[l]
        z1 = jnp.zeros((Bs, 1, D_FF), F32)
        p1 = jnp.concatenate([st[:, 1:2], z1, z1, z1], axis=1).reshape(Ms, D_FF)
        p2 = jnp.concatenate([st[:, 0:1], st[:, 1:2], z1, z1], axis=1).reshape(Ms, D_FF)
        xs, a_s = ffn(xs, g, wu, conv_w[l], cb, wd, p1, p2, gf, Ms, Ts, False, final_norm, l, mix_s, w_mix)
        conv_p = st_p[8 - (CONV_W - 1):][None]
        conv_s = a_s.reshape(Bs, Ts, D_FF)[:, Ts - (CONV_W - 1):]
        return xp, xs, conv_p, conv_s

    xp, xs, conv_p0, conv_s0 = run_ffn(0, xp, xs, False, pool_p=pool_args_p)

    lam_init = 0.8 - 0.6 * math.exp(-0.3 * 1)
    lam = (jnp.exp(jnp.sum(lambda_q1[0] * lambda_k1[0])) - jnp.exp(jnp.sum(lambda_q2[0] * lambda_k2[0])) + lam_init).astype(F32).reshape(1)
    out_scale = 1.0 - lam_init
    g_mix1 = row(norm_mix[1])
    w_qkv = w_in_o[0].astype(BF16)
    wo_o = w_out_o[0].astype(BF16)
    gain_o = row(subln_gain[0])

    q2_p, kf_p, vf_p, kb_p, vt_p = proj_odd(xp, g_mix1, w_qkv, TM_PROMPT, True)
    o_p = attn_prompt(lam, q2_p, kb_p, vt_p, _bias_blocks_t(rel_bias), gain_o.reshape(DV_C, 1), TQ, out_scale)

    q2_s, kf_s, vf_s, kb_s, vb_s = proj_odd(xs, g_mix1, w_qkv, Ms, False)
    qm = q2_s.reshape(2, Bs, Ts, H_C, DV_C).transpose(1, 3, 0, 2, 4)
    qm = jnp.pad(qm, ((0, 0), (0, 0), (0, 0), (0, 8 - Ts), (0, 0)))
    qm = qm.reshape(Bs, H_C * ROWS_PER_HEAD, DV_C)
    tok = np.minimum(np.arange(8), Ts - 1)
    tok = np.tile(tok, 2)
    ccol = np.arange(PAGE_SIZE)
    rel_last = PAGE_SIZE + tok[:, None] - ccol[None, :]
    rel_new = np.where(ccol[None, :] < Ts, tok[:, None] - ccol[None, :], -1)
    bias_last = _rel_bias_minus_far(rel_bias, rel_last).reshape(H_C * ROWS_PER_HEAD, PAGE_SIZE)
    bias_last = jnp.pad(bias_last, ((0, 0), ((PAGE_GROUP - 1) * PAGE_SIZE, 0)))
    bias_new = _rel_bias_minus_far(rel_bias, rel_new).reshape(H_C * ROWS_PER_HEAD, PAGE_SIZE)
    n_phys = cache_k.shape[1]
    ck = cache_k[0].reshape(n_phys, PAGE_SIZE * H_C, DV_C)
    cv = cache_v[0].reshape(n_phys, PAGE_SIZE * H_C, DV_C)
    kn = jnp.pad(kb_s.reshape(Bs, Ts, D_MODEL), ((0, 0), (0, PAGE_SIZE - Ts), (0, 0)))
    vn = jnp.pad(vb_s.reshape(Bs, Ts, D_MODEL), ((0, 0), (0, PAGE_SIZE - Ts), (0, 0)))
    o_s = attn_sample(page_table, lam, qm, ck, cv, kn, vn, bias_last, bias_new, gain_o, out_scale)
    o_s = o_s.reshape(Bs, H_C, 8, DV_C)[:, :, :Ts].transpose(0, 2, 1, 3).reshape(Ms, D_MODEL).astype(BF16)

    yp, ys, conv_p1, conv_s1 = run_ffn(1, xp, xs, True, o_p, o_s, wo_o)

    y_prompt = yp.reshape(Bp, Tp, D_MODEL)
    y_sample = ys.reshape(Bs, Ts, D_MODEL)
    new_m_p = m_p[:, :H_A, 0]
    new_m_s = m_s[:, :H_A, 0]
    new_k_p = kf_p.reshape(1, Bp, Tp, H_C, DV_C)
    new_v_p = vf_p.reshape(1, Bp, Tp, H_C, DV_C)
    new_k_s = kf_s.reshape(1, Bs, Ts, H_C, DV_C)
    new_v_s = vf_s.reshape(1, Bs, Ts, H_C, DV_C)
    return (y_prompt, y_sample,
            C_p[None], n_p[None], new_m_p[None], pool_p[None], new_k_p, new_v_p,
            jnp.stack([conv_p0, conv_p1]),
            C_s[None], n_s[None], new_m_s[None], pool_s[None], new_k_s, new_v_s,
            jnp.stack([conv_s0, conv_s1]))
```

```python
import functools
import math

import numpy as np
import jax
import jax.numpy as jnp
from jax import lax
from jax.experimental import pallas as pl
from jax.experimental.pallas import tpu as pltpu

F32 = jnp.float32
BF16 = jnp.bfloat16
HIGHEST = lax.Precision.HIGHEST

D_MODEL = 1024
PAST_LEN = 16384
PAGE_SIZE = 128
D_A = 512
H_A = 4
DK_A = 128
MLSTM_CHUNK = 128
MLSTM_CHUNKS_PER_STEP = 8
D_B = 512
POOL_WINDOWS = (2, 4, 8, 16)
G_B = 128
POOL_BUF = 15
POOL_HDR = 16
H_C = 8
DC = 64
DV_C = 128
N_BUCKETS = 32
MAX_DIST = 128
LOG2E = math.log2(math.e)
SCORE_SCALE = DC ** -0.5 * LOG2E
D_FF = 2816
CONV_W = 3
EPS = 1e-6

VMEM_LIMIT = 56 * 1024 * 1024
NEG_INF = float("-inf")


def _cparams(sem):
    return pltpu.CompilerParams(dimension_semantics=sem, vmem_limit_bytes=VMEM_LIMIT)


def _const_spec(shape):
    nd = len(shape)
    return pl.BlockSpec(shape, lambda *_: (0,) * nd, pipeline_mode=pl.Buffered(1))


def _rms(x, g):
    return x * lax.rsqrt(jnp.mean(x * x, axis=-1, keepdims=True) + EPS) * g


def _dot(a, b):
    return jnp.dot(a, b, preferred_element_type=F32)


def _dot_nt(a, b):
    return lax.dot_general(a, b, (((1,), (1,)), ((), ())), preferred_element_type=F32)


def _dot_tn(a, b):
    return lax.dot_general(a, b, (((0,), (0,)), ((), ())), preferred_element_type=F32)


def _log_sigmoid(x):
    return jnp.minimum(x, 0.0) - jnp.log1p(jnp.exp(-jnp.abs(x)))


def _sigmoid(x):
    return 1.0 / (1.0 + jnp.exp(-x))


def _proj_even_kernel(x_ref, g_ref, w_ref, bc_ref, qkv_ref, ogu_ref, gc_ref, gr_ref):
    h = _rms(x_ref[...], g_ref[...])
    z = _dot(h.astype(BF16), w_ref[...])
    qkv_ref[:, 0:D_A] = z[:, 0:D_A].astype(BF16)
    qkv_ref[:, D_A:2 * D_A] = (z[:, D_A:2 * D_A] * (DK_A ** -0.5)).astype(BF16)
    qkv_ref[:, 2 * D_A:3 * D_A] = z[:, 2 * D_A:3 * D_A].astype(BF16)
    ogu_ref[...] = z[:, 3 * D_A:3 * D_A + D_A + D_B]
    gc = z[:, 3 * D_A + D_A + D_B:] + bc_ref[...]
    gc_ref[...] = gc
    gr_ref[...] = gc.T[0:8, :]


def proj_even(x, g, w_main, b_col, tm):
    M = x.shape[0]
    n_main = w_main.shape[1]
    return pl.pallas_call(
        _proj_even_kernel,
        grid=(M // tm,),
        in_specs=[
            pl.BlockSpec((tm, D_MODEL), lambda i: (i, 0)),
            _const_spec((1, D_MODEL)),
            _const_spec((D_MODEL, n_main)),
            _const_spec((1, 128)),
        ],
        out_specs=[
            pl.BlockSpec((tm, 3 * D_A), lambda i: (i, 0)),
            pl.BlockSpec((tm, D_A + D_B), lambda i: (i, 0)),
            pl.BlockSpec((tm, 128), lambda i: (i, 0)),
            pl.BlockSpec((8, tm), lambda i: (0, i)),
        ],
        out_shape=[
            jax.ShapeDtypeStruct((M, 3 * D_A), BF16),
            jax.ShapeDtypeStruct((M, D_A + D_B), F32),
            jax.ShapeDtypeStruct((M, 128), F32),
            jax.ShapeDtypeStruct((8, M), F32),
        ],
        compiler_params=_cparams(("arbitrary",)),
        name="proj_even",
    )(x, g, w_main, b_col)


def _mlstm_kernel(qkv_ref, og_ref, gc_ref, gr_ref, c0_ref, n0_ref, m0_ref, gain_ref,
                  hh_ref, c_out_ref, n_out_ref, m_out_ref, c_s, n_s, m_s, *, L, valid, chunks):
    c = pl.program_id(1)

    @pl.when(c == 0)
    def _():
        c_s[...] = c0_ref[0]
        n_s[...] = n0_ref[0]
        m_s[...] = m0_ref[0]

    row = lax.broadcasted_iota(jnp.int32, (L, L), 0)
    col = lax.broadcasted_iota(jnp.int32, (L, L), 1)
    tri = (col <= row).astype(F32)
    mask = (col <= row) & (col < valid)
    rvalid = lax.broadcasted_iota(jnp.int32, (L, 1), 0) < valid
    cvalid = lax.broadcasted_iota(jnp.int32, (1, L), 1) < valid

    c_st = [c_s[h] for h in range(H_A)]
    n_st = [n_s[h:h + 1, :] for h in range(H_A)]
    m_st = [m_s[h:h + 1, 0:1] for h in range(H_A)]

    for ci in range(chunks):
        rows = slice(ci * L, (ci + 1) * L)
        gcol = gc_ref[0, rows, :]
        grow = gr_ref[0, :, rows]
        lf_col = jnp.where(rvalid, _log_sigmoid(gcol), 0.0)
        lf_row = jnp.where(cvalid, _log_sigmoid(grow), 0.0)
        b_col_all = jnp.dot(tri, lf_col, precision=HIGHEST, preferred_element_type=F32)
        b_row_all = lax.dot_general(lf_row, tri, (((1,), (1,)), ((), ())), precision=HIGHEST,
                                    preferred_element_type=F32)
        for h in range(H_A):
            bc = b_col_all[:, H_A + h:H_A + h + 1]
            br = b_row_all[H_A + h:H_A + h + 1, :]
            igc = gcol[:, h:h + 1]
            igr = grow[h:h + 1, :]
            m0 = m_st[h]
            logd = jnp.where(mask, bc - br + igr, NEG_INF)
            log_inter = bc + m0
            m_t = jnp.maximum(log_inter, jnp.max(logd, axis=-1, keepdims=True))
            dm = jnp.exp(logd - m_t)
            w_inter = jnp.exp(log_inter - m_t)
            q = qkv_ref[0, rows, h * DK_A:(h + 1) * DK_A]
            k = qkv_ref[0, rows, D_A + h * DK_A:D_A + (h + 1) * DK_A]
            v = qkv_ref[0, rows, 2 * D_A + h * DK_A:2 * D_A + (h + 1) * DK_A]
            s = _dot_nt(q, k) * dm
            c_old = c_st[h]
            n_old = n_st[h]
            v_ext = jnp.concatenate([v, jnp.ones((L, DK_A), BF16)], axis=1)
            c_ext = jnp.concatenate([c_old, jnp.broadcast_to(n_old, (DK_A, DK_A))], axis=0).astype(BF16)
            both = _dot(s.astype(BF16), v_ext) + w_inter * _dot_nt(q, c_ext)
            denom = jnp.maximum(jnp.abs(both[:, DK_A:]), jnp.exp(-m_t))
            hh = both[:, :DK_A] / denom
            y = _rms(hh, gain_ref[:, h * DK_A:(h + 1) * DK_A]) * _sigmoid(og_ref[0, rows, h * DK_A:(h + 1) * DK_A])
            hh_ref[0, rows, h * DK_A:(h + 1) * DK_A] = y.astype(BF16)
            m_new = m_t[valid - 1:valid, :]
            b_last = bc[valid - 1:valid, :]
            w_s = jnp.where(rvalid, jnp.exp(b_last - bc + igc - m_new), 0.0)
            decay = jnp.exp(b_last + m0 - m_new)
            kf = k.astype(F32)
            vw = (v.astype(F32) * w_s).astype(BF16)
            c_st[h] = decay * c_old + _dot_tn(vw, k)
            n_st[h] = decay * n_old + jnp.sum(kf * w_s, axis=0, keepdims=True)
            m_st[h] = m_new

    for h in range(H_A):
        c_s[h] = c_st[h]
        n_s[h:h + 1, :] = n_st[h]
        m_s[h:h + 1, :] = jnp.broadcast_to(m_st[h], (1, 128))

    @pl.when(c == pl.num_programs(1) - 1)
    def _():
        c_out_ref[0] = c_s[...]
        n_out_ref[0] = n_s[...]
        m_out_ref[0] = m_s[...]


def mlstm(qkv, ogu, gc, gr, c0, n0, m0, gain, L, valid, chunks):
    B, T = qkv.shape[:2]
    LB = L * chunks
    nc = T // LB
    return pl.pallas_call(
        functools.partial(_mlstm_kernel, L=L, valid=valid, chunks=chunks),
        grid=(B, nc),
        in_specs=[
            pl.BlockSpec((1, LB, 3 * D_A), lambda b, c: (b, c, 0)),
            pl.BlockSpec((1, LB, D_A), lambda b, c: (b, c, 0)),
            pl.BlockSpec((1, LB, 128), lambda b, c: (b, c, 0)),
            pl.BlockSpec((1, 8, LB), lambda b, c: (b, 0, c)),
            pl.BlockSpec((1, H_A, DK_A, DK_A), lambda b, c: (b, 0, 0, 0)),
            pl.BlockSpec((1, H_A, DK_A), lambda b, c: (b, 0, 0)),
            pl.BlockSpec((1, 8, 128), lambda b, c: (b, 0, 0)),
            pl.BlockSpec((1, D_A), lambda b, c: (0, 0)),
        ],
        out_specs=[
            pl.BlockSpec((1, LB, D_A), lambda b, c: (b, c, 0)),
            pl.BlockSpec((1, H_A, DK_A, DK_A), lambda b, c: (b, 0, 0, 0)),
            pl.BlockSpec((1, H_A, DK_A), lambda b, c: (b, 0, 0)),
            pl.BlockSpec((1, 8, 128), lambda b, c: (b, 0, 0)),
        ],
        out_shape=[
            jax.ShapeDtypeStruct((B, T, D_A), BF16),
            jax.ShapeDtypeStruct((B, H_A, DK_A, DK_A), F32),
            jax.ShapeDtypeStruct((B, H_A, DK_A), F32),
            jax.ShapeDtypeStruct((B, 8, 128), F32),
        ],
        scratch_shapes=[
            pltpu.VMEM((H_A, DK_A, DK_A), F32),
            pltpu.VMEM((H_A, DK_A), F32),
            pltpu.VMEM((8, 128), F32),
        ],
        compiler_params=_cparams(("arbitrary", "arbitrary")),
        name="mlstm",
    )(qkv, ogu, gc, gr, c0, n0, m0, gain)


POOL_LEAD = 8
assert POOL_WINDOWS == tuple(2 ** (g + 1) for g in range(len(POOL_WINDOWS)))


def _pool_scratch(tm):
    rows = POOL_LEAD + POOL_HDR + tm
    return [pltpu.VMEM((rows, D_B), F32), pltpu.VMEM((rows, D_B - G_B), F32), pltpu.VMEM((rows, D_B - 2 * G_B), F32)]


def _even_mixer_out(hh, u, prev, wp_ref, ps_ref, wo_ref, bufs, t, *, tm, nt, pos0):
    e_s, pa_s, pb_s = bufs
    H = POOL_HDR
    lo = POOL_LEAD
    top = lo + H + tm
    out0 = H

    @pl.when(t == 0)
    def _():
        e_s[0:lo, :] = jnp.zeros((lo, e_s.shape[1]), F32)
        pa_s[0:lo, :] = jnp.zeros((lo, pa_s.shape[1]), F32)
        pb_s[0:lo, :] = jnp.zeros((lo, pb_s.shape[1]), F32)
        e_s[lo:lo + H, :] = prev

    if nt > 1:
        @pl.when(t > 0)
        def _():
            e_s[lo:lo + H, :] = e_s[lo + tm:lo + tm + H, :]

    e_s[lo + H:top, :] = u
    pos = pos0 + t * tm + lax.broadcasted_iota(jnp.int32, (tm, 1), 0)
    ys = []
    src, dst = e_s, pa_s
    for g, w in enumerate(POOL_WINDOWS):
        ncol = D_B - g * G_B
        shift = w // 2
        sums = src[lo:top, 0:ncol] + src[lo - shift:top - shift, 0:ncol]
        if g + 1 < len(POOL_WINDOWS):
            dst[lo:top, 0:ncol - G_B] = sums[:, G_B:]
        cur = e_s[lo + H:top, g * G_B:(g + 1) * G_B]
        cnt = jnp.minimum(pos + 1, w).astype(F32)
        pooled = sums[out0:, 0:G_B] / cnt - cur
        ys.append(_dot(pooled.astype(BF16), wp_ref[g]))
        src, dst = dst, (pb_s if dst is pa_s else pa_s)
    yb = jnp.concatenate(ys, axis=-1) * ps_ref[...]
    return _dot(hh, wo_ref[0:D_A, :]) + _dot(yb.astype(BF16), wo_ref[D_A:, :])


def _pool_out_kernel(hh_ref, u_ref, prev_ref, wp_ref, ps_ref, wo_ref, x_ref, o_ref, *bufs, tm, nt, pos0):
    y = _even_mixer_out(hh_ref[0], u_ref[0], prev_ref[0], wp_ref, ps_ref, wo_ref, bufs, pl.program_id(1),
                        tm=tm, nt=nt, pos0=pos0)
    o_ref[0] = x_ref[0] + y


def pool_out(hh, ogu, prev16, w_pool, pool_scale, w_out, x, tm, pos0):
    B, T = x.shape[:2]
    nt = T // tm
    return pl.pallas_call(
        functools.partial(_pool_out_kernel, tm=tm, nt=nt, pos0=pos0),
        grid=(B, nt),
        in_specs=[
            pl.BlockSpec((1, tm, D_A), lambda b, t: (b, t, 0)),
            pl.BlockSpec((1, tm, D_B), lambda b, t: (b, t, 1)),
            pl.BlockSpec((1, POOL_HDR, D_B), lambda b, t: (b, 0, 0)),
            _const_spec((len(POOL_WINDOWS), G_B, G_B)),
            _const_spec((1, D_B)),
            _const_spec((D_A + D_B, D_MODEL)),
            pl.BlockSpec((1, tm, D_MODEL), lambda b, t: (b, t, 0)),
        ],
        out_specs=pl.BlockSpec((1, tm, D_MODEL), lambda b, t: (b, t, 0)),
        out_shape=jax.ShapeDtypeStruct((B, T, D_MODEL), F32),
        scratch_shapes=_pool_scratch(tm),
        compiler_params=_cparams(("arbitrary", "arbitrary")),
        name="pool_out",
    )(hh, ogu, prev16, w_pool, pool_scale, w_out, x)


def _gelu_tanh(y):
    return 0.5 * y * (1.0 + jnp.tanh(math.sqrt(2.0 / math.pi) * (y + 0.044715 * (y * y * y))))


FFN_CHUNKS = (1024, 1024, 768)
assert sum(FFN_CHUNKS) == D_FF


def _ffn_kernel(x_ref, g_ref, wup_ref, cw_ref, cb_ref, wdn_ref, p1_ref, p2_ref, gf_ref, *refs,
                tm, nt, seq_len, carried, final_norm, mixer):
    i = pl.program_id(0)
    x = x_ref[...]
    if mixer == "dense":
        mix_ref, wmix_ref, o_ref, st_ref, carry_s, a_s, g_s = refs
        x = x + _dot(mix_ref[...], wmix_ref[...])
    elif mixer == "pool":
        hh_ref, u_ref, prev_ref, wp_ref, ps_ref, wmix_ref, o_ref, st_ref, carry_s, a_s, g_s, *pool_bufs = refs
        x = x + _even_mixer_out(hh_ref[...], u_ref[...], prev_ref[...], wp_ref, ps_ref, wmix_ref, pool_bufs, i,
                                tm=tm, nt=nt, pos0=0)
    else:
        o_ref, st_ref, carry_s, a_s, g_s = refs
    offs = [sum(FFN_CHUNKS[:c]) for c in range(len(FFN_CHUNKS) + 1)]
    nch = len(FFN_CHUNKS)
    h = _rms(x, g_ref[...]).astype(BF16)
    t = lax.broadcasted_iota(jnp.int32, (tm, 1), 0) % seq_len

    if carried:
        @pl.when(i == 0)
        def _():
            carry_s[...] = jnp.zeros_like(carry_s)

    def up(c):
        w = FFN_CHUNKS[c]
        a_s[c % 2, :, 0:w] = _dot(h, wup_ref[:, offs[c]:offs[c + 1]])
        g_s[c % 2, :, 0:w] = _dot(h, wup_ref[:, D_FF + offs[c]:D_FF + offs[c + 1]])

    def act_down(c):
        cols = slice(offs[c], offs[c + 1])
        w = FFN_CHUNKS[c]
        a = a_s[c % 2, :, 0:w]
        s1 = jnp.where(t >= 1, pltpu.roll(a, 1, 0), 0.0)
        s2 = jnp.where(t >= 2, pltpu.roll(a, 2, 0), 0.0)
        if carried:
            prev0 = carry_s[6:7, cols]
            prev1 = carry_s[7:8, cols]
            s1 = s1 + jnp.where(t == 0, prev1, 0.0)
            s2 = s2 + jnp.where(t == 0, prev0, 0.0) + jnp.where(t == 1, prev1, 0.0)
            carry_s[:, cols] = a[tm - 8:tm, :]
            st_ref[:, cols] = a[tm - 8:tm, :]
        else:
            s1 = s1 + p1_ref[:, cols]
            s2 = s2 + p2_ref[:, cols]
            st_ref[:, cols] = a
        y = cb_ref[:, cols] + cw_ref[0:1, cols] * s2 + cw_ref[1:2, cols] * s1 + cw_ref[2:3, cols] * a
        act = (_gelu_tanh(y) * g_s[c % 2, :, 0:w]).astype(BF16)
        return _dot(act, wdn_ref[cols, :])

    up(0)
    for c in range(nch):
        if c + 1 < nch:
            up(c + 1)
        d = act_down(c)
        if c == 0:
            o_ref[...] = x + d
        else:
            o_ref[...] += d
    if final_norm:
        o_ref[...] = _rms(o_ref[...], gf_ref[...])


def _layer_spec(shape, layer):
    nd = len(shape)
    return pl.BlockSpec((None,) + tuple(shape), lambda *_: (layer,) + (0,) * nd, pipeline_mode=pl.Buffered(1))


def ffn(x, g, w_up, conv_w, conv_b, w_down, p1, p2, g_final, tm, seq_len, carried, final_norm, layer,
        mix=None, w_mix=None, pool=None):
    M = x.shape[0]
    nt = M // tm
    st_rows = 8 if carried else tm
    st_total = 8 if carried else M
    row_spec = lambda n: pl.BlockSpec((tm, n), lambda i: (i, 0))
    p_spec = _const_spec((8, D_FF)) if carried else row_spec(D_FF)
    mixer, mix_specs, mix_args, mix_scratch = None, [], [], []
    if mix is not None:
        mixer = "dense"
        mix_specs = [row_spec(D_MODEL), _const_spec((D_MODEL, D_MODEL))]
        mix_args = [mix, w_mix]
    elif pool is not None:
        assert carried
        mixer = "pool"
        mix_specs = [
            row_spec(D_A),
            pl.BlockSpec((tm, D_B), lambda i: (i, 1)),
            _const_spec((POOL_HDR, D_B)),
            _const_spec((len(POOL_WINDOWS), G_B, G_B)),
            _const_spec((1, D_B)),
            _const_spec((D_A + D_B, D_MODEL)),
        ]
        mix_args = list(pool)
        mix_scratch = _pool_scratch(tm)
    return pl.pallas_call(
        functools.partial(_ffn_kernel, tm=tm, nt=nt, seq_len=seq_len, carried=carried, final_norm=final_norm,
                          mixer=mixer),
        grid=(nt,),
        in_specs=[
            row_spec(D_MODEL),
            _const_spec((1, D_MODEL)),
            _layer_spec((D_MODEL, 2 * D_FF), layer),
            _const_spec((CONV_W, D_FF)),
            _const_spec((1, D_FF)),
            _layer_spec((D_FF, D_MODEL), layer),
            p_spec,
            p_spec,
            _const_spec((1, D_MODEL)),
            *mix_specs,
        ],
        out_specs=[
            row_spec(D_MODEL),
            pl.BlockSpec((st_rows, D_FF), (lambda i: (0, 0)) if carried else (lambda i: (i, 0))),
        ],
        out_shape=[
            jax.ShapeDtypeStruct((M, D_MODEL), F32),
            jax.ShapeDtypeStruct((st_total, D_FF), F32),
        ],
        scratch_shapes=[
            pltpu.VMEM((8, D_FF), F32),
            pltpu.VMEM((2, tm, max(FFN_CHUNKS)), F32),
            pltpu.VMEM((2, tm, max(FFN_CHUNKS)), F32),
            *mix_scratch,
        ],
        compiler_params=_cparams(("arbitrary",)),
        name="ffn",
    )(x, g, w_up, conv_w, conv_b, w_down, p1, p2, g_final, *mix_args)


def _proj_odd_kernel(x_ref, g_ref, w_ref, q_ref, kf_ref, vf_ref, kb_ref, vb_ref, *, v_transposed):
    tm = x_ref.shape[0]
    h = _rms(x_ref[...], g_ref[...]).astype(BF16)
    q = _dot(h, w_ref[:, 0:D_MODEL]) * SCORE_SCALE
    lane = lax.broadcasted_iota(jnp.int32, q.shape, 1) % DV_C
    q_ref[0] = jnp.where(lane < DC, q, 0.0).astype(BF16)
    q_ref[1] = jnp.where(lane >= DC, q, 0.0).astype(BF16)
    k = _dot(h, w_ref[:, D_MODEL:2 * D_MODEL])
    kf_ref[...] = k
    kb_ref[...] = k.astype(BF16)
    v = _dot(h, w_ref[:, 2 * D_MODEL:])
    vf_ref[...] = v
    if v_transposed:
        rows = DV_C + ATTN_ONES_ROWS
        for hd in range(H_C):
            vb_ref[hd * rows:hd * rows + DV_C, :] = v[:, hd * DV_C:(hd + 1) * DV_C].T.astype(BF16)
            vb_ref[hd * rows + DV_C:(hd + 1) * rows, :] = jnp.ones((ATTN_ONES_ROWS, tm), BF16)
    else:
        vb_ref[...] = v.astype(BF16)


def proj_odd(x, g, w, tm, v_transposed):
    M = x.shape[0]
    vt_rows = H_C * (DV_C + ATTN_ONES_ROWS)
    row_spec = pl.BlockSpec((tm, D_MODEL), lambda i: (i, 0))
    return pl.pallas_call(
        functools.partial(_proj_odd_kernel, v_transposed=v_transposed),
        grid=(M // tm,),
        in_specs=[row_spec, _const_spec((1, D_MODEL)), _const_spec((D_MODEL, 3 * D_MODEL))],
        out_specs=[pl.BlockSpec((2, tm, D_MODEL), lambda i: (0, i, 0))] + [row_spec] * 3
        + [pl.BlockSpec((vt_rows, tm), lambda i: (0, i)) if v_transposed else row_spec],
        out_shape=[
            jax.ShapeDtypeStruct((2, M, D_MODEL), BF16),
            jax.ShapeDtypeStruct((M, D_MODEL), F32),
            jax.ShapeDtypeStruct((M, D_MODEL), F32),
            jax.ShapeDtypeStruct((M, D_MODEL), BF16),
            jax.ShapeDtypeStruct((vt_rows, M) if v_transposed else (M, D_MODEL), BF16),
        ],
        compiler_params=_cparams(("arbitrary",)),
        name="proj_odd",
    )(x, g, w)


ATTN_STRIP = 256
ATTN_KEY_CHUNK = 512
ATTN_ONES_ROWS = 16


def _attn_prompt_kernel(it_ref, jt_ref, lam_ref, q_ref, k_ref, vt_ref, bias_ref, gain_ref, o_ref,
                        m_s, acc_s, s_s, *, tq, out_scale):
    i = it_ref[pl.program_id(1)]
    j = jt_ref[pl.program_id(1)]
    W = ATTN_STRIP
    KC = ATTN_KEY_CHUNK
    nstrip = tq // W

    @pl.when(j == 0)
    def _():
        m_s[...] = jnp.full_like(m_s, NEG_INF)
        acc_s[...] = jnp.zeros_like(acc_s)

    def tile(kind):
        strips = [(mp, rb) for mp in range(2) for rb in range(nstrip)]

        def nkeys(rb):
            return (rb + 1) * W if kind == 0 else tq

        def key_chunks(rb):
            nk = nkeys(rb)
            return [(lo, min(lo + KC, nk)) for lo in range(0, nk, KC)]

        def scores(idx):
            mp, rb = strips[idx]
            qs = q_ref[mp, rb * W:(rb + 1) * W, :]
            bounds = key_chunks(rb)
            chunks = [_dot_nt(k_ref[lo:hi, :], qs) for lo, hi in bounds]

            def add_bias(key_block, b):
                b_lo, b_hi = key_block * W, (key_block + 1) * W
                for c, (lo, hi) in enumerate(bounds):
                    o_lo, o_hi = max(lo, b_lo), min(hi, b_hi)
                    if o_lo >= o_hi:
                        continue
                    parts = [chunks[c][0:o_lo - lo], chunks[c][o_lo - lo:o_hi - lo] + b[o_lo - b_lo:o_hi - b_lo, :],
                             chunks[c][o_hi - lo:hi - lo]]
                    parts = [p for p in parts if p.shape[0] > 0]
                    chunks[c] = jnp.concatenate(parts, axis=0) if len(parts) > 1 else parts[0]

            if kind == 0:
                add_bias(rb, bias_ref[0, 0])
                if rb >= 1:
                    add_bias(rb - 1, bias_ref[0, 1])
            elif kind == 1 and rb == 0:
                add_bias(nstrip - 1, bias_ref[0, 1])
            for (lo, hi), s in zip(bounds, chunks):
                s_s[idx % 2, lo:hi, :] = s

        def consume(idx):
            mp, rb = strips[idx]
            nchunks = len(key_chunks(rb))
            cols = slice(mp * tq + rb * W, mp * tq + (rb + 1) * W)
            chunks = [s_s[idx % 2, lo:hi, :] for lo, hi in key_chunks(rb)]
            m_old = m_s[:, cols]
            m_new = m_old
            for s in chunks:
                m_new = jnp.maximum(m_new, jnp.max(s, axis=0, keepdims=True))
            alpha = jnp.exp2(m_old - m_new)
            ps = [jnp.exp2(s - m_new).astype(BF16) for s in chunks]
            pcat = jnp.concatenate(ps, axis=0) if nchunks > 1 else ps[0]
            acc_s[:, cols] = alpha * acc_s[:, cols] + _dot(vt_ref[:, 0:nkeys(rb)], pcat)
            m_s[:, cols] = m_new

        scores(0)
        for idx in range(len(strips)):
            if idx + 1 < len(strips):
                scores(idx + 1)
            consume(idx)

    @pl.when(j < i - 1)
    def _():
        tile(2)

    @pl.when(j == i - 1)
    def _():
        tile(1)

    @pl.when(j == i)
    def _():
        tile(0)
        n = acc_s[0:DV_C, :] / acc_s[DV_C:DV_C + 1, :]
        o = n[:, 0:tq] - lam_ref[0] * n[:, tq:]
        o = o * lax.rsqrt(jnp.mean(o * o, axis=0, keepdims=True) + EPS) * gain_ref[...] * out_scale
        o_ref[...] = o.T.astype(BF16)


def attn_prompt(lam, q2, k, vt, bias, gain_col, tq, out_scale):
    T = k.shape[0]
    nq = T // tq
    pairs = [(i, j) for i in range(nq) for j in range(i + 1)]
    itab = jnp.asarray(np.array([p[0] for p in pairs], np.int32))
    jtab = jnp.asarray(np.array([p[1] for p in pairs], np.int32))
    grid_spec = pltpu.PrefetchScalarGridSpec(
        num_scalar_prefetch=2,
        grid=(H_C, len(pairs)),
        in_specs=[
            pl.BlockSpec(memory_space=pltpu.SMEM),
            pl.BlockSpec((2, tq, DV_C), lambda h, p, it, jt: (0, it[p], h)),
            pl.BlockSpec((tq, DV_C), lambda h, p, it, jt: (jt[p], h)),
            pl.BlockSpec((DV_C + ATTN_ONES_ROWS, tq), lambda h, p, it, jt: (h, jt[p])),
            pl.BlockSpec((1, 2, ATTN_STRIP, ATTN_STRIP), lambda h, p, it, jt: (h, 0, 0, 0)),
            pl.BlockSpec((DV_C, 1), lambda h, p, it, jt: (0, 0)),
        ],
        out_specs=pl.BlockSpec((tq, DV_C), lambda h, p, it, jt: (it[p], h)),
        scratch_shapes=[
            pltpu.VMEM((1, 2 * tq), F32),
            pltpu.VMEM((DV_C + ATTN_ONES_ROWS, 2 * tq), F32),
            pltpu.VMEM((2, tq, ATTN_STRIP), F32),
        ],
    )
    return pl.pallas_call(
        functools.partial(_attn_prompt_kernel, tq=tq, out_scale=out_scale),
        grid_spec=grid_spec,
        out_shape=jax.ShapeDtypeStruct((T, H_C * DV_C), BF16),
        compiler_params=_cparams(("arbitrary", "arbitrary")),
        name="attn_prompt",
    )(itab, jtab, lam, q2, k, vt, bias, gain_col)


PAGES_PER_STEP = 16
PAGE_GROUP = 2
ROWS_PER_HEAD = 16


def _attn_sample_kernel(pt_ref, lam_ref, q_ref, *refs, out_scale):
    P = PAGES_PER_STEP
    G = PAGE_GROUP
    R = ROWS_PER_HEAD
    ngroups = P // G
    k_refs = refs[0:P]
    v_refs = refs[P:2 * P]
    kn_ref, vn_ref, bias_last_ref, bias_new_ref, gain_ref, o_ref, m_s, l_s, acc_s, s_s = refs[2 * P:]
    j = pl.program_id(1)
    nj = pl.num_programs(1)

    @pl.when(j == 0)
    def _():
        m_s[...] = jnp.full_like(m_s, NEG_INF)
        l_s[...] = jnp.zeros_like(l_s)
        acc_s[...] = jnp.zeros_like(acc_s)

    def head_rows(page_refs, p, h):
        return page_refs[p][0, pl.ds(h, PAGE_SIZE, stride=H_C), :].astype(BF16)

    def scores(g):
        for h in range(H_C):
            kcat = jnp.concatenate([head_rows(k_refs, g * G + t, h) for t in range(G)], axis=0)
            s_s[g, h * R:(h + 1) * R, :] = _dot_nt(q_ref[0, h * R:(h + 1) * R, :], kcat)

    def update(s, v_of_head):
        m_old = m_s[...]
        m_new = jnp.maximum(m_old, jnp.max(s, axis=-1, keepdims=True))
        alpha = jnp.exp2(m_old - m_new)
        p = jnp.exp2(s - m_new).astype(BF16)
        l_s[...] = alpha * l_s[...] + jnp.sum(p.astype(F32), axis=-1, keepdims=True)
        for h in range(H_C):
            rows = slice(h * R, (h + 1) * R)
            acc_s[rows, :] = alpha[rows] * acc_s[rows, :] + _dot(p[rows, :], v_of_head(h))
        m_s[...] = m_new

    def v_group(g):
        return lambda h: jnp.concatenate([head_rows(v_refs, g * G + t, h) for t in range(G)], axis=0)

    @pl.when(j < nj - 1)
    def _():
        for g in range(ngroups):
            scores(g)
        for g in range(ngroups):
            update(s_s[g], v_group(g))

    @pl.when(j == nj - 1)
    def _():
        for g in range(ngroups):
            scores(g)
        for g in range(ngroups):
            s = s_s[g]
            if g == ngroups - 1:
                s = s + bias_last_ref[...]
            update(s, v_group(g))
        for h in range(H_C):
            kh = kn_ref[0, :, h * DV_C:(h + 1) * DV_C]
            s_s[0, h * R:(h + 1) * R, 0:PAGE_SIZE] = _dot_nt(q_ref[0, h * R:(h + 1) * R, :], kh)
        update(s_s[0, :, 0:PAGE_SIZE] + bias_new_ref[...], lambda h: vn_ref[0, :, h * DV_C:(h + 1) * DV_C])
        n = acc_s[...] / l_s[...]
        for h in range(H_C):
            o = n[h * R:h * R + 8, :] - lam_ref[0] * n[h * R + 8:(h + 1) * R, :]
            o_ref[0, h * 8:(h + 1) * 8, :] = _rms(o, gain_ref[...]) * out_scale


def attn_sample(page_table, lam, qm, cache_k, cache_v, k_new, v_new, bias_last, bias_new, gain, out_scale):
    B = qm.shape[0]
    P = PAGES_PER_STEP
    n_pages = page_table.shape[1]
    nj = n_pages // P
    rows = H_C * ROWS_PER_HEAD
    page_rows = PAGE_SIZE * H_C

    def page_spec(p):
        return pl.BlockSpec((1, page_rows, DV_C), lambda b, j, pt, p=p: (pt[b, j * P + p], 0, 0))

    grid_spec = pltpu.PrefetchScalarGridSpec(
        num_scalar_prefetch=1,
        grid=(B, nj),
        in_specs=[
            pl.BlockSpec(memory_space=pltpu.SMEM),
            pl.BlockSpec((1, rows, DV_C), lambda b, j, pt: (b, 0, 0)),
            *[page_spec(p) for p in range(P)],
            *[page_spec(p) for p in range(P)],
            pl.BlockSpec((1, PAGE_SIZE, H_C * DV_C), lambda b, j, pt: (b, 0, 0)),
            pl.BlockSpec((1, PAGE_SIZE, H_C * DV_C), lambda b, j, pt: (b, 0, 0)),
            pl.BlockSpec((rows, PAGE_GROUP * PAGE_SIZE), lambda b, j, pt: (0, 0)),
            pl.BlockSpec((rows, PAGE_SIZE), lambda b, j, pt: (0, 0)),
            pl.BlockSpec((1, DV_C), lambda b, j, pt: (0, 0)),
        ],
        out_specs=pl.BlockSpec((1, H_C * 8, DV_C), lambda b, j, pt: (b, 0, 0)),
        scratch_shapes=[
            pltpu.VMEM((rows, 1), F32),
            pltpu.VMEM((rows, 1), F32),
            pltpu.VMEM((rows, DV_C), F32),
            pltpu.VMEM((P // PAGE_GROUP, rows, PAGE_GROUP * PAGE_SIZE), F32),
        ],
    )
    return pl.pallas_call(
        functools.partial(_attn_sample_kernel, out_scale=out_scale),
        grid_spec=grid_spec,
        out_shape=jax.ShapeDtypeStruct((B, H_C * 8, DV_C), F32),
        compiler_params=_cparams(("arbitrary", "arbitrary")),
        name="attn_sample",
    )(page_table, lam, qm, *([cache_k] * P), *([cache_v] * P), k_new, v_new, bias_last, bias_new, gain)


def _t5_bucket_table():
    n = np.arange(MAX_DIST + 1)
    max_exact = N_BUCKETS // 2
    nf = np.maximum(n, 1).astype(np.float32)
    large = max_exact + (np.log(nf / max_exact) / math.log(MAX_DIST / max_exact) * (N_BUCKETS - max_exact)).astype(np.int32)
    large = np.minimum(large, N_BUCKETS - 1)
    return np.where(n < max_exact, n, large).astype(np.int32)


def _rel_bias_minus_far(rel_bias, rel):
    tab = _near_bias_table(rel_bias)
    vals = tab[np.clip(rel, 0, MAX_DIST)]
    vals = jnp.where(jnp.asarray(rel >= 0)[..., None], vals, NEG_INF)
    return jnp.moveaxis(vals, -1, 0).astype(F32)


def _near_bias_table(rel_bias):
    tab = rel_bias[_t5_bucket_table()]
    return ((tab - tab[MAX_DIST][None, :]) * LOG2E).astype(F32)


def _bias_blocks_t(rel_bias):
    W = ATTN_STRIP
    H = rel_bias.shape[1]
    f = jnp.concatenate([_near_bias_table(rel_bias), jnp.zeros((W - MAX_DIST - 1, H), F32)], axis=0).T
    g0 = jnp.concatenate([f, jnp.full((H, W), NEG_INF, F32)], axis=1)
    g1 = jnp.concatenate([jnp.zeros((H, W), F32), f], axis=1)
    g = jnp.stack([g0, g1], axis=1)
    rep = jnp.tile(g, (1, 1, W))[:, :, :W * (2 * W - 1)].reshape(H, 2, W, 2 * W - 1)
    return rep[:, :, :, :W]


TM_PROMPT = 512
TM_FFN = 512
SAMPLE_PAD = 16
TQ = 2048


def kernel(x_prompt, x_sample, state_mlstm_C, state_mlstm_n, state_mlstm_m, state_pool, cache_k, cache_v, state_ffn_conv, page_table, norm_mix, norm_ffn, norm_final, w_in_e, b_gate_e, mlstm_gain, w_pool, pool_scale, w_out_e, w_in_o, lambda_q1, lambda_k1, lambda_q2, lambda_k2, subln_gain, rel_bias, w_out_o, w_up, conv_w, conv_b, w_down):
    Bp, Tp = x_prompt.shape[:2]
    Bs, Ts = x_sample.shape[:2]
    assert Bp == 1
    Ms = Bs * Ts
    xp = x_prompt.reshape(Tp, D_MODEL)
    xs = x_sample.reshape(Ms, D_MODEL)
    row = lambda a: a.reshape(1, -1)

    w_in = w_in_e[0]
    n_gate = 2 * H_A
    w_main = jnp.concatenate([w_in[:, :4 * D_A], w_in[:, 4 * D_A + n_gate:], w_in[:, 4 * D_A:4 * D_A + n_gate],
                              jnp.zeros((D_MODEL, 128 - n_gate), F32)], axis=1).astype(BF16)
    b_col = jnp.pad(b_gate_e[0], (0, 128 - n_gate)).reshape(1, 128)
    g_mix0 = row(norm_mix[0])
    wp_b = w_pool[0].astype(BF16)
    wo_e = w_out_e[0].astype(BF16)
    gain_e = row(mlstm_gain[0])
    ps_e = row(pool_scale[0])

    qkv_p, ogu_p, gc_p, gr_p = proj_even(xp, g_mix0, w_main, b_col, TM_PROMPT)
    zc = jnp.zeros((1, H_A, DK_A, DK_A), F32)
    zn = jnp.zeros((1, H_A, DK_A), F32)
    zm = jnp.zeros((1, 8, 128), F32)
    hh_p, C_p, n_p, m_p = mlstm(qkv_p[None], ogu_p[None], gc_p[None], gr_p[None], zc, zn, zm, gain_e,
                                MLSTM_CHUNK, MLSTM_CHUNK, MLSTM_CHUNKS_PER_STEP)
    pool_args_p = (hh_p[0], ogu_p, jnp.zeros((POOL_HDR, D_B), F32), wp_b, ps_e, wo_e)
    pool_p = ogu_p[Tp - POOL_BUF:, D_A:][None]

    L = SAMPLE_PAD
    qkv_s, ogu_s, gc_s, gr_s = proj_even(xs, g_mix0, w_main, b_col, Ms)
    pad_t = lambda a, n: jnp.pad(a.reshape(Bs, Ts, a.shape[-1]), ((0, 0), (0, n - Ts), (0, 0)))
    gr_s3 = jnp.pad(gr_s.reshape(8, Bs, Ts).transpose(1, 0, 2), ((0, 0), (0, 0), (0, L - Ts)))
    m0_s = jnp.broadcast_to(jnp.pad(state_mlstm_m[0], ((0, 0), (0, 8 - H_A)))[:, :, None], (Bs, 8, 128))
    ogu_s3 = pad_t(ogu_s, L)
    hh_s, C_s, n_s, m_s = mlstm(pad_t(qkv_s, L), ogu_s3, pad_t(gc_s, L), gr_s3,
                                state_mlstm_C[0], state_mlstm_n[0], m0_s, gain_e, L, Ts, 1)
    prev16 = jnp.pad(state_pool[0], ((0, 0), (POOL_HDR - POOL_BUF, 0), (0, 0)))
    xs = pool_out(hh_s, ogu_s3, prev16, wp_b, ps_e, wo_e, pad_t(xs, L), L, PAST_LEN)[:, :Ts].reshape(Ms, D_MODEL)
    pool_s = jnp.concatenate([state_pool[0], ogu_s[:, D_A:].reshape(Bs, Ts, D_B)], axis=1)[:, -POOL_BUF:]

    wu = w_up.astype(BF16)
    wd = w_down.astype(BF16)

    def run_ffn(l, xp, xs, final_norm, mix_p=None, mix_s=None, w_mix=None, pool_p=None):
        g = row(norm_ffn[l])
        cb = row(conv_b[l])
        gf = row(norm_final)
        zp = jnp.zeros((8, D_FF), F32)
        xp, st_p = ffn(xp, g, wu, conv_w[l], cb, wd, zp, zp, gf, TM_FFN, TM_FFN, True, final_norm, l, mix_p, w_mix,
                       pool_p)
        st = state_ffn_conv[l]
        z1 = jnp.zeros((Bs, 1, D_FF), F32)
        p1 = jnp.concatenate([st[:, 1:2], z1, z1, z1], axis=1).reshape(Ms, D_FF)
        p2 = jnp.concatenate([st[:, 0:1], st[:, 1:2], z1, z1], axis=1).reshape(Ms, D_FF)
        xs, a_s = ffn(xs, g, wu, conv_w[l], cb, wd, p1, p2, gf, Ms, Ts, False, final_norm, l, mix_s, w_mix)
        conv_p = st_p[8 - (CONV_W - 1):][None]
        conv_s = a_s.reshape(Bs, Ts, D_FF)[:, Ts - (CONV_W - 1):]
        return xp, xs, conv_p, conv_s

    xp, xs, conv_p0, conv_s0 = run_ffn(0, xp, xs, False, pool_p=pool_args_p)

    lam_init = 0.8 - 0.6 * math.exp(-0.3 * 1)
    lam = (jnp.exp(jnp.sum(lambda_q1[0] * lambda_k1[0])) - jnp.exp(jnp.sum(lambda_q2[0] * lambda_k2[0])) + lam_init).astype(F32).reshape(1)
    out_scale = 1.0 - lam_init
    g_mix1 = row(norm_mix[1])
    w_qkv = w_in_o[0].astype(BF16)
    wo_o = w_out_o[0].astype(BF16)
    gain_o = row(subln_gain[0])

    q2_p, kf_p, vf_p, kb_p, vt_p = proj_odd(xp, g_mix1, w_qkv, TM_PROMPT, True)
    o_p = attn_prompt(lam, q2_p, kb_p, vt_p, _bias_blocks_t(rel_bias), gain_o.reshape(DV_C, 1), TQ, out_scale)

    q2_s, kf_s, vf_s, kb_s, vb_s = proj_odd(xs, g_mix1, w_qkv, Ms, False)
    qm = q2_s.reshape(2, Bs, Ts, H_C, DV_C).transpose(1, 3, 0, 2, 4)
    qm = jnp.pad(qm, ((0, 0), (0, 0), (0, 0), (0, 8 - Ts), (0, 0)))
    qm = qm.reshape(Bs, H_C * ROWS_PER_HEAD, DV_C)
    tok = np.minimum(np.arange(8), Ts - 1)
    tok = np.tile(tok, 2)
    ccol = np.arange(PAGE_SIZE)
    rel_last = PAGE_SIZE + tok[:, None] - ccol[None, :]
    rel_new = np.where(ccol[None, :] < Ts, tok[:, None] - ccol[None, :], -1)
    bias_last = _rel_bias_minus_far(rel_bias, rel_last).reshape(H_C * ROWS_PER_HEAD, PAGE_SIZE)
    bias_last = jnp.pad(bias_last, ((0, 0), ((PAGE_GROUP - 1) * PAGE_SIZE, 0)))
    bias_new = _rel_bias_minus_far(rel_bias, rel_new).reshape(H_C * ROWS_PER_HEAD, PAGE_SIZE)
    n_phys = cache_k.shape[1]
    ck = cache_k[0].reshape(n_phys, PAGE_SIZE * H_C, DV_C)
    cv = cache_v[0].reshape(n_phys, PAGE_SIZE * H_C, DV_C)
    kn = jnp.pad(kb_s.reshape(Bs, Ts, D_MODEL), ((0, 0), (0, PAGE_SIZE - Ts), (0, 0)))
    vn = jnp.pad(vb_s.reshape(Bs, Ts, D_MODEL), ((0, 0), (0, PAGE_SIZE - Ts), (0, 0)))
    o_s = attn_sample(page_table, lam, qm, ck, cv, kn, vn, bias_last, bias_new, gain_o, out_scale)
    o_s = o_s.reshape(Bs, H_C, 8, DV_C)[:, :, :Ts].transpose(0, 2, 1, 3).reshape(Ms, D_MODEL).astype(BF16)

    yp, ys, conv_p1, conv_s1 = run_ffn(1, xp, xs, True, o_p, o_s, wo_o)

    y_prompt = yp.reshape(Bp, Tp, D_MODEL)
    y_sample = ys.reshape(Bs, Ts, D_MODEL)
    new_m_p = m_p[:, :H_A, 0]
    new_m_s = m_s[:, :H_A, 0]
    new_k_p = kf_p.reshape(1, Bp, Tp, H_C, DV_C)
    new_v_p = vf_p.reshape(1, Bp, Tp, H_C, DV_C)
    new_k_s = kf_s.reshape(1, Bs, Ts, H_C, DV_C)
    new_v_s = vf_s.reshape(1, Bs, Ts, H_C, DV_C)
    return (y_prompt, y_sample,
            C_p[None], n_p[None], new_m_p[None], pool_p[None], new_k_p, new_v_p,
            jnp.stack([conv_p0, conv_p1]),
            C_s[None], n_s[None], new_m_s[None], pool_s[None], new_k_s, new_v_s,
            jnp.stack([conv_s0, conv_s1]))
```

```python
import functools
import math

import numpy as np
import jax
import jax.numpy as jnp
from jax import lax
from jax.experimental import pallas as pl
from jax.experimental.pallas import tpu as pltpu

F32 = jnp.float32
BF16 = jnp.bfloat16
HIGHEST = lax.Precision.HIGHEST

D_MODEL = 1024
PAST_LEN = 16384
PAGE_SIZE = 128
D_A = 512
H_A = 4
DK_A = 128
MLSTM_CHUNK = 128
MLSTM_CHUNKS_PER_STEP = 8
D_B = 512
POOL_WINDOWS = (2, 4, 8, 16)
G_B = 128
POOL_BUF = 15
POOL_HDR = 16
H_C = 8
DC = 64
DV_C = 128
N_BUCKETS = 32
MAX_DIST = 128
LOG2E = math.log2(math.e)
SCORE_SCALE = DC ** -0.5 * LOG2E
D_FF = 2816
CONV_W = 3
EPS = 1e-6

VMEM_LIMIT = 56 * 1024 * 1024
NEG_INF = float("-inf")


def _cparams(sem):
    return pltpu.CompilerParams(dimension_semantics=sem, vmem_limit_bytes=VMEM_LIMIT)


def _const_spec(shape):
    nd = len(shape)
    return pl.BlockSpec(shape, lambda *_: (0,) * nd, pipeline_mode=pl.Buffered(1))


def _rms(x, g):
    return x * lax.rsqrt(jnp.mean(x * x, axis=-1, keepdims=True) + EPS) * g


def _dot(a, b):
    return jnp.dot(a, b, preferred_element_type=F32)


def _dot_nt(a, b):
    return lax.dot_general(a, b, (((1,), (1,)), ((), ())), preferred_element_type=F32)


def _dot_tn(a, b):
    return lax.dot_general(a, b, (((0,), (0,)), ((), ())), preferred_element_type=F32)


def _log_sigmoid(x):
    return jnp.minimum(x, 0.0) - jnp.log1p(jnp.exp(-jnp.abs(x)))


def _sigmoid(x):
    return 1.0 / (1.0 + jnp.exp(-x))


def _proj_even_kernel(x_ref, g_ref, w_ref, bc_ref, qkv_ref, ogu_ref, gc_ref, gr_ref):
    h = _rms(x_ref[...], g_ref[...])
    z = _dot(h.astype(BF16), w_ref[...])
    qkv_ref[:, 0:D_A] = z[:, 0:D_A].astype(BF16)
    qkv_ref[:, D_A:2 * D_A] = (z[:, D_A:2 * D_A] * (DK_A ** -0.5)).astype(BF16)
    qkv_ref[:, 2 * D_A:3 * D_A] = z[:, 2 * D_A:3 * D_A].astype(BF16)
    ogu_ref[...] = z[:, 3 * D_A:3 * D_A + D_A + D_B]
    gc = z[:, 3 * D_A + D_A + D_B:] + bc_ref[...]
    gc_ref[...] = gc
    gr_ref[...] = gc.T[0:8, :]


def proj_even(x, g, w_main, b_col, tm):
    M = x.shape[0]
    n_main = w_main.shape[1]
    return pl.pallas_call(
        _proj_even_kernel,
        grid=(M // tm,),
        in_specs=[
            pl.BlockSpec((tm, D_MODEL), lambda i: (i, 0)),
            _const_spec((1, D_MODEL)),
            _const_spec((D_MODEL, n_main)),
            _const_spec((1, 128)),
        ],
        out_specs=[
            pl.BlockSpec((tm, 3 * D_A), lambda i: (i, 0)),
            pl.BlockSpec((tm, D_A + D_B), lambda i: (i, 0)),
            pl.BlockSpec((tm, 128), lambda i: (i, 0)),
            pl.BlockSpec((8, tm), lambda i: (0, i)),
        ],
        out_shape=[
            jax.ShapeDtypeStruct((M, 3 * D_A), BF16),
            jax.ShapeDtypeStruct((M, D_A + D_B), F32),
            jax.ShapeDtypeStruct((M, 128), F32),
            jax.ShapeDtypeStruct((8, M), F32),
        ],
        compiler_params=_cparams(("arbitrary",)),
        name="proj_even",
    )(x, g, w_main, b_col)


def _mlstm_kernel(qkv_ref, og_ref, gc_ref, gr_ref, c0_ref, n0_ref, m0_ref, gain_ref,
                  hh_ref, c_out_ref, n_out_ref, m_out_ref, c_s, n_s, m_s, *, L, valid, chunks, seqs):
    c = pl.program_id(1)

    @pl.when(c == 0)
    def _():
        c_s[...] = c0_ref[...]
        n_s[...] = n0_ref[...]
        m_s[...] = m0_ref[...]

    row = lax.broadcasted_iota(jnp.int32, (L, L), 0)
    col = lax.broadcasted_iota(jnp.int32, (L, L), 1)
    tri = (col <= row).astype(F32)
    mask = (col <= row) & (col < valid)
    rvalid = lax.broadcasted_iota(jnp.int32, (L, 1), 0) < valid
    cvalid = lax.broadcasted_iota(jnp.int32, (1, L), 1) < valid

    c_st = [[c_s[sq, h] for h in range(H_A)] for sq in range(seqs)]
    n_st = [[n_s[sq, h:h + 1, :] for h in range(H_A)] for sq in range(seqs)]
    m_st = [[m_s[sq, h:h + 1, 0:1] for h in range(H_A)] for sq in range(seqs)]

    for sq, ci in [(a, b) for a in range(seqs) for b in range(chunks)]:
        rows = slice(ci * L, (ci + 1) * L)
        gcol = gc_ref[sq, rows, :]
        grow = gr_ref[sq, :, rows]
        lf_col = jnp.where(rvalid, _log_sigmoid(gcol), 0.0)
        lf_row = jnp.where(cvalid, _log_sigmoid(grow), 0.0)
        b_col_all = jnp.dot(tri, lf_col, precision=HIGHEST, preferred_element_type=F32)
        b_row_all = lax.dot_general(lf_row, tri, (((1,), (1,)), ((), ())), precision=HIGHEST,
                                    preferred_element_type=F32)
        for h in range(H_A):
            bc = b_col_all[:, H_A + h:H_A + h + 1]
            br = b_row_all[H_A + h:H_A + h + 1, :]
            igc = gcol[:, h:h + 1]
            igr = grow[h:h + 1, :]
            m0 = m_st[sq][h]
            logd = jnp.where(mask, bc - br + igr, NEG_INF)
            log_inter = bc + m0
            m_t = jnp.maximum(log_inter, jnp.max(logd, axis=-1, keepdims=True))
            dm = jnp.exp(logd - m_t)
            w_inter = jnp.exp(log_inter - m_t)
            q = qkv_ref[sq, rows, h * DK_A:(h + 1) * DK_A]
            k = qkv_ref[sq, rows, D_A + h * DK_A:D_A + (h + 1) * DK_A]
            v = qkv_ref[sq, rows, 2 * D_A + h * DK_A:2 * D_A + (h + 1) * DK_A]
            s = _dot_nt(q, k) * dm
            c_old = c_st[sq][h]
            n_old = n_st[sq][h]
            v_ext = jnp.concatenate([v, jnp.ones((L, DK_A), BF16)], axis=1)
            c_ext = jnp.concatenate([c_old, jnp.broadcast_to(n_old, (DK_A, DK_A))], axis=0).astype(BF16)
            both = _dot(s.astype(BF16), v_ext) + w_inter * _dot_nt(q, c_ext)
            denom = jnp.maximum(jnp.abs(both[:, DK_A:]), jnp.exp(-m_t))
            hh = both[:, :DK_A] / denom
            y = _rms(hh, gain_ref[:, h * DK_A:(h + 1) * DK_A]) * _sigmoid(og_ref[sq, rows, h * DK_A:(h + 1) * DK_A])
            hh_ref[sq, rows, h * DK_A:(h + 1) * DK_A] = y.astype(BF16)
            m_new = m_t[valid - 1:valid, :]
            b_last = bc[valid - 1:valid, :]
            w_s = jnp.where(rvalid, jnp.exp(b_last - bc + igc - m_new), 0.0)
            decay = jnp.exp(b_last + m0 - m_new)
            kf = k.astype(F32)
            vw = (v.astype(F32) * w_s).astype(BF16)
            c_st[sq][h] = decay * c_old + _dot_tn(vw, k)
            n_st[sq][h] = decay * n_old + jnp.sum(kf * w_s, axis=0, keepdims=True)
            m_st[sq][h] = m_new

    for sq in range(seqs):
        for h in range(H_A):
            c_s[sq, h] = c_st[sq][h]
            n_s[sq, h:h + 1, :] = n_st[sq][h]
            m_s[sq, h:h + 1, :] = jnp.broadcast_to(m_st[sq][h], (1, 128))

    @pl.when(c == pl.num_programs(1) - 1)
    def _():
        c_out_ref[...] = c_s[...]
        n_out_ref[...] = n_s[...]
        m_out_ref[...] = m_s[...]


def mlstm(qkv, ogu, gc, gr, c0, n0, m0, gain, L, valid, chunks, seqs):
    B, T = qkv.shape[:2]
    LB = L * chunks
    nc = T // LB
    return pl.pallas_call(
        functools.partial(_mlstm_kernel, L=L, valid=valid, chunks=chunks, seqs=seqs),
        grid=(B // seqs, nc),
        in_specs=[
            pl.BlockSpec((seqs, LB, 3 * D_A), lambda b, c: (b, c, 0)),
            pl.BlockSpec((seqs, LB, D_A), lambda b, c: (b, c, 0)),
            pl.BlockSpec((seqs, LB, 128), lambda b, c: (b, c, 0)),
            pl.BlockSpec((seqs, 8, LB), lambda b, c: (b, 0, c)),
            pl.BlockSpec((seqs, H_A, DK_A, DK_A), lambda b, c: (b, 0, 0, 0)),
            pl.BlockSpec((seqs, H_A, DK_A), lambda b, c: (b, 0, 0)),
            pl.BlockSpec((seqs, 8, 128), lambda b, c: (b, 0, 0)),
            pl.BlockSpec((1, D_A), lambda b, c: (0, 0)),
        ],
        out_specs=[
            pl.BlockSpec((seqs, LB, D_A), lambda b, c: (b, c, 0)),
            pl.BlockSpec((seqs, H_A, DK_A, DK_A), lambda b, c: (b, 0, 0, 0)),
            pl.BlockSpec((seqs, H_A, DK_A), lambda b, c: (b, 0, 0)),
            pl.BlockSpec((seqs, 8, 128), lambda b, c: (b, 0, 0)),
        ],
        out_shape=[
            jax.ShapeDtypeStruct((B, T, D_A), BF16),
            jax.ShapeDtypeStruct((B, H_A, DK_A, DK_A), F32),
            jax.ShapeDtypeStruct((B, H_A, DK_A), F32),
            jax.ShapeDtypeStruct((B, 8, 128), F32),
        ],
        scratch_shapes=[
            pltpu.VMEM((seqs, H_A, DK_A, DK_A), F32),
            pltpu.VMEM((seqs, H_A, DK_A), F32),
            pltpu.VMEM((seqs, 8, 128), F32),
        ],
        compiler_params=_cparams(("arbitrary", "arbitrary")),
        name="mlstm",
    )(qkv, ogu, gc, gr, c0, n0, m0, gain)


POOL_LEAD = 8
assert POOL_WINDOWS == tuple(2 ** (g + 1) for g in range(len(POOL_WINDOWS)))


def _pool_scratch(tm):
    rows = POOL_LEAD + POOL_HDR + tm
    return [pltpu.VMEM((rows, D_B), F32), pltpu.VMEM((rows, D_B - G_B), F32), pltpu.VMEM((rows, D_B - 2 * G_B), F32)]


def _even_mixer_out(hh, u, prev, wp_ref, ps_ref, wo_ref, bufs, t, *, tm, nt, pos0):
    e_s, pa_s, pb_s = bufs
    H = POOL_HDR
    lo = POOL_LEAD
    top = lo + H + tm
    out0 = H

    @pl.when(t == 0)
    def _():
        e_s[0:lo, :] = jnp.zeros((lo, e_s.shape[1]), F32)
        pa_s[0:lo, :] = jnp.zeros((lo, pa_s.shape[1]), F32)
        pb_s[0:lo, :] = jnp.zeros((lo, pb_s.shape[1]), F32)
        e_s[lo:lo + H, :] = prev

    if nt > 1:
        @pl.when(t > 0)
        def _():
            e_s[lo:lo + H, :] = e_s[lo + tm:lo + tm + H, :]

    e_s[lo + H:top, :] = u
    pos = pos0 + t * tm + lax.broadcasted_iota(jnp.int32, (tm, 1), 0)
    ys = []
    src, dst = e_s, pa_s
    for g, w in enumerate(POOL_WINDOWS):
        ncol = D_B - g * G_B
        shift = w // 2
        sums = src[lo:top, 0:ncol] + src[lo - shift:top - shift, 0:ncol]
        if g + 1 < len(POOL_WINDOWS):
            dst[lo:top, 0:ncol - G_B] = sums[:, G_B:]
        cur = e_s[lo + H:top, g * G_B:(g + 1) * G_B]
        cnt = jnp.minimum(pos + 1, w).astype(F32)
        pooled = sums[out0:, 0:G_B] / cnt - cur
        ys.append(_dot(pooled.astype(BF16), wp_ref[g]))
        src, dst = dst, (pb_s if dst is pa_s else pa_s)
    yb = jnp.concatenate(ys, axis=-1) * ps_ref[...]
    return _dot(hh, wo_ref[0:D_A, :]) + _dot(yb.astype(BF16), wo_ref[D_A:, :])


def _pool_out_kernel(hh_ref, u_ref, prev_ref, wp_ref, ps_ref, wo_ref, x_ref, o_ref, *bufs, tm, nt, pos0):
    y = _even_mixer_out(hh_ref[0], u_ref[0], prev_ref[0], wp_ref, ps_ref, wo_ref, bufs, pl.program_id(1),
                        tm=tm, nt=nt, pos0=pos0)
    o_ref[0] = x_ref[0] + y


def pool_out(hh, ogu, prev16, w_pool, pool_scale, w_out, x, tm, pos0):
    B, T = x.shape[:2]
    nt = T // tm
    return pl.pallas_call(
        functools.partial(_pool_out_kernel, tm=tm, nt=nt, pos0=pos0),
        grid=(B, nt),
        in_specs=[
            pl.BlockSpec((1, tm, D_A), lambda b, t: (b, t, 0)),
            pl.BlockSpec((1, tm, D_B), lambda b, t: (b, t, 1)),
            pl.BlockSpec((1, POOL_HDR, D_B), lambda b, t: (b, 0, 0)),
            _const_spec((len(POOL_WINDOWS), G_B, G_B)),
            _const_spec((1, D_B)),
            _const_spec((D_A + D_B, D_MODEL)),
            pl.BlockSpec((1, tm, D_MODEL), lambda b, t: (b, t, 0)),
        ],
        out_specs=pl.BlockSpec((1, tm, D_MODEL), lambda b, t: (b, t, 0)),
        out_shape=jax.ShapeDtypeStruct((B, T, D_MODEL), F32),
        scratch_shapes=_pool_scratch(tm),
        compiler_params=_cparams(("arbitrary", "arbitrary")),
        name="pool_out",
    )(hh, ogu, prev16, w_pool, pool_scale, w_out, x)


def _gelu_tanh(y):
    return 0.5 * y * (1.0 + jnp.tanh(math.sqrt(2.0 / math.pi) * (y + 0.044715 * (y * y * y))))


FFN_CHUNKS = (1024, 1024, 768)
assert sum(FFN_CHUNKS) == D_FF


def _ffn_kernel(x_ref, g_ref, wup_ref, cw_ref, cb_ref, wdn_ref, p1_ref, p2_ref, gf_ref, *refs,
                tm, nt, seq_len, carried, final_norm, mixer):
    i = pl.program_id(0)
    x = x_ref[...]
    if mixer == "dense":
        mix_ref, wmix_ref, o_ref, st_ref, carry_s, a_s, g_s = refs
        x = x + _dot(mix_ref[...], wmix_ref[...])
    elif mixer == "pool":
        hh_ref, u_ref, prev_ref, wp_ref, ps_ref, wmix_ref, o_ref, st_ref, carry_s, a_s, g_s, *pool_bufs = refs
        x = x + _even_mixer_out(hh_ref[...], u_ref[...], prev_ref[...], wp_ref, ps_ref, wmix_ref, pool_bufs, i,
                                tm=tm, nt=nt, pos0=0)
    else:
        o_ref, st_ref, carry_s, a_s, g_s = refs
    offs = [sum(FFN_CHUNKS[:c]) for c in range(len(FFN_CHUNKS) + 1)]
    nch = len(FFN_CHUNKS)
    h = _rms(x, g_ref[...]).astype(BF16)
    t = lax.broadcasted_iota(jnp.int32, (tm, 1), 0) % seq_len

    if carried:
        @pl.when(i == 0)
        def _():
            carry_s[...] = jnp.zeros_like(carry_s)

    def up(c):
        w = FFN_CHUNKS[c]
        a_s[c % 2, :, 0:w] = _dot(h, wup_ref[:, offs[c]:offs[c + 1]])
        g_s[c % 2, :, 0:w] = _dot(h, wup_ref[:, D_FF + offs[c]:D_FF + offs[c + 1]])

    def act_down(c):
        cols = slice(offs[c], offs[c + 1])
        w = FFN_CHUNKS[c]
        a = a_s[c % 2, :, 0:w]
        s1 = jnp.where(t >= 1, pltpu.roll(a, 1, 0), 0.0)
        s2 = jnp.where(t >= 2, pltpu.roll(a, 2, 0), 0.0)
        if carried:
            prev0 = carry_s[6:7, cols]
            prev1 = carry_s[7:8, cols]
            s1 = s1 + jnp.where(t == 0, prev1, 0.0)
            s2 = s2 + jnp.where(t == 0, prev0, 0.0) + jnp.where(t == 1, prev1, 0.0)
            carry_s[:, cols] = a[tm - 8:tm, :]
            st_ref[:, cols] = a[tm - 8:tm, :]
        else:
            s1 = s1 + p1_ref[:, cols]
            s2 = s2 + p2_ref[:, cols]
            st_ref[:, cols] = a
        y = cb_ref[:, cols] + cw_ref[0:1, cols] * s2 + cw_ref[1:2, cols] * s1 + cw_ref[2:3, cols] * a
        act = (_gelu_tanh(y) * g_s[c % 2, :, 0:w]).astype(BF16)
        return _dot(act, wdn_ref[cols, :])

    up(0)
    for c in range(nch):
        if c + 1 < nch:
            up(c + 1)
        d = act_down(c)
        if c == 0:
            o_ref[...] = x + d
        else:
            o_ref[...] += d
    if final_norm:
        o_ref[...] = _rms(o_ref[...], gf_ref[...])


def _layer_spec(shape, layer):
    nd = len(shape)
    return pl.BlockSpec((None,) + tuple(shape), lambda *_: (layer,) + (0,) * nd, pipeline_mode=pl.Buffered(1))


def ffn(x, g, w_up, conv_w, conv_b, w_down, p1, p2, g_final, tm, seq_len, carried, final_norm, layer,
        mix=None, w_mix=None, pool=None):
    M = x.shape[0]
    nt = M // tm
    st_rows = 8 if carried else tm
    st_total = 8 if carried else M
    row_spec = lambda n: pl.BlockSpec((tm, n), lambda i: (i, 0))
    p_spec = _const_spec((8, D_FF)) if carried else row_spec(D_FF)
    mixer, mix_specs, mix_args, mix_scratch = None, [], [], []
    if mix is not None:
        mixer = "dense"
        mix_specs = [row_spec(D_MODEL), _const_spec((D_MODEL, D_MODEL))]
        mix_args = [mix, w_mix]
    elif pool is not None:
        assert carried
        mixer = "pool"
        mix_specs = [
            row_spec(D_A),
            pl.BlockSpec((tm, D_B), lambda i: (i, 1)),
            _const_spec((POOL_HDR, D_B)),
            _const_spec((len(POOL_WINDOWS), G_B, G_B)),
            _const_spec((1, D_B)),
            _const_spec((D_A + D_B, D_MODEL)),
        ]
        mix_args = list(pool)
        mix_scratch = _pool_scratch(tm)
    return pl.pallas_call(
        functools.partial(_ffn_kernel, tm=tm, nt=nt, seq_len=seq_len, carried=carried, final_norm=final_norm,
                          mixer=mixer),
        grid=(nt,),
        in_specs=[
            row_spec(D_MODEL),
            _const_spec((1, D_MODEL)),
            _layer_spec((D_MODEL, 2 * D_FF), layer),
            _const_spec((CONV_W, D_FF)),
            _const_spec((1, D_FF)),
            _layer_spec((D_FF, D_MODEL), layer),
            p_spec,
            p_spec,
            _const_spec((1, D_MODEL)),
            *mix_specs,
        ],
        out_specs=[
            row_spec(D_MODEL),
            pl.BlockSpec((st_rows, D_FF), (lambda i: (0, 0)) if carried else (lambda i: (i, 0))),
        ],
        out_shape=[
            jax.ShapeDtypeStruct((M, D_MODEL), F32),
            jax.ShapeDtypeStruct((st_total, D_FF), F32),
        ],
        scratch_shapes=[
            pltpu.VMEM((8, D_FF), F32),
            pltpu.VMEM((2, tm, max(FFN_CHUNKS)), F32),
            pltpu.VMEM((2, tm, max(FFN_CHUNKS)), F32),
            *mix_scratch,
        ],
        compiler_params=_cparams(("arbitrary",)),
        name="ffn",
    )(x, g, w_up, conv_w, conv_b, w_down, p1, p2, g_final, *mix_args)


def _proj_odd_kernel(x_ref, g_ref, w_ref, q_ref, kf_ref, vf_ref, kb_ref, vb_ref, *, v_transposed):
    tm = x_ref.shape[0]
    h = _rms(x_ref[...], g_ref[...]).astype(BF16)
    q = _dot(h, w_ref[:, 0:D_MODEL]) * SCORE_SCALE
    lane = lax.broadcasted_iota(jnp.int32, q.shape, 1) % DV_C
    q_ref[0] = jnp.where(lane < DC, q, 0.0).astype(BF16)
    q_ref[1] = jnp.where(lane >= DC, q, 0.0).astype(BF16)
    k = _dot(h, w_ref[:, D_MODEL:2 * D_MODEL])
    kf_ref[...] = k
    kb_ref[...] = k.astype(BF16)
    v = _dot(h, w_ref[:, 2 * D_MODEL:])
    vf_ref[...] = v
    if v_transposed:
        rows = DV_C + ATTN_ONES_ROWS
        for hd in range(H_C):
            vb_ref[hd * rows:hd * rows + DV_C, :] = v[:, hd * DV_C:(hd + 1) * DV_C].T.astype(BF16)
            vb_ref[hd * rows + DV_C:(hd + 1) * rows, :] = jnp.ones((ATTN_ONES_ROWS, tm), BF16)
    else:
        vb_ref[...] = v.astype(BF16)


def proj_odd(x, g, w, tm, v_transposed):
    M = x.shape[0]
    vt_rows = H_C * (DV_C + ATTN_ONES_ROWS)
    row_spec = pl.BlockSpec((tm, D_MODEL), lambda i: (i, 0))
    return pl.pallas_call(
        functools.partial(_proj_odd_kernel, v_transposed=v_transposed),
        grid=(M // tm,),
        in_specs=[row_spec, _const_spec((1, D_MODEL)), _const_spec((D_MODEL, 3 * D_MODEL))],
        out_specs=[pl.BlockSpec((2, tm, D_MODEL), lambda i: (0, i, 0))] + [row_spec] * 3
        + [pl.BlockSpec((vt_rows, tm), lambda i: (0, i)) if v_transposed else row_spec],
        out_shape=[
            jax.ShapeDtypeStruct((2, M, D_MODEL), BF16),
            jax.ShapeDtypeStruct((M, D_MODEL), F32),
            jax.ShapeDtypeStruct((M, D_MODEL), F32),
            jax.ShapeDtypeStruct((M, D_MODEL), BF16),
            jax.ShapeDtypeStruct((vt_rows, M) if v_transposed else (M, D_MODEL), BF16),
        ],
        compiler_params=_cparams(("arbitrary",)),
        name="proj_odd",
    )(x, g, w)


ATTN_STRIP = 256
ATTN_KEY_CHUNK = 512
ATTN_ONES_ROWS = 16


def _attn_prompt_kernel(it_ref, jt_ref, lam_ref, q_ref, k_ref, vt_ref, bias_ref, gain_ref, o_ref,
                        m_s, acc_s, s_s, *, tq, out_scale):
    i = it_ref[pl.program_id(1)]
    j = jt_ref[pl.program_id(1)]
    W = ATTN_STRIP
    KC = ATTN_KEY_CHUNK
    nstrip = tq // W

    @pl.when(j == 0)
    def _():
        m_s[...] = jnp.full_like(m_s, NEG_INF)
        acc_s[...] = jnp.zeros_like(acc_s)

    def tile(kind):
        strips = [(mp, rb) for mp in range(2) for rb in range(nstrip)]

        def nkeys(rb):
            return (rb + 1) * W if kind == 0 else tq

        def key_chunks(rb):
            nk = nkeys(rb)
            return [(lo, min(lo + KC, nk)) for lo in range(0, nk, KC)]

        def scores(idx):
            mp, rb = strips[idx]
            qs = q_ref[mp, rb * W:(rb + 1) * W, :]
            bounds = key_chunks(rb)
            chunks = [_dot_nt(k_ref[lo:hi, :], qs) for lo, hi in bounds]

            def add_bias(key_block, b):
                b_lo, b_hi = key_block * W, (key_block + 1) * W
                for c, (lo, hi) in enumerate(bounds):
                    o_lo, o_hi = max(lo, b_lo), min(hi, b_hi)
                    if o_lo >= o_hi:
                        continue
                    parts = [chunks[c][0:o_lo - lo], chunks[c][o_lo - lo:o_hi - lo] + b[o_lo - b_lo:o_hi - b_lo, :],
                             chunks[c][o_hi - lo:hi - lo]]
                    parts = [p for p in parts if p.shape[0] > 0]
                    chunks[c] = jnp.concatenate(parts, axis=0) if len(parts) > 1 else parts[0]

            if kind == 0:
                add_bias(rb, bias_ref[0, 0])
                if rb >= 1:
                    add_bias(rb - 1, bias_ref[0, 1])
            elif kind == 1 and rb == 0:
                add_bias(nstrip - 1, bias_ref[0, 1])
            for (lo, hi), s in zip(bounds, chunks):
                s_s[idx % 2, lo:hi, :] = s

        def consume(idx):
            mp, rb = strips[idx]
            nchunks = len(key_chunks(rb))
            cols = slice(mp * tq + rb * W, mp * tq + (rb + 1) * W)
            chunks = [s_s[idx % 2, lo:hi, :] for lo, hi in key_chunks(rb)]
            m_old = m_s[:, cols]
            m_new = m_old
            for s in chunks:
                m_new = jnp.maximum(m_new, jnp.max(s, axis=0, keepdims=True))
            alpha = jnp.exp2(m_old - m_new)
            ps = [jnp.exp2(s - m_new).astype(BF16) for s in chunks]
            pcat = jnp.concatenate(ps, axis=0) if nchunks > 1 else ps[0]
            acc_s[:, cols] = alpha * acc_s[:, cols] + _dot(vt_ref[:, 0:nkeys(rb)], pcat)
            m_s[:, cols] = m_new

        scores(0)
        for idx in range(len(strips)):
            if idx + 1 < len(strips):
                scores(idx + 1)
            consume(idx)

    @pl.when(j < i - 1)
    def _():
        tile(2)

    @pl.when(j == i - 1)
    def _():
        tile(1)

    @pl.when(j == i)
    def _():
        tile(0)
        n = acc_s[0:DV_C, :] / acc_s[DV_C:DV_C + 1, :]
        o = n[:, 0:tq] - lam_ref[0] * n[:, tq:]
        o = o * lax.rsqrt(jnp.mean(o * o, axis=0, keepdims=True) + EPS) * gain_ref[...] * out_scale
        o_ref[...] = o.T.astype(BF16)


def attn_prompt(lam, q2, k, vt, bias, gain_col, tq, out_scale):
    T = k.shape[0]
    nq = T // tq
    pairs = [(i, j) for i in range(nq) for j in range(i + 1)]
    itab = jnp.asarray(np.array([p[0] for p in pairs], np.int32))
    jtab = jnp.asarray(np.array([p[1] for p in pairs], np.int32))
    grid_spec = pltpu.PrefetchScalarGridSpec(
        num_scalar_prefetch=2,
        grid=(H_C, len(pairs)),
        in_specs=[
            pl.BlockSpec(memory_space=pltpu.SMEM),
            pl.BlockSpec((2, tq, DV_C), lambda h, p, it, jt: (0, it[p], h)),
            pl.BlockSpec((tq, DV_C), lambda h, p, it, jt: (jt[p], h)),
            pl.BlockSpec((DV_C + ATTN_ONES_ROWS, tq), lambda h, p, it, jt: (h, jt[p])),
            pl.BlockSpec((1, 2, ATTN_STRIP, ATTN_STRIP), lambda h, p, it, jt: (h, 0, 0, 0)),
            pl.BlockSpec((DV_C, 1), lambda h, p, it, jt: (0, 0)),
        ],
        out_specs=pl.BlockSpec((tq, DV_C), lambda h, p, it, jt: (it[p], h)),
        scratch_shapes=[
            pltpu.VMEM((1, 2 * tq), F32),
            pltpu.VMEM((DV_C + ATTN_ONES_ROWS, 2 * tq), F32),
            pltpu.VMEM((2, tq, ATTN_STRIP), F32),
        ],
    )
    return pl.pallas_call(
        functools.partial(_attn_prompt_kernel, tq=tq, out_scale=out_scale),
        grid_spec=grid_spec,
        out_shape=jax.ShapeDtypeStruct((T, H_C * DV_C), BF16),
        compiler_params=_cparams(("arbitrary", "arbitrary")),
        name="attn_prompt",
    )(itab, jtab, lam, q2, k, vt, bias, gain_col)


PAGES_PER_STEP = 16
PAGE_GROUP = 2
ROWS_PER_HEAD = 16


def _attn_sample_kernel(pt_ref, lam_ref, q_ref, *refs, out_scale):
    P = PAGES_PER_STEP
    G = PAGE_GROUP
    R = ROWS_PER_HEAD
    ngroups = P // G
    k_refs = refs[0:P]
    v_refs = refs[P:2 * P]
    kn_ref, vn_ref, bias_last_ref, bias_new_ref, gain_ref, o_ref, m_s, l_s, acc_s, s_s = refs[2 * P:]
    j = pl.program_id(1)
    nj = pl.num_programs(1)

    @pl.when(j == 0)
    def _():
        m_s[...] = jnp.full_like(m_s, NEG_INF)
        l_s[...] = jnp.zeros_like(l_s)
        acc_s[...] = jnp.zeros_like(acc_s)

    def head_rows(page_refs, p, h):
        return page_refs[p][0, pl.ds(h, PAGE_SIZE, stride=H_C), :].astype(BF16)

    def scores(g):
        for h in range(H_C):
            kcat = jnp.concatenate([head_rows(k_refs, g * G + t, h) for t in range(G)], axis=0)
            s_s[g, h * R:(h + 1) * R, :] = _dot_nt(q_ref[0, h * R:(h + 1) * R, :], kcat)

    def update(s, v_of_head):
        m_old = m_s[...]
        m_new = jnp.maximum(m_old, jnp.max(s, axis=-1, keepdims=True))
        alpha = jnp.exp2(m_old - m_new)
        p = jnp.exp2(s - m_new).astype(BF16)
        l_s[...] = alpha * l_s[...] + jnp.sum(p.astype(F32), axis=-1, keepdims=True)
        for h in range(H_C):
            rows = slice(h * R, (h + 1) * R)
            acc_s[rows, :] = alpha[rows] * acc_s[rows, :] + _dot(p[rows, :], v_of_head(h))
        m_s[...] = m_new

    def v_group(g):
        return lambda h: jnp.concatenate([head_rows(v_refs, g * G + t, h) for t in range(G)], axis=0)

    @pl.when(j < nj - 1)
    def _():
        for g in range(ngroups):
            scores(g)
        for g in range(ngroups):
            update(s_s[g], v_group(g))

    @pl.when(j == nj - 1)
    def _():
        for g in range(ngroups):
            scores(g)
        for g in range(ngroups):
            s = s_s[g]
            if g == ngroups - 1:
                s = s + bias_last_ref[...]
            update(s, v_group(g))
        for h in range(H_C):
            kh = kn_ref[0, :, h * DV_C:(h + 1) * DV_C]
            s_s[0, h * R:(h + 1) * R, 0:PAGE_SIZE] = _dot_nt(q_ref[0, h * R:(h + 1) * R, :], kh)
        update(s_s[0, :, 0:PAGE_SIZE] + bias_new_ref[...], lambda h: vn_ref[0, :, h * DV_C:(h + 1) * DV_C])
        n = acc_s[...] / l_s[...]
        for h in range(H_C):
            o = n[h * R:h * R + 8, :] - lam_ref[0] * n[h * R + 8:(h + 1) * R, :]
            o_ref[0, h * 8:(h + 1) * 8, :] = _rms(o, gain_ref[...]) * out_scale


def attn_sample(page_table, lam, qm, cache_k, cache_v, k_new, v_new, bias_last, bias_new, gain, out_scale):
    B = qm.shape[0]
    P = PAGES_PER_STEP
    n_pages = page_table.shape[1]
    nj = n_pages // P
    rows = H_C * ROWS_PER_HEAD
    page_rows = PAGE_SIZE * H_C

    def page_spec(p):
        return pl.BlockSpec((1, page_rows, DV_C), lambda b, j, pt, p=p: (pt[b, j * P + p], 0, 0))

    grid_spec = pltpu.PrefetchScalarGridSpec(
        num_scalar_prefetch=1,
        grid=(B, nj),
        in_specs=[
            pl.BlockSpec(memory_space=pltpu.SMEM),
            pl.BlockSpec((1, rows, DV_C), lambda b, j, pt: (b, 0, 0)),
            *[page_spec(p) for p in range(P)],
            *[page_spec(p) for p in range(P)],
            pl.BlockSpec((1, PAGE_SIZE, H_C * DV_C), lambda b, j, pt: (b, 0, 0)),
            pl.BlockSpec((1, PAGE_SIZE, H_C * DV_C), lambda b, j, pt: (b, 0, 0)),
            pl.BlockSpec((rows, PAGE_GROUP * PAGE_SIZE), lambda b, j, pt: (0, 0)),
            pl.BlockSpec((rows, PAGE_SIZE), lambda b, j, pt: (0, 0)),
            pl.BlockSpec((1, DV_C), lambda b, j, pt: (0, 0)),
        ],
        out_specs=pl.BlockSpec((1, H_C * 8, DV_C), lambda b, j, pt: (b, 0, 0)),
        scratch_shapes=[
            pltpu.VMEM((rows, 1), F32),
            pltpu.VMEM((rows, 1), F32),
            pltpu.VMEM((rows, DV_C), F32),
            pltpu.VMEM((P // PAGE_GROUP, rows, PAGE_GROUP * PAGE_SIZE), F32),
        ],
    )
    return pl.pallas_call(
        functools.partial(_attn_sample_kernel, out_scale=out_scale),
        grid_spec=grid_spec,
        out_shape=jax.ShapeDtypeStruct((B, H_C * 8, DV_C), F32),
        compiler_params=_cparams(("arbitrary", "arbitrary")),
        name="attn_sample",
    )(page_table, lam, qm, *([cache_k] * P), *([cache_v] * P), k_new, v_new, bias_last, bias_new, gain)


def _t5_bucket_table():
    n = np.arange(MAX_DIST + 1)
    max_exact = N_BUCKETS // 2
    nf = np.maximum(n, 1).astype(np.float32)
    large = max_exact + (np.log(nf / max_exact) / math.log(MAX_DIST / max_exact) * (N_BUCKETS - max_exact)).astype(np.int32)
    large = np.minimum(large, N_BUCKETS - 1)
    return np.where(n < max_exact, n, large).astype(np.int32)


def _rel_bias_minus_far(rel_bias, rel):
    tab = _near_bias_table(rel_bias)
    vals = tab[np.clip(rel, 0, MAX_DIST)]
    vals = jnp.where(jnp.asarray(rel >= 0)[..., None], vals, NEG_INF)
    return jnp.moveaxis(vals, -1, 0).astype(F32)


def _near_bias_table(rel_bias):
    tab = rel_bias[_t5_bucket_table()]
    return ((tab - tab[MAX_DIST][None, :]) * LOG2E).astype(F32)


def _bias_blocks_t(rel_bias):
    W = ATTN_STRIP
    H = rel_bias.shape[1]
    f = jnp.concatenate([_near_bias_table(rel_bias), jnp.zeros((W - MAX_DIST - 1, H), F32)], axis=0).T
    g0 = jnp.concatenate([f, jnp.full((H, W), NEG_INF, F32)], axis=1)
    g1 = jnp.concatenate([jnp.zeros((H, W), F32), f], axis=1)
    g = jnp.stack([g0, g1], axis=1)
    rep = jnp.tile(g, (1, 1, W))[:, :, :W * (2 * W - 1)].reshape(H, 2, W, 2 * W - 1)
    return rep[:, :, :, :W]


TM_PROMPT = 512
TM_FFN = 512
SAMPLE_SEQS_PER_STEP = 4
SAMPLE_PAD = 16
TQ = 2048


def kernel(x_prompt, x_sample, state_mlstm_C, state_mlstm_n, state_mlstm_m, state_pool, cache_k, cache_v, state_ffn_conv, page_table, norm_mix, norm_ffn, norm_final, w_in_e, b_gate_e, mlstm_gain, w_pool, pool_scale, w_out_e, w_in_o, lambda_q1, lambda_k1, lambda_q2, lambda_k2, subln_gain, rel_bias, w_out_o, w_up, conv_w, conv_b, w_down):
    Bp, Tp = x_prompt.shape[:2]
    Bs, Ts = x_sample.shape[:2]
    assert Bp == 1
    Ms = Bs * Ts
    xp = x_prompt.reshape(Tp, D_MODEL)
    xs = x_sample.reshape(Ms, D_MODEL)
    row = lambda a: a.reshape(1, -1)

    w_in = w_in_e[0]
    n_gate = 2 * H_A
    w_main = jnp.concatenate([w_in[:, :4 * D_A], w_in[:, 4 * D_A + n_gate:], w_in[:, 4 * D_A:4 * D_A + n_gate],
                              jnp.zeros((D_MODEL, 128 - n_gate), F32)], axis=1).astype(BF16)
    b_col = jnp.pad(b_gate_e[0], (0, 128 - n_gate)).reshape(1, 128)
    g_mix0 = row(norm_mix[0])
    wp_b = w_pool[0].astype(BF16)
    wo_e = w_out_e[0].astype(BF16)
    gain_e = row(mlstm_gain[0])
    ps_e = row(pool_scale[0])

    qkv_p, ogu_p, gc_p, gr_p = proj_even(xp, g_mix0, w_main, b_col, TM_PROMPT)
    zc = jnp.zeros((1, H_A, DK_A, DK_A), F32)
    zn = jnp.zeros((1, H_A, DK_A), F32)
    zm = jnp.zeros((1, 8, 128), F32)
    hh_p, C_p, n_p, m_p = mlstm(qkv_p[None], ogu_p[None], gc_p[None], gr_p[None], zc, zn, zm, gain_e,
                                MLSTM_CHUNK, MLSTM_CHUNK, MLSTM_CHUNKS_PER_STEP, 1)
    pool_args_p = (hh_p[0], ogu_p, jnp.zeros((POOL_HDR, D_B), F32), wp_b, ps_e, wo_e)
    pool_p = ogu_p[Tp - POOL_BUF:, D_A:][None]

    L = SAMPLE_PAD
    qkv_s, ogu_s, gc_s, gr_s = proj_even(xs, g_mix0, w_main, b_col, Ms)
    pad_t = lambda a, n: jnp.pad(a.reshape(Bs, Ts, a.shape[-1]), ((0, 0), (0, n - Ts), (0, 0)))
    gr_s3 = jnp.pad(gr_s.reshape(8, Bs, Ts).transpose(1, 0, 2), ((0, 0), (0, 0), (0, L - Ts)))
    m0_s = jnp.broadcast_to(jnp.pad(state_mlstm_m[0], ((0, 0), (0, 8 - H_A)))[:, :, None], (Bs, 8, 128))
    ogu_s3 = pad_t(ogu_s, L)
    hh_s, C_s, n_s, m_s = mlstm(pad_t(qkv_s, L), ogu_s3, pad_t(gc_s, L), gr_s3,
                                state_mlstm_C[0], state_mlstm_n[0], m0_s, gain_e, L, Ts, 1, SAMPLE_SEQS_PER_STEP)
    prev16 = jnp.pad(state_pool[0], ((0, 0), (POOL_HDR - POOL_BUF, 0), (0, 0)))
    xs = pool_out(hh_s, ogu_s3, prev16, wp_b, ps_e, wo_e, pad_t(xs, L), L, PAST_LEN)[:, :Ts].reshape(Ms, D_MODEL)
    pool_s = jnp.concatenate([state_pool[0], ogu_s[:, D_A:].reshape(Bs, Ts, D_B)], axis=1)[:, -POOL_BUF:]

    wu = w_up.astype(BF16)
    wd = w_down.astype(BF16)

    def run_ffn(l, xp, xs, final_norm, mix_p=None, mix_s=None, w_mix=None, pool_p=None):
        g = row(norm_ffn[l])
        cb = row(conv_b[l])
        gf = row(norm_final)
        zp = jnp.zeros((8, D_FF), F32)
        xp, st_p = ffn(xp, g, wu, conv_w[l], cb, wd, zp, zp, gf, TM_FFN, TM_FFN, True, final_norm, l, mix_p, w_mix,
                       pool_p)
        st = state_ffn_conv[l]
        z1 = jnp.zeros((Bs, 1, D_FF), F32)
        p1 = jnp.concatenate([st[:, 1:2], z1, z1, z1], axis=1).reshape(Ms, D_FF)
        p2 = jnp.concatenate([st[:, 0:1], st[:, 1:2], z1, z1], axis=1).reshape(Ms, D_FF)
        xs, a_s = ffn(xs, g, wu, conv_w[l], cb, wd, p1, p2, gf, Ms, Ts, False, final_norm, l, mix_s, w_mix)
        conv_p = st_p[8 - (CONV_W - 1):][None]
        conv_s = a_s.reshape(Bs, Ts, D_FF)[:, Ts - (CONV_W - 1):]
        return xp, xs, conv_p, conv_s

    xp, xs, conv_p0, conv_s0 = run_ffn(0, xp, xs, False, pool_p=pool_args_p)

    lam_init = 0.8 - 0.6 * math.exp(-0.3 * 1)
    lam = (jnp.exp(jnp.sum(lambda_q1[0] * lambda_k1[0])) - jnp.exp(jnp.sum(lambda_q2[0] * lambda_k2[0])) + lam_init).astype(F32).reshape(1)
    out_scale = 1.0 - lam_init
    g_mix1 = row(norm_mix[1])
    w_qkv = w_in_o[0].astype(BF16)
    wo_o = w_out_o[0].astype(BF16)
    gain_o = row(subln_gain[0])

    q2_p, kf_p, vf_p, kb_p, vt_p = proj_odd(xp, g_mix1, w_qkv, TM_PROMPT, True)
    o_p = attn_prompt(lam, q2_p, kb_p, vt_p, _bias_blocks_t(rel_bias), gain_o.reshape(DV_C, 1), TQ, out_scale)

    q2_s, kf_s, vf_s, kb_s, vb_s = proj_odd(xs, g_mix1, w_qkv, Ms, False)
    qm = q2_s.reshape(2, Bs, Ts, H_C, DV_C).transpose(1, 3, 0, 2, 4)
    qm = jnp.pad(qm, ((0, 0), (0, 0), (0, 0), (0, 8 - Ts), (0, 0)))
    qm = qm.reshape(Bs, H_C * ROWS_PER_HEAD, DV_C)
    tok = np.minimum(np.arange(8), Ts - 1)
    tok = np.tile(tok, 2)
    ccol = np.arange(PAGE_SIZE)
    rel_last = PAGE_SIZE + tok[:, None] - ccol[None, :]
    rel_new = np.where(ccol[None, :] < Ts, tok[:, None] - ccol[None, :], -1)
    bias_last = _rel_bias_minus_far(rel_bias, rel_last).reshape(H_C * ROWS_PER_HEAD, PAGE_SIZE)
    bias_last = jnp.pad(bias_last, ((0, 0), ((PAGE_GROUP - 1) * PAGE_SIZE, 0)))
    bias_new = _rel_bias_minus_far(rel_bias, rel_new).reshape(H_C * ROWS_PER_HEAD, PAGE_SIZE)
    n_phys = cache_k.shape[1]
    ck = cache_k[0].reshape(n_phys, PAGE_SIZE * H_C, DV_C)
    cv = cache_v[0].reshape(n_phys, PAGE_SIZE * H_C, DV_C)
    kn = jnp.pad(kb_s.reshape(Bs, Ts, D_MODEL), ((0, 0), (0, PAGE_SIZE - Ts), (0, 0)))
    vn = jnp.pad(vb_s.reshape(Bs, Ts, D_MODEL), ((0, 0), (0, PAGE_SIZE - Ts), (0, 0)))
    o_s = attn_sample(page_table, lam, qm, ck, cv, kn, vn, bias_last, bias_new, gain_o, out_scale)
    o_s = o_s.reshape(Bs, H_C, 8, DV_C)[:, :, :Ts].transpose(0, 2, 1, 3).reshape(Ms, D_MODEL).astype(BF16)

    yp, ys, conv_p1, conv_s1 = run_ffn(1, xp, xs, True, o_p, o_s, wo_o)

    y_prompt = yp.reshape(Bp, Tp, D_MODEL)
    y_sample = ys.reshape(Bs, Ts, D_MODEL)
    new_m_p = m_p[:, :H_A, 0]
    new_m_s = m_s[:, :H_A, 0]
    new_k_p = kf_p.reshape(1, Bp, Tp, H_C, DV_C)
    new_v_p = vf_p.reshape(1, Bp, Tp, H_C, DV_C)
    new_k_s = kf_s.reshape(1, Bs, Ts, H_C, DV_C)
    new_v_s = vf_s.reshape(1, Bs, Ts, H_C, DV_C)
    return (y_prompt, y_sample,
            C_p[None], n_p[None], new_m_p[None], pool_p[None], new_k_p, new_v_p,
            jnp.stack([conv_p0, conv_p1]),
            C_s[None], n_s[None], new_m_s[None], pool_s[None], new_k_s, new_v_s,
            jnp.stack([conv_s0, conv_s1]))
```
